```python
import math
import jax, jax.numpy as jnp
from jax import lax
import numpy as np

D_MODEL = 1024
BATCH = 8
SEQ = 8192
DEPTH = 1
DEC_BATCH = 128
DEC_SEQ = 4
PAST_LEN = 8192
PAGE_SIZE = 128

HEAD_DIM = 64
NSA_HEADS = 8
NSA_KV_HEADS = 2
NSA_GROUP = NSA_HEADS // NSA_KV_HEADS
CMP_BLOCK = 32
CMP_STRIDE = 16
SEL_BLOCK = 64
SEL_TOPN = 16
WINDOW = 512
Q_BLOCK = 128
FORCE_SCORE = 1e6
RWKV_HEADS = 8
RWKV_DECAY_LORA = 64
RWKV_A_LORA = 64
RWKV_GATE_LORA = 128
RWKV_GN_EPS = 64e-5
N_EXPERTS = 32
TOP_K = 4
D_FF = 1024
SWIGLU_LIMIT = 7.0
SWIGLU_ALPHA = 1.702
EXPERT_BLOCK = 128
RMS_EPS = 1e-5

NSA_WIDTH = NSA_HEADS * HEAD_DIM
KV_WIDTH = NSA_KV_HEADS * HEAD_DIM
RWKV_WIDTH = RWKV_HEADS * HEAD_DIM
MIX_WIDTH = NSA_WIDTH + RWKV_WIDTH
NSA_COLS = NSA_WIDTH + 6 * KV_WIDTH + 3 * NSA_HEADS
RWKV_COLS = 3 * RWKV_WIDTH + RWKV_DECAY_LORA + RWKV_A_LORA + RWKV_GATE_LORA
IN_COLS = NSA_COLS + RWKV_COLS
RWKV_SPLITS = (RWKV_WIDTH, 2 * RWKV_WIDTH, 3 * RWKV_WIDTH, 3 * RWKV_WIDTH + RWKV_DECAY_LORA,
               3 * RWKV_WIDTH + RWKV_DECAY_LORA + RWKV_A_LORA)

kernel_name = 'hymba_nsa_rwkv7_moe_step'


def rms_norm(x, g):
    xf = x.astype(jnp.float32)
    y = xf * lax.rsqrt(jnp.mean(xf * xf, -1, keepdims=True) + RMS_EPS)
    return (y * g.astype(jnp.float32)).astype(x.dtype)


def masked_softmax(s, mask):
    s = jnp.where(mask, s.astype(jnp.float32), -jnp.inf)
    m = jnp.max(s, -1, keepdims=True)
    m = jnp.where(jnp.isfinite(m), m, 0.0)
    e = jnp.where(mask, jnp.exp(s - m), 0.0)
    return e / jnp.maximum(jnp.sum(e, -1, keepdims=True), 1e-30)


def split_nsa(z):
    b, l = z.shape[:2]
    q = z[..., :NSA_WIDTH].reshape(b, l, NSA_KV_HEADS, NSA_GROUP, HEAD_DIM)
    kv = z[..., NSA_WIDTH:NSA_WIDTH + 6 * KV_WIDTH].reshape(b, l, 3, 2, NSA_KV_HEADS, HEAD_DIM)
    gates = jax.nn.sigmoid(z[..., NSA_WIDTH + 6 * KV_WIDTH:]).reshape(b, l, NSA_KV_HEADS, NSA_GROUP, 3)
    return q, kv[:, :, 0], kv[:, :, 1], kv[:, :, 2], gates


def compress_blocks(kv, cmp_pos, w_cmp):
    b, l = kv.shape[:2]
    chunks = kv.reshape(b, l // CMP_STRIDE, CMP_STRIDE, 2, NSA_KV_HEADS, HEAD_DIM)
    lo = jnp.einsum('bnlchd,cldo->bncho', chunks + cmp_pos[None, None, :CMP_STRIDE, :, None, :], w_cmp[:, :CMP_STRIDE])
    hi = jnp.einsum('bnlchd,cldo->bncho', chunks + cmp_pos[None, None, CMP_STRIDE:, :, None, :], w_cmp[:, CMP_STRIDE:])
    return lo[:, :-1] + hi[:, 1:]


def cmp_branch(qg, t, summ):
    n = summ.shape[1]
    end = jnp.arange(n) * CMP_STRIDE + (CMP_BLOCK - 1)
    mask = end[None, :] <= t[:, None]
    s = jnp.einsum('bqhgd,bnhd->bhgqn', qg, summ[:, :, 0]) * (HEAD_DIM ** -0.5)
    p = masked_softmax(s, mask)
    o = jnp.einsum('bhgqn,bnhd->bqhgd', p.astype(qg.dtype), summ[:, :, 1])
    return o, jnp.sum(p, axis=2)


def select_blocks(p_cmp, t, n_sel):
    n_cmp = p_cmp.shape[-1]
    i = jnp.arange(n_cmp)[:, None]
    j = jnp.arange(n_sel)[None, :]
    overlap = jnp.clip(jnp.minimum(i * CMP_STRIDE + CMP_BLOCK, (j + 1) * SEL_BLOCK)
                       - jnp.maximum(i * CMP_STRIDE, j * SEL_BLOCK), 0, None)
    score = jnp.einsum('bhqn,nj->bhqj', p_cmp, overlap.astype(jnp.float32) / CMP_BLOCK)
    tb = (t // SEL_BLOCK)[:, None]
    forced = (j == 0) | (j == tb) | (j == tb - 1)
    score = jnp.where(forced, FORCE_SCORE, score)
    score = jnp.where(j <= tb, score, -1.0)
    top_s, idx = lax.top_k(score, min(SEL_TOPN, n_sel))
    return idx, top_s >= 0.0


def slc_branch(qg, t, kv_sel, idx, ok):
    b, h, q, n = idx.shape
    kpos = idx[..., None] * SEL_BLOCK + jnp.arange(SEL_BLOCK)
    mask = (ok[..., None] & (kpos <= t[None, None, :, None, None])).reshape(b, h, 1, q, n * SEL_BLOCK)
    s = jnp.einsum('bqhgd,bhqnld->bhgqnl', qg, kv_sel[..., 0, :]) * (HEAD_DIM ** -0.5)
    p = masked_softmax(s.reshape(b, h, NSA_GROUP, q, n * SEL_BLOCK), mask)
    v = kv_sel[..., 1, :].reshape(b, h, q, n * SEL_BLOCK, HEAD_DIM)
    return jnp.einsum('bhgqk,bhqkd->bqhgd', p.astype(qg.dtype), v)


def win_branch(qg, t, kv_w, kpos):
    d = t[:, None] - kpos[None, :]
    mask = (d >= 0) & (d < WINDOW) & (kpos[None, :] >= 0)
    s = jnp.einsum('bqhgd,bkhd->bhgqk', qg, kv_w[:, :, 0]) * (HEAD_DIM ** -0.5)
    p = masked_softmax(s, mask)
    return jnp.einsum('bhgqk,bkhd->bqhgd', p.astype(qg.dtype), kv_w[:, :, 1])


def gate_merge(o_c, o_s, o_w, g):
    o = o_c * g[..., 0:1] + o_s * g[..., 1:2] + o_w * g[..., 2:3]
    return o.reshape(o.shape[0], o.shape[1], NSA_WIDTH)


def nsa_prompt(z, cmp_pos, w_cmp):
    q, kv_cmp, kv_slc, kv_win, gates = split_nsa(z)
    b, s = z.shape[:2]
    summ = compress_blocks(kv_cmp, cmp_pos, w_cmp)
    n_sel = s // SEL_BLOCK
    sel_blocks = kv_slc.reshape(b, n_sel, SEL_BLOCK, 2, NSA_KV_HEADS, HEAD_DIM)
    win_pad = jnp.pad(kv_win, ((0, 0), (WINDOW, 0), (0, 0), (0, 0), (0, 0)))
    b_ix = jnp.arange(b)[:, None, None, None]
    h_ix = jnp.arange(NSA_KV_HEADS)[None, :, None, None]

    def query_block(qb):
        start = qb * Q_BLOCK
        t = start + jnp.arange(Q_BLOCK)
        qg = lax.dynamic_slice_in_dim(q, start, Q_BLOCK, axis=1)
        g = lax.dynamic_slice_in_dim(gates, start, Q_BLOCK, axis=1)
        o_c, p_c = cmp_branch(qg, t, summ)
        idx, ok = select_blocks(p_c, t, n_sel)
        o_s = slc_branch(qg, t, sel_blocks[b_ix, idx, :, :, h_ix, :], idx, ok)
        kv_w = lax.dynamic_slice_in_dim(win_pad, start, Q_BLOCK + WINDOW, axis=1)
        o_w = win_branch(qg, t, kv_w, start - WINDOW + jnp.arange(Q_BLOCK + WINDOW))
        return gate_merge(o_c, o_s, o_w, g)

    o = lax.map(query_block, jnp.arange(s // Q_BLOCK))
    o = jnp.moveaxis(o, 0, 1).reshape(b, s, NSA_WIDTH)
    return o, (kv_cmp, kv_slc, kv_win[:, s - min(WINDOW, s):])


def nsa_sample(z, cache_cmp, cache_slc, page_table, state_win, cmp_pos, w_cmp):
    q, kv_cmp, kv_slc, kv_win, gates = split_nsa(z)
    b, n = z.shape[:2]
    n_pages = page_table.shape[1]
    past = n_pages * PAGE_SIZE
    total = past + n
    t = past + jnp.arange(n)
    b_ix = jnp.arange(b)[:, None, None, None]
    h_ix = jnp.arange(NSA_KV_HEADS)[None, :, None, None]
    past_cmp = cache_cmp[page_table].reshape(b, past, 2, NSA_KV_HEADS, HEAD_DIM)
    pad = (-total) % CMP_STRIDE
    full = jnp.concatenate([past_cmp, kv_cmp, jnp.zeros((b, pad, 2, NSA_KV_HEADS, HEAD_DIM), kv_cmp.dtype)], axis=1)
    summ = compress_blocks(full, cmp_pos, w_cmp)
    o_c, p_c = cmp_branch(q, t, summ)
    n_sel = -(-total // SEL_BLOCK)
    idx, ok = select_blocks(p_c, t, n_sel)
    past_blocks = past // SEL_BLOCK
    bpp = PAGE_SIZE // SEL_BLOCK
    pool = cache_slc.reshape(cache_slc.shape[0], bpp, SEL_BLOCK, 2, NSA_KV_HEADS, HEAD_DIM)
    phys = page_table[b_ix, jnp.minimum(idx // bpp, n_pages - 1)]
    from_past = pool[phys, idx % bpp, :, :, h_ix, :]
    n_new_blk = -(-n // SEL_BLOCK)
    new_blocks = jnp.pad(kv_slc, ((0, 0), (0, n_new_blk * SEL_BLOCK - n), (0, 0), (0, 0), (0, 0)))
    new_blocks = new_blocks.reshape(b, n_new_blk, SEL_BLOCK, 2, NSA_KV_HEADS, HEAD_DIM)
    from_new = new_blocks[b_ix, jnp.clip(idx - past_blocks, 0, n_new_blk - 1), :, :, h_ix, :]
    kv_sel = jnp.where((idx >= past_blocks)[..., None, None, None], from_new, from_past)
    o_s = slc_branch(q, t, kv_sel, idx, ok)
    wb = state_win.shape[1]
    kv_w = jnp.concatenate([state_win, kv_win], axis=1)
    o_w = win_branch(q, t, kv_w, past - wb + jnp.arange(wb + n))
    o = gate_merge(o_c, o_s, o_w, gates)
    return o, (kv_cmp, kv_slc, kv_w[:, n:])


def rwkv7_time_mix(z, prev_row, wkv0, shift_mu, w0, w2, a0, a2, g2, k_k, k_a, r_k, gn_g, gn_b):
    b, l = z.shape[:2]
    zf = z.astype(jnp.float32)
    prev = jnp.concatenate([prev_row[:, None].astype(jnp.float32), zf[:, :-1]], axis=1)
    zs = zf + shift_mu * (prev - zf)
    r, k, v, wd, ad, gd = jnp.split(zs, RWKV_SPLITS, axis=-1)
    w = -jax.nn.softplus(-(w0 + jnp.tanh(wd) @ w2)) - 0.5
    decay = jnp.exp(-jnp.exp(w))
    a = jax.nn.sigmoid(a0 + ad @ a2)
    g = jax.nn.sigmoid(gd) @ g2
    heads = lambda u: u.reshape(b, l, RWKV_HEADS, HEAD_DIM)
    kk = heads(k * k_k)
    kk = kk / jnp.maximum(jnp.sqrt(jnp.sum(kk * kk, -1, keepdims=True)), 1e-12)
    k = k * (1.0 + (a - 1.0) * k_a)
    r, k, v, a, decay = heads(r), heads(k), heads(v), heads(a), heads(decay)

    def step(S, inp):
        r_t, k_t, v_t, kk_t, a_t, w_t = inp
        sa = jnp.einsum('bhij,bhj->bhi', S, -kk_t)
        S = S * w_t[:, :, None, :] + sa[..., None] * (kk_t * a_t)[:, :, None, :] + v_t[..., None] * k_t[:, :, None, :]
        return S, jnp.einsum('bhij,bhj->bhi', S, r_t)

    seq_first = lambda u: jnp.moveaxis(u, 1, 0)
    s_final, o = lax.scan(step, wkv0.astype(jnp.float32), tuple(seq_first(u) for u in (r, k, v, kk, a, decay)))
    o = seq_first(o)
    mu = jnp.mean(o, -1, keepdims=True)
    var = jnp.mean(jnp.square(o - mu), -1, keepdims=True)
    o = ((o - mu) * lax.rsqrt(var + RWKV_GN_EPS)).reshape(b, l, RWKV_WIDTH) * gn_g + gn_b
    bonus = (jnp.sum(r * k * r_k, -1, keepdims=True) * v).reshape(b, l, RWKV_WIDTH)
    out = (o + bonus) * g
    return out.astype(z.dtype), s_final, z[:, -1]


def moe_ffn(x, w_router, b_router, w_gate_up, b_gate_up, w_down, b_down):
    shape = x.shape
    xt = x.reshape(-1, shape[-1])
    n_tok = xt.shape[0]
    logits = (xt @ w_router).astype(jnp.float32) + b_router.astype(jnp.float32)
    top_v, top_e = lax.top_k(logits, TOP_K)
    gates = jax.nn.softmax(top_v, axis=-1)
    flat_e = top_e.reshape(-1)
    n_assign = n_tok * TOP_K
    order = jnp.argsort(flat_e)
    sorted_e = flat_e[order]
    counts = jnp.bincount(flat_e, length=N_EXPERTS)
    padded = (counts + EXPERT_BLOCK - 1) // EXPERT_BLOCK * EXPERT_BLOCK
    pad_end = jnp.cumsum(padded)
    pad_start = pad_end - padded
    grp_start = jnp.cumsum(counts) - counts
    dest_sorted = pad_start[sorted_e] + jnp.arange(n_assign) - grp_start[sorted_e]
    n_blocks = -(-n_assign // EXPERT_BLOCK) + N_EXPERTS
    n_rows = n_blocks * EXPERT_BLOCK
    row_token = jnp.full((n_rows,), n_tok, jnp.int32).at[dest_sorted].set((order // TOP_K).astype(jnp.int32))
    x_pad = jnp.concatenate([xt, jnp.zeros((1, shape[-1]), xt.dtype)], axis=0)
    xb = x_pad[row_token].reshape(n_blocks, EXPERT_BLOCK, shape[-1])
    blk_e = jnp.minimum(jnp.sum(pad_end[None, :] <= (jnp.arange(n_blocks) * EXPERT_BLOCK)[:, None], axis=1), N_EXPERTS - 1)

    def expert_block(args):
        xe, e = args
        hgu = xe @ w_gate_up[e] + b_gate_up[e]
        gate = jnp.minimum(hgu[:, :D_FF], SWIGLU_LIMIT)
        up = jnp.clip(hgu[:, D_FF:], -SWIGLU_LIMIT, SWIGLU_LIMIT)
        act = (up + 1.0) * gate * jax.nn.sigmoid(SWIGLU_ALPHA * gate)
        return act @ w_down[e] + b_down[e]

    yb = lax.map(expert_block, (xb, blk_e)).reshape(n_rows, shape[-1])
    dest = jnp.zeros((n_assign,), dest_sorted.dtype).at[order].set(dest_sorted)
    y = yb[dest].reshape(n_tok, TOP_K, shape[-1])
    return jnp.einsum('tkd,tk->td', y, gates.astype(y.dtype)).reshape(shape)


def setup_inputs(seed: int = 0) -> dict:
    key = jax.random.key(seed)
    ks = jax.random.split(key, 32)
    nrm = lambda k, shape, scale: scale * jax.random.normal(k, shape, jnp.float32)
    n_pages = PAST_LEN // PAGE_SIZE
    n_used = DEC_BATCH * n_pages
    n_pool = n_used + n_used // 4
    wb = min(WINDOW, PAST_LEN)
    page_table = jax.random.permutation(ks[5], n_pool)[:n_used].reshape(DEC_BATCH, n_pages).astype(jnp.int32)
    return {
        'x_prompt': nrm(ks[0], (BATCH, SEQ, D_MODEL), 1.0),
        'x_sample': nrm(ks[1], (DEC_BATCH, DEC_SEQ, D_MODEL), 1.0),
        'cache_cmp': nrm(ks[2], (DEPTH, n_pool, PAGE_SIZE, 2, NSA_KV_HEADS, HEAD_DIM), 1.0),
        'cache_slc': nrm(ks[3], (DEPTH, n_pool, PAGE_SIZE, 2, NSA_KV_HEADS, HEAD_DIM), 1.0),
        'page_table': page_table,
        'state_win': nrm(ks[4], (DEPTH, DEC_BATCH, wb, 2, NSA_KV_HEADS, HEAD_DIM), 1.0),
        'state_wkv': nrm(ks[6], (DEPTH, DEC_BATCH, RWKV_HEADS, HEAD_DIM, HEAD_DIM), 0.5),
        'state_shift': nrm(ks[7], (DEPTH, DEC_BATCH, RWKV_COLS), 1.0),
        'norm_mix_g': 1.0 + nrm(ks[8], (DEPTH, D_MODEL), 0.01),
        'w_in': nrm(ks[9], (DEPTH, D_MODEL, IN_COLS), D_MODEL ** -0.5),
        'cmp_pos': nrm(ks[10], (DEPTH, CMP_BLOCK, 2, HEAD_DIM), 0.1),
        'w_cmp': nrm(ks[11], (DEPTH, 2, CMP_BLOCK, HEAD_DIM, HEAD_DIM), (CMP_BLOCK * HEAD_DIM) ** -0.5),
        'shift_mu': jax.random.uniform(ks[12], (DEPTH, RWKV_COLS), jnp.float32),
        'w0': -2.0 + nrm(ks[13], (DEPTH, RWKV_WIDTH), 0.5),
        'w2': nrm(ks[14], (DEPTH, RWKV_DECAY_LORA, RWKV_WIDTH), 0.1),
        'a0': nrm(ks[15], (DEPTH, RWKV_WIDTH), 0.1),
        'a2': nrm(ks[16], (DEPTH, RWKV_A_LORA, RWKV_WIDTH), 0.1),
        'g2': nrm(ks[17], (DEPTH, RWKV_GATE_LORA, RWKV_WIDTH), RWKV_GATE_LORA ** -0.5),
        'k_k': 0.85 + nrm(ks[18], (DEPTH, RWKV_WIDTH), 0.02),
        'k_a': 1.0 + nrm(ks[19], (DEPTH, RWKV_WIDTH), 0.02),
        'r_k': nrm(ks[20], (DEPTH, RWKV_HEADS, HEAD_DIM), 0.1),
        'gn_g': 1.0 + nrm(ks[21], (DEPTH, RWKV_WIDTH), 0.01),
        'gn_b': nrm(ks[22], (DEPTH, RWKV_WIDTH), 0.01),
        'w_out': nrm(ks[23], (DEPTH, MIX_WIDTH, D_MODEL), MIX_WIDTH ** -0.5),
        'norm_ffn_g': 1.0 + nrm(ks[24], (DEPTH, D_MODEL), 0.01),
        'w_router': nrm(ks[25], (DEPTH, D_MODEL, N_EXPERTS), D_MODEL ** -0.5),
        'b_router': nrm(ks[26], (DEPTH, N_EXPERTS), 0.01),
        'w_gate_up': nrm(ks[27], (DEPTH, N_EXPERTS, D_MODEL, 2 * D_FF), D_MODEL ** -0.5),
        'b_gate_up': nrm(ks[28], (DEPTH, N_EXPERTS, 2 * D_FF), 0.01),
        'w_down': nrm(ks[29], (DEPTH, N_EXPERTS, D_FF, D_MODEL), D_FF ** -0.5),
        'b_down': nrm(ks[30], (DEPTH, N_EXPERTS, D_MODEL), 0.01),
        'norm_final_g': 1.0 + nrm(ks[31], (D_MODEL,), 0.01),
    }


def reference(x_prompt, x_sample, cache_cmp, cache_slc, page_table, state_win, state_wkv, state_shift,
              norm_mix_g, w_in, cmp_pos, w_cmp, shift_mu, w0, w2, a0, a2, g2, k_k, k_a, r_k, gn_g, gn_b,
              w_out, norm_ffn_g, w_router, b_router, w_gate_up, b_gate_up, w_down, b_down, norm_final_g):
    xp, xs = x_prompt, x_sample
    cmp_p, slc_p, win_p, wkv_p, shift_p = [], [], [], [], []
    cmp_s, slc_s, win_s, wkv_s, shift_s = [], [], [], [], []
    for l in range(DEPTH):
        rw = (shift_mu[l], w0[l], w2[l], a0[l], a2[l], g2[l], k_k[l], k_a[l], r_k[l], gn_g[l], gn_b[l])
        ffn = (w_router[l], b_router[l], w_gate_up[l], b_gate_up[l], w_down[l], b_down[l])
        zp = rms_norm(xp, norm_mix_g[l]) @ w_in[l]
        oa, (kc, ksl, kw) = nsa_prompt(zp[..., :NSA_COLS], cmp_pos[l], w_cmp[l])
        ob, wkv, sh = rwkv7_time_mix(zp[..., NSA_COLS:], jnp.zeros((xp.shape[0], RWKV_COLS), zp.dtype),
                                     jnp.zeros((xp.shape[0], RWKV_HEADS, HEAD_DIM, HEAD_DIM), jnp.float32), *rw)
        xp = xp + jnp.concatenate([oa, ob], axis=-1) @ w_out[l]
        xp = xp + moe_ffn(rms_norm(xp, norm_ffn_g[l]), *ffn)
        cmp_p.append(kc); slc_p.append(ksl); win_p.append(kw); wkv_p.append(wkv); shift_p.append(sh)
        zs = rms_norm(xs, norm_mix_g[l]) @ w_in[l]
        oa, (kc, ksl, kw) = nsa_sample(zs[..., :NSA_COLS], cache_cmp[l], cache_slc[l], page_table, state_win[l],
                                       cmp_pos[l], w_cmp[l])
        ob, wkv, sh = rwkv7_time_mix(zs[..., NSA_COLS:], state_shift[l], state_wkv[l], *rw)
        xs = xs + jnp.concatenate([oa, ob], axis=-1) @ w_out[l]
        xs = xs + moe_ffn(rms_norm(xs, norm_ffn_g[l]), *ffn)
        cmp_s.append(kc); slc_s.append(ksl); win_s.append(kw); wkv_s.append(wkv); shift_s.append(sh)
    y_prompt = rms_norm(xp, norm_final_g)
    y_sample = rms_norm(xs, norm_final_g)
    return (y_prompt, y_sample,
            jnp.stack(cmp_p), jnp.stack(slc_p), jnp.stack(win_p), jnp.stack(wkv_p), jnp.stack(shift_p),
            jnp.stack(cmp_s), jnp.stack(slc_s), jnp.stack(win_s), jnp.stack(wkv_s), jnp.stack(shift_s))
```

```python
import functools
import math

import jax
import jax.numpy as jnp
from jax import lax
from jax.experimental import pallas as pl
from jax.experimental.pallas import tpu as pltpu

D_MODEL = 1024
DEPTH = 1
PAGE_SIZE = 128

HEAD_DIM = 64
NSA_HEADS = 8
NSA_KV_HEADS = 2
NSA_GROUP = NSA_HEADS // NSA_KV_HEADS
CMP_BLOCK = 32
CMP_STRIDE = 16
SEL_BLOCK = 64
SEL_TOPN = 16
WINDOW = 512
Q_BLOCK = 128
FORCE_SCORE = 1e6
RWKV_HEADS = 8
RWKV_DECAY_LORA = 64
RWKV_A_LORA = 64
RWKV_GATE_LORA = 128
RWKV_GN_EPS = 64e-5
N_EXPERTS = 32
TOP_K = 4
D_FF = 1024
SWIGLU_LIMIT = 7.0
SWIGLU_ALPHA = 1.702
EXPERT_BLOCK = 128
RMS_EPS = 1e-5

NSA_WIDTH = NSA_HEADS * HEAD_DIM
KV_WIDTH = NSA_KV_HEADS * HEAD_DIM
RWKV_WIDTH = RWKV_HEADS * HEAD_DIM
MIX_WIDTH = NSA_WIDTH + RWKV_WIDTH
NSA_COLS = NSA_WIDTH + 6 * KV_WIDTH + 3 * NSA_HEADS
RWKV_COLS = 3 * RWKV_WIDTH + RWKV_DECAY_LORA + RWKV_A_LORA + RWKV_GATE_LORA
IN_COLS = NSA_COLS + RWKV_COLS
RWKV_SPLITS = (RWKV_WIDTH, 2 * RWKV_WIDTH, 3 * RWKV_WIDTH, 3 * RWKV_WIDTH + RWKV_DECAY_LORA,
               3 * RWKV_WIDTH + RWKV_DECAY_LORA + RWKV_A_LORA)

LANE = 128
VMEM_LIMIT = 48 * 1024 * 1024


def _norm_proj_body(x_ref, g_ref, w_ref, o_ref):
    x = x_ref[...]
    y = x * lax.rsqrt(jnp.mean(x * x, -1, keepdims=True) + RMS_EPS) * g_ref[...]
    o_ref[...] = jnp.dot(y.astype(jnp.bfloat16), w_ref[...], preferred_element_type=jnp.float32)


def norm_proj(x, g, w, block_rows=512):
    n, d = x.shape
    c = w.shape[1]
    cp = -(-c // LANE) * LANE
    wb = jnp.pad(w, ((0, 0), (0, cp - c))).astype(jnp.bfloat16)
    out = pl.pallas_call(
        _norm_proj_body,
        grid=(n // block_rows,),
        in_specs=[pl.BlockSpec((block_rows, d), lambda i: (i, 0)),
                  pl.BlockSpec((1, d), lambda i: (0, 0)),
                  pl.BlockSpec((d, cp), lambda i: (0, 0))],
        out_specs=pl.BlockSpec((block_rows, cp), lambda i: (i, 0)),
        out_shape=jax.ShapeDtypeStruct((n, cp), jnp.float32),
        compiler_params=pltpu.CompilerParams(dimension_semantics=("arbitrary",), vmem_limit_bytes=VMEM_LIMIT),
        name="norm_proj",
    )(x, g.reshape(1, d), wb)
    return out[:, :c]


def rms_norm(x, g):
    xf = x.astype(jnp.float32)
    y = xf * lax.rsqrt(jnp.mean(xf * xf, -1, keepdims=True) + RMS_EPS)
    return (y * g.astype(jnp.float32)).astype(x.dtype)


def masked_softmax(s, mask):
    s = jnp.where(mask, s.astype(jnp.float32), -jnp.inf)
    m = jnp.max(s, -1, keepdims=True)
    m = jnp.where(jnp.isfinite(m), m, 0.0)
    e = jnp.where(mask, jnp.exp(s - m), 0.0)
    return e / jnp.maximum(jnp.sum(e, -1, keepdims=True), 1e-30)


def split_nsa(z):
    b, l = z.shape[:2]
    q = z[..., :NSA_WIDTH].reshape(b, l, NSA_KV_HEADS, NSA_GROUP, HEAD_DIM)
    kv = z[..., NSA_WIDTH:NSA_WIDTH + 6 * KV_WIDTH].reshape(b, l, 3, 2, NSA_KV_HEADS, HEAD_DIM)
    gates = jax.nn.sigmoid(z[..., NSA_WIDTH + 6 * KV_WIDTH:]).reshape(b, l, NSA_KV_HEADS, NSA_GROUP, 3)
    return q, kv[:, :, 0], kv[:, :, 1], kv[:, :, 2], gates


def compress_blocks(kv, cmp_pos, w_cmp):
    b, l = kv.shape[:2]
    chunks = kv.reshape(b, l // CMP_STRIDE, CMP_STRIDE, 2, NSA_KV_HEADS, HEAD_DIM)
    lo = jnp.einsum('bnlchd,cldo->bncho', chunks + cmp_pos[None, None, :CMP_STRIDE, :, None, :], w_cmp[:, :CMP_STRIDE])
    hi = jnp.einsum('bnlchd,cldo->bncho', chunks + cmp_pos[None, None, CMP_STRIDE:, :, None, :], w_cmp[:, CMP_STRIDE:])
    return lo[:, :-1] + hi[:, 1:]


def cmp_branch(qg, t, summ):
    n = summ.shape[1]
    end = jnp.arange(n) * CMP_STRIDE + (CMP_BLOCK - 1)
    mask = end[None, :] <= t[:, None]
    s = jnp.einsum('bqhgd,bnhd->bhgqn', qg, summ[:, :, 0]) * (HEAD_DIM ** -0.5)
    p = masked_softmax(s, mask)
    o = jnp.einsum('bhgqn,bnhd->bqhgd', p.astype(qg.dtype), summ[:, :, 1])
    return o, jnp.sum(p, axis=2)


def select_blocks(p_cmp, t, n_sel):
    n_cmp = p_cmp.shape[-1]
    i = jnp.arange(n_cmp)[:, None]
    j = jnp.arange(n_sel)[None, :]
    overlap = jnp.clip(jnp.minimum(i * CMP_STRIDE + CMP_BLOCK, (j + 1) * SEL_BLOCK)
                       - jnp.maximum(i * CMP_STRIDE, j * SEL_BLOCK), 0, None)
    score = jnp.einsum('bhqn,nj->bhqj', p_cmp, overlap.astype(jnp.float32) / CMP_BLOCK)
    tb = (t // SEL_BLOCK)[:, None]
    forced = (j == 0) | (j == tb) | (j == tb - 1)
    score = jnp.where(forced, FORCE_SCORE, score)
    score = jnp.where(j <= tb, score, -1.0)
    top_s, idx = lax.top_k(score, min(SEL_TOPN, n_sel))
    return idx, top_s >= 0.0


def slc_branch(qg, t, kv_sel, idx, ok):
    b, h, q, n = idx.shape
    kpos = idx[..., None] * SEL_BLOCK + jnp.arange(SEL_BLOCK)
    mask = (ok[..., None] & (kpos <= t[None, None, :, None, None])).reshape(b, h, 1, q, n * SEL_BLOCK)
    s = jnp.einsum('bqhgd,bhqnld->bhgqnl', qg, kv_sel[..., 0, :]) * (HEAD_DIM ** -0.5)
    p = masked_softmax(s.reshape(b, h, NSA_GROUP, q, n * SEL_BLOCK), mask)
    v = kv_sel[..., 1, :].reshape(b, h, q, n * SEL_BLOCK, HEAD_DIM)
    return jnp.einsum('bhgqk,bhqkd->bqhgd', p.astype(qg.dtype), v)


def win_branch(qg, t, kv_w, kpos):
    d = t[:, None] - kpos[None, :]
    mask = (d >= 0) & (d < WINDOW) & (kpos[None, :] >= 0)
    s = jnp.einsum('bqhgd,bkhd->bhgqk', qg, kv_w[:, :, 0]) * (HEAD_DIM ** -0.5)
    p = masked_softmax(s, mask)
    return jnp.einsum('bhgqk,bkhd->bqhgd', p.astype(qg.dtype), kv_w[:, :, 1])


def gate_merge(o_c, o_s, o_w, g):
    o = o_c * g[..., 0:1] + o_s * g[..., 1:2] + o_w * g[..., 2:3]
    return o.reshape(o.shape[0], o.shape[1], NSA_WIDTH)


def nsa_prompt(z, cmp_pos, w_cmp):
    q, kv_cmp, kv_slc, kv_win, gates = split_nsa(z)
    b, s = z.shape[:2]
    summ = compress_blocks(kv_cmp, cmp_pos, w_cmp)
    n_sel = s // SEL_BLOCK
    sel_blocks = kv_slc.reshape(b, n_sel, SEL_BLOCK, 2, NSA_KV_HEADS, HEAD_DIM)
    win_pad = jnp.pad(kv_win, ((0, 0), (WINDOW, 0), (0, 0), (0, 0), (0, 0)))
    b_ix = jnp.arange(b)[:, None, None, None]
    h_ix = jnp.arange(NSA_KV_HEADS)[None, :, None, None]

    def query_block(qb):
        start = qb * Q_BLOCK
        t = start + jnp.arange(Q_BLOCK)
        qg = lax.dynamic_slice_in_dim(q, start, Q_BLOCK, axis=1)
        g = lax.dynamic_slice_in_dim(gates, start, Q_BLOCK, axis=1)
        o_c, p_c = cmp_branch(qg, t, summ)
        idx, ok = select_blocks(p_c, t, n_sel)
        o_s = slc_branch(qg, t, sel_blocks[b_ix, idx, :, :, h_ix, :], idx, ok)
        kv_w = lax.dynamic_slice_in_dim(win_pad, start, Q_BLOCK + WINDOW, axis=1)
        o_w = win_branch(qg, t, kv_w, start - WINDOW + jnp.arange(Q_BLOCK + WINDOW))
        return gate_merge(o_c, o_s, o_w, g)

    o = lax.map(query_block, jnp.arange(s // Q_BLOCK))
    o = jnp.moveaxis(o, 0, 1).reshape(b, s, NSA_WIDTH)
    return o, (kv_cmp, kv_slc, kv_win[:, s - min(WINDOW, s):])


def nsa_sample(z, cache_cmp, cache_slc, page_table, state_win, cmp_pos, w_cmp):
    q, kv_cmp, kv_slc, kv_win, gates = split_nsa(z)
    b, n = z.shape[:2]
    n_pages = page_table.shape[1]
    past = n_pages * PAGE_SIZE
    total = past + n
    t = past + jnp.arange(n)
    b_ix = jnp.arange(b)[:, None, None, None]
    h_ix = jnp.arange(NSA_KV_HEADS)[None, :, None, None]
    past_cmp = cache_cmp[page_table].reshape(b, past, 2, NSA_KV_HEADS, HEAD_DIM)
    pad = (-total) % CMP_STRIDE
    full = jnp.concatenate([past_cmp, kv_cmp, jnp.zeros((b, pad, 2, NSA_KV_HEADS, HEAD_DIM), kv_cmp.dtype)], axis=1)
    summ = compress_blocks(full, cmp_pos, w_cmp)
    o_c, p_c = cmp_branch(q, t, summ)
    n_sel = -(-total // SEL_BLOCK)
    idx, ok = select_blocks(p_c, t, n_sel)
    past_blocks = past // SEL_BLOCK
    bpp = PAGE_SIZE // SEL_BLOCK
    pool = cache_slc.reshape(cache_slc.shape[0], bpp, SEL_BLOCK, 2, NSA_KV_HEADS, HEAD_DIM)
    phys = page_table[b_ix, jnp.minimum(idx // bpp, n_pages - 1)]
    from_past = pool[phys, idx % bpp, :, :, h_ix, :]
    n_new_blk = -(-n // SEL_BLOCK)
    new_blocks = jnp.pad(kv_slc, ((0, 0), (0, n_new_blk * SEL_BLOCK - n), (0, 0), (0, 0), (0, 0)))
    new_blocks = new_blocks.reshape(b, n_new_blk, SEL_BLOCK, 2, NSA_KV_HEADS, HEAD_DIM)
    from_new = new_blocks[b_ix, jnp.clip(idx - past_blocks, 0, n_new_blk - 1), :, :, h_ix, :]
    kv_sel = jnp.where((idx >= past_blocks)[..., None, None, None], from_new, from_past)
    o_s = slc_branch(q, t, kv_sel, idx, ok)
    wb = state_win.shape[1]
    kv_w = jnp.concatenate([state_win, kv_win], axis=1)
    o_w = win_branch(q, t, kv_w, past - wb + jnp.arange(wb + n))
    o = gate_merge(o_c, o_s, o_w, gates)
    return o, (kv_cmp, kv_slc, kv_w[:, n:])


def rwkv7_time_mix(z, prev_row, wkv0, shift_mu, w0, w2, a0, a2, g2, k_k, k_a, r_k, gn_g, gn_b):
    b, l = z.shape[:2]
    zf = z.astype(jnp.float32)
    prev = jnp.concatenate([prev_row[:, None].astype(jnp.float32), zf[:, :-1]], axis=1)
    zs = zf + shift_mu * (prev - zf)
    r, k, v, wd, ad, gd = jnp.split(zs, RWKV_SPLITS, axis=-1)
    w = -jax.nn.softplus(-(w0 + jnp.tanh(wd) @ w2)) - 0.5
    decay = jnp.exp(-jnp.exp(w))
    a = jax.nn.sigmoid(a0 + ad @ a2)
    g = jax.nn.sigmoid(gd) @ g2
    heads = lambda u: u.reshape(b, l, RWKV_HEADS, HEAD_DIM)
    kk = heads(k * k_k)
    kk = kk / jnp.maximum(jnp.sqrt(jnp.sum(kk * kk, -1, keepdims=True)), 1e-12)
    k = k * (1.0 + (a - 1.0) * k_a)
    r, k, v, a, decay = heads(r), heads(k), heads(v), heads(a), heads(decay)

    def step(S, inp):
        r_t, k_t, v_t, kk_t, a_t, w_t = inp
        sa = jnp.einsum('bhij,bhj->bhi', S, -kk_t)
        S = S * w_t[:, :, None, :] + sa[..., None] * (kk_t * a_t)[:, :, None, :] + v_t[..., None] * k_t[:, :, None, :]
        return S, jnp.einsum('bhij,bhj->bhi', S, r_t)

    seq_first = lambda u: jnp.moveaxis(u, 1, 0)
    s_final, o = lax.scan(step, wkv0.astype(jnp.float32), tuple(seq_first(u) for u in (r, k, v, kk, a, decay)))
    o = seq_first(o)
    mu = jnp.mean(o, -1, keepdims=True)
    var = jnp.mean(jnp.square(o - mu), -1, keepdims=True)
    o = ((o - mu) * lax.rsqrt(var + RWKV_GN_EPS)).reshape(b, l, RWKV_WIDTH) * gn_g + gn_b
    bonus = (jnp.sum(r * k * r_k, -1, keepdims=True) * v).reshape(b, l, RWKV_WIDTH)
    out = (o + bonus) * g
    return out.astype(z.dtype), s_final, z[:, -1]


def moe_ffn(x, w_router, b_router, w_gate_up, b_gate_up, w_down, b_down):
    shape = x.shape
    xt = x.reshape(-1, shape[-1])
    n_tok = xt.shape[0]
    logits = (xt @ w_router).astype(jnp.float32) + b_router.astype(jnp.float32)
    top_v, top_e = lax.top_k(logits, TOP_K)
    gates = jax.nn.softmax(top_v, axis=-1)
    flat_e = top_e.reshape(-1)
    n_assign = n_tok * TOP_K
    order = jnp.argsort(flat_e)
    sorted_e = flat_e[order]
    counts = jnp.bincount(flat_e, length=N_EXPERTS)
    padded = (counts + EXPERT_BLOCK - 1) // EXPERT_BLOCK * EXPERT_BLOCK
    pad_end = jnp.cumsum(padded)
    pad_start = pad_end - padded
    grp_start = jnp.cumsum(counts) - counts
    dest_sorted = pad_start[sorted_e] + jnp.arange(n_assign) - grp_start[sorted_e]
    n_blocks = -(-n_assign // EXPERT_BLOCK) + N_EXPERTS
    n_rows = n_blocks * EXPERT_BLOCK
    row_token = jnp.full((n_rows,), n_tok, jnp.int32).at[dest_sorted].set((order // TOP_K).astype(jnp.int32))
    x_pad = jnp.concatenate([xt, jnp.zeros((1, shape[-1]), xt.dtype)], axis=0)
    xb = x_pad[row_token].reshape(n_blocks, EXPERT_BLOCK, shape[-1])
    blk_e = jnp.minimum(jnp.sum(pad_end[None, :] <= (jnp.arange(n_blocks) * EXPERT_BLOCK)[:, None], axis=1), N_EXPERTS - 1)

    def expert_block(args):
        xe, e = args
        hgu = xe @ w_gate_up[e] + b_gate_up[e]
        gate = jnp.minimum(hgu[:, :D_FF], SWIGLU_LIMIT)
        up = jnp.clip(hgu[:, D_FF:], -SWIGLU_LIMIT, SWIGLU_LIMIT)
        act = (up + 1.0) * gate * jax.nn.sigmoid(SWIGLU_ALPHA * gate)
        return act @ w_down[e] + b_down[e]

    yb = lax.map(expert_block, (xb, blk_e)).reshape(n_rows, shape[-1])
    dest = jnp.zeros((n_assign,), dest_sorted.dtype).at[order].set(dest_sorted)
    y = yb[dest].reshape(n_tok, TOP_K, shape[-1])
    return jnp.einsum('tkd,tk->td', y, gates.astype(y.dtype)).reshape(shape)


def kernel(x_prompt, x_sample, cache_cmp, cache_slc, page_table, state_win, state_wkv, state_shift,
           norm_mix_g, w_in, cmp_pos, w_cmp, shift_mu, w0, w2, a0, a2, g2, k_k, k_a, r_k, gn_g, gn_b,
           w_out, norm_ffn_g, w_router, b_router, w_gate_up, b_gate_up, w_down, b_down, norm_final_g):
    xp, xs = x_prompt, x_sample
    l = 0
    rw = (shift_mu[l], w0[l], w2[l], a0[l], a2[l], g2[l], k_k[l], k_a[l], r_k[l], gn_g[l], gn_b[l])
    ffn = (w_router[l], b_router[l], w_gate_up[l], b_gate_up[l], w_down[l], b_down[l])
    bp, sp = xp.shape[:2]
    bs, ss = xs.shape[:2]
    zp = norm_proj(xp.reshape(bp * sp, D_MODEL), norm_mix_g[l], w_in[l]).reshape(bp, sp, IN_COLS)
    oa, (kc_p, ksl_p, kw_p) = nsa_prompt(zp[..., :NSA_COLS], cmp_pos[l], w_cmp[l])
    ob, wkv_p, sh_p = rwkv7_time_mix(zp[..., NSA_COLS:], jnp.zeros((bp, RWKV_COLS), zp.dtype),
                                     jnp.zeros((bp, RWKV_HEADS, HEAD_DIM, HEAD_DIM), jnp.float32), *rw)
    xp = xp + jnp.concatenate([oa, ob], axis=-1) @ w_out[l]
    xp = xp + moe_ffn(rms_norm(xp, norm_ffn_g[l]), *ffn)

    zs = norm_proj(xs.reshape(bs * ss, D_MODEL), norm_mix_g[l], w_in[l]).reshape(bs, ss, IN_COLS)
    oa, (kc_s, ksl_s, kw_s) = nsa_sample(zs[..., :NSA_COLS], cache_cmp[l], cache_slc[l], page_table, state_win[l],
                                         cmp_pos[l], w_cmp[l])
    ob, wkv_s, sh_s = rwkv7_time_mix(zs[..., NSA_COLS:], state_shift[l], state_wkv[l], *rw)
    xs = xs + jnp.concatenate([oa, ob], axis=-1) @ w_out[l]
    xs = xs + moe_ffn(rms_norm(xs, norm_ffn_g[l]), *ffn)

    y_prompt = rms_norm(xp, norm_final_g)
    y_sample = rms_norm(xs, norm_final_g)
    st = lambda u: u[None]
    return (y_prompt, y_sample, st(kc_p), st(ksl_p), st(kw_p), st(wkv_p), st(sh_p),
            st(kc_s), st(ksl_s), st(kw_s), st(wkv_s), st(sh_s))
```

```python
import functools
import math

import jax
import jax.numpy as jnp
from jax import lax
from jax.experimental import pallas as pl
from jax.experimental.pallas import tpu as pltpu

D_MODEL = 1024
DEPTH = 1
PAGE_SIZE = 128

HEAD_DIM = 64
NSA_HEADS = 8
NSA_KV_HEADS = 2
NSA_GROUP = NSA_HEADS // NSA_KV_HEADS
CMP_BLOCK = 32
CMP_STRIDE = 16
SEL_BLOCK = 64
SEL_TOPN = 16
WINDOW = 512
Q_BLOCK = 128
FORCE_SCORE = 1e6
RWKV_HEADS = 8
RWKV_DECAY_LORA = 64
RWKV_A_LORA = 64
RWKV_GATE_LORA = 128
RWKV_GN_EPS = 64e-5
N_EXPERTS = 32
TOP_K = 4
D_FF = 1024
SWIGLU_LIMIT = 7.0
SWIGLU_ALPHA = 1.702
EXPERT_BLOCK = 128
RMS_EPS = 1e-5

NSA_WIDTH = NSA_HEADS * HEAD_DIM
KV_WIDTH = NSA_KV_HEADS * HEAD_DIM
RWKV_WIDTH = RWKV_HEADS * HEAD_DIM
MIX_WIDTH = NSA_WIDTH + RWKV_WIDTH
NSA_COLS = NSA_WIDTH + 6 * KV_WIDTH + 3 * NSA_HEADS
RWKV_COLS = 3 * RWKV_WIDTH + RWKV_DECAY_LORA + RWKV_A_LORA + RWKV_GATE_LORA
IN_COLS = NSA_COLS + RWKV_COLS
RWKV_SPLITS = (RWKV_WIDTH, 2 * RWKV_WIDTH, 3 * RWKV_WIDTH, 3 * RWKV_WIDTH + RWKV_DECAY_LORA,
               3 * RWKV_WIDTH + RWKV_DECAY_LORA + RWKV_A_LORA)

LANE = 128
VMEM_LIMIT = 48 * 1024 * 1024


def _norm_proj_body(x_ref, g_ref, w_ref, o_ref):
    x = x_ref[...]
    y = x * lax.rsqrt(jnp.mean(x * x, -1, keepdims=True) + RMS_EPS) * g_ref[...]
    o_ref[...] = jnp.dot(y.astype(jnp.bfloat16), w_ref[...], preferred_element_type=jnp.float32)


def norm_proj(x, g, w, block_rows=512):
    n, d = x.shape
    c = w.shape[1]
    cp = -(-c // LANE) * LANE
    wb = jnp.pad(w, ((0, 0), (0, cp - c))).astype(jnp.bfloat16)
    out = pl.pallas_call(
        _norm_proj_body,
        grid=(n // block_rows,),
        in_specs=[pl.BlockSpec((block_rows, d), lambda i: (i, 0)),
                  pl.BlockSpec((1, d), lambda i: (0, 0)),
                  pl.BlockSpec((d, cp), lambda i: (0, 0))],
        out_specs=pl.BlockSpec((block_rows, cp), lambda i: (i, 0)),
        out_shape=jax.ShapeDtypeStruct((n, cp), jnp.float32),
        compiler_params=pltpu.CompilerParams(dimension_semantics=("arbitrary",), vmem_limit_bytes=VMEM_LIMIT),
        name="norm_proj",
    )(x, g.reshape(1, d), wb)
    return out[:, :c]


QROWS = NSA_GROUP * Q_BLOCK
SLC_TK = 256
MASK_BIAS = -float(2 ** 30)
NEG_INIT = -3.0e38
V_AUG = 2 * HEAD_DIM

_NT = (((1,), (1,)), ((), ()))


def _cmp_select_body(q_ref, kc_ref, vc_ref, ov_ref, oc_ref, bias_ref):
    qb = pl.program_id(2)
    ncp = kc_ref.shape[2]
    nsel = ov_ref.shape[1]
    q = q_ref[0, 0, 0]
    s = lax.dot_general(q, kc_ref[0, 0], _NT, preferred_element_type=jnp.float32)
    row = lax.broadcasted_iota(jnp.int32, (QROWS, ncp), 0)
    t = qb * Q_BLOCK + (row & (Q_BLOCK - 1))
    n = lax.broadcasted_iota(jnp.int32, (QROWS, ncp), 1)
    mask = n * CMP_STRIDE + (CMP_BLOCK - 1) <= t
    s = jnp.where(mask, s, -jnp.inf)
    m = jnp.max(s, -1, keepdims=True)
    m = jnp.where(m == -jnp.inf, 0.0, m)
    e = jnp.exp(s - m)
    p = e / jnp.maximum(jnp.sum(e, -1, keepdims=True), 1e-30)
    oc_ref[0, 0, 0] = jnp.dot(p.astype(jnp.bfloat16), vc_ref[0, 0], preferred_element_type=jnp.float32)
    pc = p[0:Q_BLOCK]
    for g in range(1, NSA_GROUP):
        pc = pc + p[g * Q_BLOCK:(g + 1) * Q_BLOCK]
    score = jnp.dot(pc.astype(jnp.bfloat16), ov_ref[...], preferred_element_type=jnp.float32)
    j = lax.broadcasted_iota(jnp.int32, (Q_BLOCK, nsel), 1)
    tq = qb * Q_BLOCK + lax.broadcasted_iota(jnp.int32, (Q_BLOCK, nsel), 0)
    tb = tq >> (SEL_BLOCK.bit_length() - 1)
    forced = (j == 0) | (j == tb) | (j == tb - 1)
    score = jnp.where(forced, FORCE_SCORE, score)
    score = jnp.where(j <= tb, score, -1.0)
    jf = j.astype(jnp.float32)
    sel = jnp.zeros((Q_BLOCK, nsel), jnp.bool_)
    for _ in range(min(SEL_TOPN, nsel)):
        top = jnp.max(score, -1, keepdims=True)
        first = jnp.min(jnp.where(score == top, jf, float(nsel)), -1, keepdims=True)
        pick = jf == first
        sel = sel | pick
        score = jnp.where(pick, -2.0, score)
    bias_ref[0, 0] = jnp.where(sel, 0.0, MASK_BIAS).astype(jnp.bfloat16)


def _slc_body(q_ref, bias_ref, kaug_ref, vaug_ref, o_ref, qaug_sc, m_sc, acc_sc):
    qb = pl.program_id(2)
    start = qb * Q_BLOCK
    nsel = bias_ref.shape[3]
    bias = bias_ref[0, 0]
    for g in range(NSA_GROUP):
        qaug_sc[g * Q_BLOCK:(g + 1) * Q_BLOCK, 0:nsel] = bias
    qaug_sc[:, nsel:nsel + HEAD_DIM] = q_ref[0, 0, 0]
    m_sc[...] = jnp.full(m_sc.shape, NEG_INIT, jnp.float32)
    acc_sc[...] = jnp.zeros(acc_sc.shape, jnp.float32)

    def tile(kt, causal):
        k0 = pl.multiple_of(kt * SLC_TK, SLC_TK)
        s = lax.dot_general(qaug_sc[...], kaug_ref[0, 0, pl.ds(k0, SLC_TK), :], _NT,
                            preferred_element_type=jnp.float32)
        if causal:
            row = lax.broadcasted_iota(jnp.int32, (QROWS, SLC_TK), 0)
            col = lax.broadcasted_iota(jnp.int32, (QROWS, SLC_TK), 1)
            s = jnp.where(k0 + col <= start + (row & (Q_BLOCK - 1)), s, -jnp.inf)
        m_prev = m_sc[...]
        m_new = jnp.maximum(m_prev, jnp.max(s, -1, keepdims=True))
        p = jnp.exp(s - m_new)
        acc_sc[...] = jnp.exp(m_prev - m_new) * acc_sc[...] + jnp.dot(
            p.astype(jnp.bfloat16), vaug_ref[0, 0, pl.ds(k0, SLC_TK), :], preferred_element_type=jnp.float32)
        m_sc[...] = m_new

    n_full = start // SLC_TK

    def body(kt, carry):
        tile(kt, False)
        return carry

    lax.fori_loop(0, n_full, body, 0)
    tile(n_full, True)
    acc = acc_sc[...]
    o_ref[0, 0, 0] = acc[:, :HEAD_DIM] / jnp.maximum(acc[:, HEAD_DIM:HEAD_DIM + 1], 1e-30)


def _win_body(q_ref, kpad_ref, vpad_ref, o_ref):
    qb = pl.program_id(2)
    start = pl.multiple_of(qb * Q_BLOCK, Q_BLOCK)
    nk = Q_BLOCK + WINDOW
    k = kpad_ref[0, 0, pl.ds(start, nk), :]
    s = lax.dot_general(q_ref[0, 0, 0], k, _NT, preferred_element_type=jnp.float32)
    r = lax.broadcasted_iota(jnp.int32, (QROWS, nk), 0) & (Q_BLOCK - 1)
    c = lax.broadcasted_iota(jnp.int32, (QROWS, nk), 1)
    d = r + WINDOW - c
    mask = (d >= 0) & (d < WINDOW) & (c >= WINDOW - start)
    s = jnp.where(mask, s, -jnp.inf)
    m = jnp.max(s, -1, keepdims=True)
    m = jnp.where(m == -jnp.inf, 0.0, m)
    p = jnp.exp(s - m)
    pv = jnp.dot(p.astype(jnp.bfloat16), vpad_ref[0, 0, pl.ds(start, nk), :], preferred_element_type=jnp.float32)
    o_ref[0, 0, 0] = pv[:, :HEAD_DIM] / jnp.maximum(pv[:, HEAD_DIM:HEAD_DIM + 1], 1e-30)


def _to_tiles(u, scale=None):
    b, s = u.shape[:2]
    u = u.reshape(b, s // Q_BLOCK, Q_BLOCK, NSA_KV_HEADS, NSA_GROUP, HEAD_DIM)
    u = jnp.transpose(u, (0, 3, 1, 4, 2, 5)).reshape(b, NSA_KV_HEADS, s // Q_BLOCK, QROWS, HEAD_DIM)
    return u if scale is None else (u * scale).astype(jnp.bfloat16)


def _from_tiles(o):
    b, _, nq, _, x = o.shape
    o = o.reshape(b, NSA_KV_HEADS, nq, NSA_GROUP, Q_BLOCK, x)
    return jnp.transpose(o, (0, 2, 4, 1, 3, 5)).reshape(b, nq * Q_BLOCK, NSA_KV_HEADS, NSA_GROUP, x)


def _head_major(kv):
    return jnp.transpose(kv, (0, 2, 1, 3))


def _with_ones(v):
    ones = jnp.ones(v.shape[:-1] + (1,), v.dtype)
    zeros = jnp.zeros(v.shape[:-1] + (V_AUG - HEAD_DIM - 1,), v.dtype)
    return jnp.concatenate([v, ones, zeros], -1).astype(jnp.bfloat16)


def nsa_prompt(z, cmp_pos, w_cmp):
    q, kv_cmp, kv_slc, kv_win, gates = split_nsa(z)
    b, s = z.shape[:2]
    nq = s // Q_BLOCK
    nsel = s // SEL_BLOCK
    assert s % SLC_TK == 0 and nsel % LANE == 0
    summ = compress_blocks(kv_cmp, cmp_pos, w_cmp)
    n_cmp = summ.shape[1]
    ncp = -(-n_cmp // LANE) * LANE
    summ = jnp.pad(summ, ((0, 0), (0, ncp - n_cmp), (0, 0), (0, 0), (0, 0))).astype(jnp.bfloat16)
    kc, vc = _head_major(summ[:, :, 0]), _head_major(summ[:, :, 1])
    i = jnp.arange(ncp)[:, None]
    jj = jnp.arange(nsel)[None, :]
    overlap = jnp.clip(jnp.minimum(i * CMP_STRIDE + CMP_BLOCK, (jj + 1) * SEL_BLOCK)
                       - jnp.maximum(i * CMP_STRIDE, jj * SEL_BLOCK), 0, None)
    ov = (overlap.astype(jnp.float32) / CMP_BLOCK).astype(jnp.bfloat16)
    qt = _to_tiles(q, HEAD_DIM ** -0.5)

    grid = (b, NSA_KV_HEADS, nq)
    params = pltpu.CompilerParams(dimension_semantics=("arbitrary",) * 3, vmem_limit_bytes=VMEM_LIMIT)
    q_spec = pl.BlockSpec((1, 1, 1, QROWS, HEAD_DIM), lambda bi, h, qi: (bi, h, qi, 0, 0))
    o_spec = pl.BlockSpec((1, 1, 1, QROWS, HEAD_DIM), lambda bi, h, qi: (bi, h, qi, 0, 0))
    o_shape = jax.ShapeDtypeStruct((b, NSA_KV_HEADS, nq, QROWS, HEAD_DIM), jnp.float32)
    seq_spec = lambda rows, width: pl.BlockSpec((1, 1, rows, width), lambda bi, h, qi: (bi, h, 0, 0))
    bias_spec = pl.BlockSpec((1, 1, Q_BLOCK, nsel), lambda bi, h, qi: (bi, h, qi, 0))

    o_c, bias = pl.pallas_call(
        _cmp_select_body, grid=grid,
        in_specs=[q_spec, seq_spec(ncp, HEAD_DIM), seq_spec(ncp, HEAD_DIM),
                  pl.BlockSpec((ncp, nsel), lambda bi, h, qi: (0, 0))],
        out_specs=[o_spec, bias_spec],
        out_shape=[o_shape, jax.ShapeDtypeStruct((b, NSA_KV_HEADS, s, nsel), jnp.bfloat16)],
        compiler_params=params, name="nsa_cmp_select",
    )(qt, kc, vc, ov)

    k_s, v_s = _head_major(kv_slc[:, :, 0]), _head_major(kv_slc[:, :, 1])
    onehot = (jnp.arange(s)[:, None] // SEL_BLOCK == jnp.arange(nsel)[None, :]).astype(jnp.bfloat16)
    kaug = jnp.concatenate([jnp.broadcast_to(onehot, (b, NSA_KV_HEADS, s, nsel)), k_s.astype(jnp.bfloat16)], -1)
    o_s = pl.pallas_call(
        _slc_body, grid=grid,
        in_specs=[q_spec, bias_spec, seq_spec(s, nsel + HEAD_DIM), seq_spec(s, V_AUG)],
        out_specs=o_spec, out_shape=o_shape,
        scratch_shapes=[pltpu.VMEM((QROWS, nsel + HEAD_DIM), jnp.bfloat16),
                        pltpu.VMEM((QROWS, 1), jnp.float32),
                        pltpu.VMEM((QROWS, V_AUG), jnp.float32)],
        compiler_params=params, name="nsa_slc",
    )(qt, bias, kaug, _with_ones(v_s))

    front = ((0, 0), (0, 0), (WINDOW, 0), (0, 0))
    k_w = jnp.pad(_head_major(kv_win[:, :, 0]).astype(jnp.bfloat16), front)
    v_w = jnp.pad(_with_ones(_head_major(kv_win[:, :, 1])), front)
    o_w = pl.pallas_call(
        _win_body, grid=grid,
        in_specs=[q_spec, seq_spec(s + WINDOW, HEAD_DIM), seq_spec(s + WINDOW, V_AUG)],
        out_specs=o_spec, out_shape=o_shape,
        compiler_params=params, name="nsa_win",
    )(qt, k_w, v_w)

    o = (_from_tiles(o_c) * gates[..., 0:1] + _from_tiles(o_s) * gates[..., 1:2]
         + _from_tiles(o_w) * gates[..., 2:3]).reshape(b, s, NSA_WIDTH)
    return o, (kv_cmp, kv_slc, kv_win[:, s - min(WINDOW, s):])


def rms_norm(x, g):
    xf = x.astype(jnp.float32)
    y = xf * lax.rsqrt(jnp.mean(xf * xf, -1, keepdims=True) + RMS_EPS)
    return (y * g.astype(jnp.float32)).astype(x.dtype)


def masked_softmax(s, mask):
    s = jnp.where(mask, s.astype(jnp.float32), -jnp.inf)
    m = jnp.max(s, -1, keepdims=True)
    m = jnp.where(jnp.isfinite(m), m, 0.0)
    e = jnp.where(mask, jnp.exp(s - m), 0.0)
    return e / jnp.maximum(jnp.sum(e, -1, keepdims=True), 1e-30)


def split_nsa(z):
    b, l = z.shape[:2]
    q = z[..., :NSA_WIDTH].reshape(b, l, NSA_KV_HEADS, NSA_GROUP, HEAD_DIM)
    kv = z[..., NSA_WIDTH:NSA_WIDTH + 6 * KV_WIDTH].reshape(b, l, 3, 2, NSA_KV_HEADS, HEAD_DIM)
    gates = jax.nn.sigmoid(z[..., NSA_WIDTH + 6 * KV_WIDTH:]).reshape(b, l, NSA_KV_HEADS, NSA_GROUP, 3)
    return q, kv[:, :, 0], kv[:, :, 1], kv[:, :, 2], gates


def compress_blocks(kv, cmp_pos, w_cmp):
    b, l = kv.shape[:2]
    chunks = kv.reshape(b, l // CMP_STRIDE, CMP_STRIDE, 2, NSA_KV_HEADS, HEAD_DIM)
    lo = jnp.einsum('bnlchd,cldo->bncho', chunks + cmp_pos[None, None, :CMP_STRIDE, :, None, :], w_cmp[:, :CMP_STRIDE])
    hi = jnp.einsum('bnlchd,cldo->bncho', chunks + cmp_pos[None, None, CMP_STRIDE:, :, None, :], w_cmp[:, CMP_STRIDE:])
    return lo[:, :-1] + hi[:, 1:]


def cmp_branch(qg, t, summ):
    n = summ.shape[1]
    end = jnp.arange(n) * CMP_STRIDE + (CMP_BLOCK - 1)
    mask = end[None, :] <= t[:, None]
    s = jnp.einsum('bqhgd,bnhd->bhgqn', qg, summ[:, :, 0]) * (HEAD_DIM ** -0.5)
    p = masked_softmax(s, mask)
    o = jnp.einsum('bhgqn,bnhd->bqhgd', p.astype(qg.dtype), summ[:, :, 1])
    return o, jnp.sum(p, axis=2)


def select_blocks(p_cmp, t, n_sel):
    n_cmp = p_cmp.shape[-1]
    i = jnp.arange(n_cmp)[:, None]
    j = jnp.arange(n_sel)[None, :]
    overlap = jnp.clip(jnp.minimum(i * CMP_STRIDE + CMP_BLOCK, (j + 1) * SEL_BLOCK)
                       - jnp.maximum(i * CMP_STRIDE, j * SEL_BLOCK), 0, None)
    score = jnp.einsum('bhqn,nj->bhqj', p_cmp, overlap.astype(jnp.float32) / CMP_BLOCK)
    tb = (t // SEL_BLOCK)[:, None]
    forced = (j == 0) | (j == tb) | (j == tb - 1)
    score = jnp.where(forced, FORCE_SCORE, score)
    score = jnp.where(j <= tb, score, -1.0)
    top_s, idx = lax.top_k(score, min(SEL_TOPN, n_sel))
    return idx, top_s >= 0.0


def slc_branch(qg, t, kv_sel, idx, ok):
    b, h, q, n = idx.shape
    kpos = idx[..., None] * SEL_BLOCK + jnp.arange(SEL_BLOCK)
    mask = (ok[..., None] & (kpos <= t[None, None, :, None, None])).reshape(b, h, 1, q, n * SEL_BLOCK)
    s = jnp.einsum('bqhgd,bhqnld->bhgqnl', qg, kv_sel[..., 0, :]) * (HEAD_DIM ** -0.5)
    p = masked_softmax(s.reshape(b, h, NSA_GROUP, q, n * SEL_BLOCK), mask)
    v = kv_sel[..., 1, :].reshape(b, h, q, n * SEL_BLOCK, HEAD_DIM)
    return jnp.einsum('bhgqk,bhqkd->bqhgd', p.astype(qg.dtype), v)


def win_branch(qg, t, kv_w, kpos):
    d = t[:, None] - kpos[None, :]
    mask = (d >= 0) & (d < WINDOW) & (kpos[None, :] >= 0)
    s = jnp.einsum('bqhgd,bkhd->bhgqk', qg, kv_w[:, :, 0]) * (HEAD_DIM ** -0.5)
    p = masked_softmax(s, mask)
    return jnp.einsum('bhgqk,bkhd->bqhgd', p.astype(qg.dtype), kv_w[:, :, 1])


def gate_merge(o_c, o_s, o_w, g):
    o = o_c * g[..., 0:1] + o_s * g[..., 1:2] + o_w * g[..., 2:3]
    return o.reshape(o.shape[0], o.shape[1], NSA_WIDTH)


def nsa_sample(z, cache_cmp, cache_slc, page_table, state_win, cmp_pos, w_cmp):
    q, kv_cmp, kv_slc, kv_win, gates = split_nsa(z)
    b, n = z.shape[:2]
    n_pages = page_table.shape[1]
    past = n_pages * PAGE_SIZE
    total = past + n
    t = past + jnp.arange(n)
    b_ix = jnp.arange(b)[:, None, None, None]
    h_ix = jnp.arange(NSA_KV_HEADS)[None, :, None, None]
    past_cmp = cache_cmp[page_table].reshape(b, past, 2, NSA_KV_HEADS, HEAD_DIM)
    pad = (-total) % CMP_STRIDE
    full = jnp.concatenate([past_cmp, kv_cmp, jnp.zeros((b, pad, 2, NSA_KV_HEADS, HEAD_DIM), kv_cmp.dtype)], axis=1)
    summ = compress_blocks(full, cmp_pos, w_cmp)
    o_c, p_c = cmp_branch(q, t, summ)
    n_sel = -(-total // SEL_BLOCK)
    idx, ok = select_blocks(p_c, t, n_sel)
    past_blocks = past // SEL_BLOCK
    bpp = PAGE_SIZE // SEL_BLOCK
    phys = page_table[b_ix, jnp.minimum(idx // bpp, n_pages - 1)]
    blk_row = (phys * bpp + idx % bpp)[..., None, None] * SEL_BLOCK + jnp.arange(SEL_BLOCK)[:, None]
    flat_row = (blk_row * 2 + jnp.arange(2)) * NSA_KV_HEADS + h_ix[..., None, None]
    from_past = cache_slc.reshape(-1, HEAD_DIM)[flat_row]
    n_new_blk = -(-n // SEL_BLOCK)
    new_blocks = jnp.pad(kv_slc, ((0, 0), (0, n_new_blk * SEL_BLOCK - n), (0, 0), (0, 0), (0, 0)))
    new_blocks = new_blocks.reshape(b, n_new_blk, SEL_BLOCK, 2, NSA_KV_HEADS, HEAD_DIM)
    from_new = new_blocks[b_ix, jnp.clip(idx - past_blocks, 0, n_new_blk - 1), :, :, h_ix, :]
    kv_sel = jnp.where((idx >= past_blocks)[..., None, None, None], from_new, from_past)
    o_s = slc_branch(q, t, kv_sel, idx, ok)
    wb = state_win.shape[1]
    kv_w = jnp.concatenate([state_win, kv_win], axis=1)
    o_w = win_branch(q, t, kv_w, past - wb + jnp.arange(wb + n))
    o = gate_merge(o_c, o_s, o_w, gates)
    return o, (kv_cmp, kv_slc, kv_w[:, n:])


def rwkv7_time_mix(z, prev_row, wkv0, shift_mu, w0, w2, a0, a2, g2, k_k, k_a, r_k, gn_g, gn_b):
    b, l = z.shape[:2]
    zf = z.astype(jnp.float32)
    prev = jnp.concatenate([prev_row[:, None].astype(jnp.float32), zf[:, :-1]], axis=1)
    zs = zf + shift_mu * (prev - zf)
    r, k, v, wd, ad, gd = jnp.split(zs, RWKV_SPLITS, axis=-1)
    w = -jax.nn.softplus(-(w0 + jnp.tanh(wd) @ w2)) - 0.5
    decay = jnp.exp(-jnp.exp(w))
    a = jax.nn.sigmoid(a0 + ad @ a2)
    g = jax.nn.sigmoid(gd) @ g2
    heads = lambda u: u.reshape(b, l, RWKV_HEADS, HEAD_DIM)
    kk = heads(k * k_k)
    kk = kk / jnp.maximum(jnp.sqrt(jnp.sum(kk * kk, -1, keepdims=True)), 1e-12)
    k = k * (1.0 + (a - 1.0) * k_a)
    r, k, v, a, decay = heads(r), heads(k), heads(v), heads(a), heads(decay)

    def step(S, inp):
        r_t, k_t, v_t, kk_t, a_t, w_t = inp
        sa = jnp.einsum('bhij,bhj->bhi', S, -kk_t)
        S = S * w_t[:, :, None, :] + sa[..., None] * (kk_t * a_t)[:, :, None, :] + v_t[..., None] * k_t[:, :, None, :]
        return S, jnp.einsum('bhij,bhj->bhi', S, r_t)

    seq_first = lambda u: jnp.moveaxis(u, 1, 0)
    s_final, o = lax.scan(step, wkv0.astype(jnp.float32), tuple(seq_first(u) for u in (r, k, v, kk, a, decay)))
    o = seq_first(o)
    mu = jnp.mean(o, -1, keepdims=True)
    var = jnp.mean(jnp.square(o - mu), -1, keepdims=True)
    o = ((o - mu) * lax.rsqrt(var + RWKV_GN_EPS)).reshape(b, l, RWKV_WIDTH) * gn_g + gn_b
    bonus = (jnp.sum(r * k * r_k, -1, keepdims=True) * v).reshape(b, l, RWKV_WIDTH)
    out = (o + bonus) * g
    return out.astype(z.dtype), s_final, z[:, -1]


def moe_ffn(x, w_router, b_router, w_gate_up, b_gate_up, w_down, b_down):
    shape = x.shape
    xt = x.reshape(-1, shape[-1])
    n_tok = xt.shape[0]
    logits = (xt @ w_router).astype(jnp.float32) + b_router.astype(jnp.float32)
    top_v, top_e = lax.top_k(logits, TOP_K)
    gates = jax.nn.softmax(top_v, axis=-1)
    flat_e = top_e.reshape(-1)
    n_assign = n_tok * TOP_K
    order = jnp.argsort(flat_e)
    sorted_e = flat_e[order]
    counts = jnp.bincount(flat_e, length=N_EXPERTS)
    padded = (counts + EXPERT_BLOCK - 1) // EXPERT_BLOCK * EXPERT_BLOCK
    pad_end = jnp.cumsum(padded)
    pad_start = pad_end - padded
    grp_start = jnp.cumsum(counts) - counts
    dest_sorted = pad_start[sorted_e] + jnp.arange(n_assign) - grp_start[sorted_e]
    n_blocks = -(-n_assign // EXPERT_BLOCK) + N_EXPERTS
    n_rows = n_blocks * EXPERT_BLOCK
    row_token = jnp.full((n_rows,), n_tok, jnp.int32).at[dest_sorted].set((order // TOP_K).astype(jnp.int32))
    x_pad = jnp.concatenate([xt, jnp.zeros((1, shape[-1]), xt.dtype)], axis=0)
    xb = x_pad[row_token].reshape(n_blocks, EXPERT_BLOCK, shape[-1])
    blk_e = jnp.minimum(jnp.sum(pad_end[None, :] <= (jnp.arange(n_blocks) * EXPERT_BLOCK)[:, None], axis=1), N_EXPERTS - 1)

    def expert_block(args):
        xe, e = args
        hgu = xe @ w_gate_up[e] + b_gate_up[e]
        gate = jnp.minimum(hgu[:, :D_FF], SWIGLU_LIMIT)
        up = jnp.clip(hgu[:, D_FF:], -SWIGLU_LIMIT, SWIGLU_LIMIT)
        act = (up + 1.0) * gate * jax.nn.sigmoid(SWIGLU_ALPHA * gate)
        return act @ w_down[e] + b_down[e]

    yb = lax.map(expert_block, (xb, blk_e)).reshape(n_rows, shape[-1])
    dest = jnp.zeros((n_assign,), dest_sorted.dtype).at[order].set(dest_sorted)
    y = yb[dest].reshape(n_tok, TOP_K, shape[-1])
    return jnp.einsum('tkd,tk->td', y, gates.astype(y.dtype)).reshape(shape)


def kernel(x_prompt, x_sample, cache_cmp, cache_slc, page_table, state_win, state_wkv, state_shift,
           norm_mix_g, w_in, cmp_pos, w_cmp, shift_mu, w0, w2, a0, a2, g2, k_k, k_a, r_k, gn_g, gn_b,
           w_out, norm_ffn_g, w_router, b_router, w_gate_up, b_gate_up, w_down, b_down, norm_final_g):
    xp, xs = x_prompt, x_sample
    l = 0
    rw = (shift_mu[l], w0[l], w2[l], a0[l], a2[l], g2[l], k_k[l], k_a[l], r_k[l], gn_g[l], gn_b[l])
    ffn = (w_router[l], b_router[l], w_gate_up[l], b_gate_up[l], w_down[l], b_down[l])
    bp, sp = xp.shape[:2]
    bs, ss = xs.shape[:2]
    zp = norm_proj(xp.reshape(bp * sp, D_MODEL), norm_mix_g[l], w_in[l]).reshape(bp, sp, IN_COLS)
    oa, (kc_p, ksl_p, kw_p) = nsa_prompt(zp[..., :NSA_COLS], cmp_pos[l], w_cmp[l])
    ob, wkv_p, sh_p = rwkv7_time_mix(zp[..., NSA_COLS:], jnp.zeros((bp, RWKV_COLS), zp.dtype),
                                     jnp.zeros((bp, RWKV_HEADS, HEAD_DIM, HEAD_DIM), jnp.float32), *rw)
    xp = xp + jnp.concatenate([oa, ob], axis=-1) @ w_out[l]
    xp = xp + moe_ffn(rms_norm(xp, norm_ffn_g[l]), *ffn)

    zs = norm_proj(xs.reshape(bs * ss, D_MODEL), norm_mix_g[l], w_in[l]).reshape(bs, ss, IN_COLS)
    oa, (kc_s, ksl_s, kw_s) = nsa_sample(zs[..., :NSA_COLS], cache_cmp[l], cache_slc[l], page_table, state_win[l],
                                         cmp_pos[l], w_cmp[l])
    ob, wkv_s, sh_s = rwkv7_time_mix(zs[..., NSA_COLS:], state_shift[l], state_wkv[l], *rw)
    xs = xs + jnp.concatenate([oa, ob], axis=-1) @ w_out[l]
    xs = xs + moe_ffn(rms_norm(xs, norm_ffn_g[l]), *ffn)

    y_prompt = rms_norm(xp, norm_final_g)
    y_sample = rms_norm(xs, norm_final_g)
    st = lambda u: u[None]
    return (y_prompt, y_sample, st(kc_p), st(ksl_p), st(kw_p), st(wkv_p), st(sh_p),
            st(kc_s), st(ksl_s), st(kw_s), st(wkv_s), st(sh_s))
```

```python
import functools
import math

import jax
import jax.numpy as jnp
from jax import lax
from jax.experimental import pallas as pl
from jax.experimental.pallas import tpu as pltpu

D_MODEL = 1024
DEPTH = 1
PAGE_SIZE = 128

HEAD_DIM = 64
NSA_HEADS = 8
NSA_KV_HEADS = 2
NSA_GROUP = NSA_HEADS // NSA_KV_HEADS
CMP_BLOCK = 32
CMP_STRIDE = 16
SEL_BLOCK = 64
SEL_TOPN = 16
WINDOW = 512
Q_BLOCK = 128
FORCE_SCORE = 1e6
RWKV_HEADS = 8
RWKV_DECAY_LORA = 64
RWKV_A_LORA = 64
RWKV_GATE_LORA = 128
RWKV_GN_EPS = 64e-5
N_EXPERTS = 32
TOP_K = 4
D_FF = 1024
SWIGLU_LIMIT = 7.0
SWIGLU_ALPHA = 1.702
EXPERT_BLOCK = 128
RMS_EPS = 1e-5

NSA_WIDTH = NSA_HEADS * HEAD_DIM
KV_WIDTH = NSA_KV_HEADS * HEAD_DIM
RWKV_WIDTH = RWKV_HEADS * HEAD_DIM
MIX_WIDTH = NSA_WIDTH + RWKV_WIDTH
NSA_COLS = NSA_WIDTH + 6 * KV_WIDTH + 3 * NSA_HEADS
RWKV_COLS = 3 * RWKV_WIDTH + RWKV_DECAY_LORA + RWKV_A_LORA + RWKV_GATE_LORA
IN_COLS = NSA_COLS + RWKV_COLS
RWKV_SPLITS = (RWKV_WIDTH, 2 * RWKV_WIDTH, 3 * RWKV_WIDTH, 3 * RWKV_WIDTH + RWKV_DECAY_LORA,
               3 * RWKV_WIDTH + RWKV_DECAY_LORA + RWKV_A_LORA)

LANE = 128
VMEM_LIMIT = 48 * 1024 * 1024


def _norm_proj_body(x_ref, g_ref, w_ref, o_ref):
    x = x_ref[...]
    y = x * lax.rsqrt(jnp.mean(x * x, -1, keepdims=True) + RMS_EPS) * g_ref[...]
    o_ref[...] = jnp.dot(y.astype(jnp.bfloat16), w_ref[...], preferred_element_type=jnp.float32)


def norm_proj(x, g, w, block_rows=512):
    n, d = x.shape
    c = w.shape[1]
    cp = -(-c // LANE) * LANE
    wb = jnp.pad(w, ((0, 0), (0, cp - c))).astype(jnp.bfloat16)
    out = pl.pallas_call(
        _norm_proj_body,
        grid=(n // block_rows,),
        in_specs=[pl.BlockSpec((block_rows, d), lambda i: (i, 0)),
                  pl.BlockSpec((1, d), lambda i: (0, 0)),
                  pl.BlockSpec((d, cp), lambda i: (0, 0))],
        out_specs=pl.BlockSpec((block_rows, cp), lambda i: (i, 0)),
        out_shape=jax.ShapeDtypeStruct((n, cp), jnp.float32),
        compiler_params=pltpu.CompilerParams(dimension_semantics=("arbitrary",), vmem_limit_bytes=VMEM_LIMIT),
        name="norm_proj",
    )(x, g.reshape(1, d), wb)
    return out[:, :c]


QROWS = NSA_GROUP * Q_BLOCK
SLC_TK = 512
MASK_BIAS = -float(2 ** 30)
NEG_INIT = -3.0e38
V_AUG = 2 * HEAD_DIM

_NT = (((1,), (1,)), ((), ()))


def _cmp_select_body(q_ref, kc_ref, vc_ref, ov_ref, oc_ref, bias_ref):
    qb = pl.program_id(2)
    ncp = kc_ref.shape[2]
    nsel = ov_ref.shape[1]
    q = q_ref[0, 0, 0]
    s = lax.dot_general(q, kc_ref[0, 0], _NT, preferred_element_type=jnp.float32)
    row = lax.broadcasted_iota(jnp.int32, (QROWS, ncp), 0)
    t = qb * Q_BLOCK + (row & (Q_BLOCK - 1))
    n = lax.broadcasted_iota(jnp.int32, (QROWS, ncp), 1)
    mask = n * CMP_STRIDE + (CMP_BLOCK - 1) <= t
    s = jnp.where(mask, s, -jnp.inf)
    m = jnp.max(s, -1, keepdims=True)
    m = jnp.where(m == -jnp.inf, 0.0, m)
    e = jnp.exp(s - m)
    p = e / jnp.maximum(jnp.sum(e, -1, keepdims=True), 1e-30)
    oc_ref[0, 0, 0] = jnp.dot(p.astype(jnp.bfloat16), vc_ref[0, 0], preferred_element_type=jnp.float32)
    pc = p[0:Q_BLOCK]
    for g in range(1, NSA_GROUP):
        pc = pc + p[g * Q_BLOCK:(g + 1) * Q_BLOCK]
    score = jnp.dot(pc.astype(jnp.bfloat16), ov_ref[...], preferred_element_type=jnp.float32)
    j = lax.broadcasted_iota(jnp.int32, (Q_BLOCK, nsel), 1)
    tq = qb * Q_BLOCK + lax.broadcasted_iota(jnp.int32, (Q_BLOCK, nsel), 0)
    tb = tq >> (SEL_BLOCK.bit_length() - 1)
    forced = (j == 0) | (j == tb) | (j == tb - 1)
    score = jnp.where(forced, FORCE_SCORE, score)
    score = jnp.where(j <= tb, score, -1.0)
    jf = j.astype(jnp.float32)
    sel = jnp.zeros((Q_BLOCK, nsel), jnp.bool_)
    for _ in range(min(SEL_TOPN, nsel)):
        top = jnp.max(score, -1, keepdims=True)
        first = jnp.min(jnp.where(score == top, jf, float(nsel)), -1, keepdims=True)
        pick = jf == first
        sel = sel | pick
        score = jnp.where(pick, -2.0, score)
    bias_ref[0, 0] = jnp.where(sel, 0.0, MASK_BIAS).astype(jnp.bfloat16)


def _slc_body(q_ref, bias_ref, kaug_ref, vaug_ref, o_ref, qaug_sc, m_sc, acc_sc):
    qb = pl.program_id(2)
    start = qb * Q_BLOCK
    nsel = bias_ref.shape[3]
    bias = bias_ref[0, 0]
    for g in range(NSA_GROUP):
        qaug_sc[g * Q_BLOCK:(g + 1) * Q_BLOCK, 0:nsel] = bias
    qaug_sc[:, nsel:nsel + HEAD_DIM] = q_ref[0, 0, 0]
    m_sc[...] = jnp.full(m_sc.shape, NEG_INIT, jnp.float32)
    acc_sc[...] = jnp.zeros(acc_sc.shape, jnp.float32)

    def scores(kt):
        k0 = pl.multiple_of(kt * SLC_TK, SLC_TK)
        return lax.dot_general(qaug_sc[...], kaug_ref[0, 0, pl.ds(k0, SLC_TK), :], _NT,
                               preferred_element_type=jnp.float32)

    def absorb(kt, s, causal):
        k0 = pl.multiple_of(kt * SLC_TK, SLC_TK)
        slabs = [s[:, c * LANE:(c + 1) * LANE] for c in range(SLC_TK // LANE)]
        if causal:
            row = lax.broadcasted_iota(jnp.int32, (QROWS, LANE), 0)
            col = lax.broadcasted_iota(jnp.int32, (QROWS, LANE), 1)
            t_rel = start + (row & (Q_BLOCK - 1)) - k0 - col
            slabs = [jnp.where(t_rel >= c * LANE, sl, -jnp.inf) for c, sl in enumerate(slabs)]
        top = slabs[0]
        for sl in slabs[1:]:
            top = jnp.maximum(top, sl)
        m_prev = m_sc[...]
        m_new = jnp.maximum(m_prev, jnp.max(top, -1, keepdims=True))
        p = jnp.concatenate([jnp.exp(sl - m_new) for sl in slabs], axis=1).astype(jnp.bfloat16)
        acc_sc[...] = jnp.exp(m_prev - m_new) * acc_sc[...] + jnp.dot(
            p, vaug_ref[0, 0, pl.ds(k0, SLC_TK), :], preferred_element_type=jnp.float32)
        m_sc[...] = m_new

    n_full = start // SLC_TK

    def body(kt, s_cur):
        s_next = scores(kt + 1)
        absorb(kt, s_cur, False)
        return s_next

    s_last = lax.fori_loop(0, n_full, body, scores(0))
    absorb(n_full, s_last, True)
    acc = acc_sc[...]
    o_ref[0, 0, 0] = acc[:, :HEAD_DIM] / jnp.maximum(acc[:, HEAD_DIM:HEAD_DIM + 1], 1e-30)


def _win_body(q_ref, kpad_ref, vpad_ref, o_ref):
    qb = pl.program_id(2)
    start = pl.multiple_of(qb * Q_BLOCK, Q_BLOCK)
    nk = Q_BLOCK + WINDOW
    k = kpad_ref[0, 0, pl.ds(start, nk), :]
    s = lax.dot_general(q_ref[0, 0, 0], k, _NT, preferred_element_type=jnp.float32)
    r = lax.broadcasted_iota(jnp.int32, (QROWS, nk), 0) & (Q_BLOCK - 1)
    c = lax.broadcasted_iota(jnp.int32, (QROWS, nk), 1)
    d = r + WINDOW - c
    mask = (d >= 0) & (d < WINDOW) & (c >= WINDOW - start)
    s = jnp.where(mask, s, -jnp.inf)
    m = jnp.max(s, -1, keepdims=True)
    m = jnp.where(m == -jnp.inf, 0.0, m)
    p = jnp.exp(s - m)
    pv = jnp.dot(p.astype(jnp.bfloat16), vpad_ref[0, 0, pl.ds(start, nk), :], preferred_element_type=jnp.float32)
    o_ref[0, 0, 0] = pv[:, :HEAD_DIM] / jnp.maximum(pv[:, HEAD_DIM:HEAD_DIM + 1], 1e-30)


def _to_tiles(u, scale=None):
    b, s = u.shape[:2]
    u = u.reshape(b, s // Q_BLOCK, Q_BLOCK, NSA_KV_HEADS, NSA_GROUP, HEAD_DIM)
    u = jnp.transpose(u, (0, 3, 1, 4, 2, 5)).reshape(b, NSA_KV_HEADS, s // Q_BLOCK, QROWS, HEAD_DIM)
    return u if scale is None else (u * scale).astype(jnp.bfloat16)


def _from_tiles(o):
    b, _, nq, _, x = o.shape
    o = o.reshape(b, NSA_KV_HEADS, nq, NSA_GROUP, Q_BLOCK, x)
    return jnp.transpose(o, (0, 2, 4, 1, 3, 5)).reshape(b, nq * Q_BLOCK, NSA_KV_HEADS, NSA_GROUP, x)


def _head_major(kv):
    return jnp.transpose(kv, (0, 2, 1, 3))


def _with_ones(v):
    ones = jnp.ones(v.shape[:-1] + (1,), v.dtype)
    zeros = jnp.zeros(v.shape[:-1] + (V_AUG - HEAD_DIM - 1,), v.dtype)
    return jnp.concatenate([v, ones, zeros], -1).astype(jnp.bfloat16)


def nsa_prompt(z, cmp_pos, w_cmp):
    q, kv_cmp, kv_slc, kv_win, gates = split_nsa(z)
    b, s = z.shape[:2]
    nq = s // Q_BLOCK
    nsel = s // SEL_BLOCK
    assert s % SLC_TK == 0 and nsel % LANE == 0
    summ = compress_blocks(kv_cmp, cmp_pos, w_cmp)
    n_cmp = summ.shape[1]
    ncp = -(-n_cmp // LANE) * LANE
    summ = jnp.pad(summ, ((0, 0), (0, ncp - n_cmp), (0, 0), (0, 0), (0, 0))).astype(jnp.bfloat16)
    kc, vc = _head_major(summ[:, :, 0]), _head_major(summ[:, :, 1])
    i = jnp.arange(ncp)[:, None]
    jj = jnp.arange(nsel)[None, :]
    overlap = jnp.clip(jnp.minimum(i * CMP_STRIDE + CMP_BLOCK, (jj + 1) * SEL_BLOCK)
                       - jnp.maximum(i * CMP_STRIDE, jj * SEL_BLOCK), 0, None)
    ov = (overlap.astype(jnp.float32) / CMP_BLOCK).astype(jnp.bfloat16)
    qt = _to_tiles(q, HEAD_DIM ** -0.5)

    grid = (b, NSA_KV_HEADS, nq)
    params = pltpu.CompilerParams(dimension_semantics=("arbitrary",) * 3, vmem_limit_bytes=VMEM_LIMIT)
    q_spec = pl.BlockSpec((1, 1, 1, QROWS, HEAD_DIM), lambda bi, h, qi: (bi, h, qi, 0, 0))
    o_spec = pl.BlockSpec((1, 1, 1, QROWS, HEAD_DIM), lambda bi, h, qi: (bi, h, qi, 0, 0))
    o_shape = jax.ShapeDtypeStruct((b, NSA_KV_HEADS, nq, QROWS, HEAD_DIM), jnp.float32)
    seq_spec = lambda rows, width: pl.BlockSpec((1, 1, rows, width), lambda bi, h, qi: (bi, h, 0, 0))
    bias_spec = pl.BlockSpec((1, 1, Q_BLOCK, nsel), lambda bi, h, qi: (bi, h, qi, 0))

    o_c, bias = pl.pallas_call(
        _cmp_select_body, grid=grid,
        in_specs=[q_spec, seq_spec(ncp, HEAD_DIM), seq_spec(ncp, HEAD_DIM),
                  pl.BlockSpec((ncp, nsel), lambda bi, h, qi: (0, 0))],
        out_specs=[o_spec, bias_spec],
        out_shape=[o_shape, jax.ShapeDtypeStruct((b, NSA_KV_HEADS, s, nsel), jnp.bfloat16)],
        compiler_params=params, name="nsa_cmp_select",
    )(qt, kc, vc, ov)

    k_s, v_s = _head_major(kv_slc[:, :, 0]), _head_major(kv_slc[:, :, 1])
    onehot = (jnp.arange(s)[:, None] // SEL_BLOCK == jnp.arange(nsel)[None, :]).astype(jnp.bfloat16)
    kaug = jnp.concatenate([jnp.broadcast_to(onehot, (b, NSA_KV_HEADS, s, nsel)), k_s.astype(jnp.bfloat16)], -1)
    o_s = pl.pallas_call(
        _slc_body, grid=grid,
        in_specs=[q_spec, bias_spec, seq_spec(s, nsel + HEAD_DIM), seq_spec(s, V_AUG)],
        out_specs=o_spec, out_shape=o_shape,
        scratch_shapes=[pltpu.VMEM((QROWS, nsel + HEAD_DIM), jnp.bfloat16),
                        pltpu.VMEM((QROWS, LANE), jnp.float32),
                        pltpu.VMEM((QROWS, V_AUG), jnp.float32)],
        compiler_params=params, name="nsa_slc",
    )(qt, bias, kaug, _with_ones(v_s))

    front = ((0, 0), (0, 0), (WINDOW, 0), (0, 0))
    k_w = jnp.pad(_head_major(kv_win[:, :, 0]).astype(jnp.bfloat16), front)
    v_w = jnp.pad(_with_ones(_head_major(kv_win[:, :, 1])), front)
    o_w = pl.pallas_call(
        _win_body, grid=grid,
        in_specs=[q_spec, seq_spec(s + WINDOW, HEAD_DIM), seq_spec(s + WINDOW, V_AUG)],
        out_specs=o_spec, out_shape=o_shape,
        compiler_params=params, name="nsa_win",
    )(qt, k_w, v_w)

    o = (_from_tiles(o_c) * gates[..., 0:1] + _from_tiles(o_s) * gates[..., 1:2]
         + _from_tiles(o_w) * gates[..., 2:3]).reshape(b, s, NSA_WIDTH)
    return o, (kv_cmp, kv_slc, kv_win[:, s - min(WINDOW, s):])


RWKV_PAIRS = RWKV_HEADS // 2
RWKV_TC = 256
RWKV_NB = 2
RWKV_GROUP = 8
V_PIECES = 4
V_STEPS = LANE // (V_PIECES * RWKV_HEADS)


def _rwkv_body(nkk_ref, w_ref, bb_ref, k_ref, r_ref, vt_ref, e2_ref, ebd_ref, s0_ref, o_ref, sT_ref, s_sc):
    nb, tc = nkk_ref.shape[0], nkk_ref.shape[1]
    ci = pl.program_id(1)

    @pl.when(ci == 0)
    def _():
        s_sc[...] = s0_ref[...]

    lane = lax.broadcasted_iota(jnp.int32, (1, LANE), 1)
    left = lane < HEAD_DIM
    vlane = lax.broadcasted_iota(jnp.int32, (HEAD_DIM, LANE), 1)
    vstep = (vlane >> (RWKV_HEADS.bit_length() - 1)) & (V_STEPS - 1)

    def one_step(b, t0, i, rows):
        nkk, w, bb, k, r = rows
        t = t0 + i
        row = lambda u, p: u[i:i + 1, p * LANE:(p + 1) * LANE]
        vm = jnp.where(vstep == i % V_STEPS, vt_ref[b, t0 // V_STEPS + i // V_STEPS], jnp.zeros((), jnp.bfloat16))
        vb = jnp.dot(vm, e2_ref[...], preferred_element_type=jnp.float32)
        rbd = (ebd_ref[...] * r[i:i + 1, :]).astype(jnp.bfloat16)
        o = jnp.zeros((RWKV_HEADS, HEAD_DIM), jnp.float32)
        for p in range(RWKV_PAIRS):
            s = s_sc[b, p]
            c = row(nkk, p)
            c_l = jnp.where(left, c, 0.0)
            c_r = jnp.where(left, 0.0, c)
            sa = jnp.where(left, jnp.sum(s * c_l, -1, keepdims=True), jnp.sum(s * c_r, -1, keepdims=True))
            s = s * row(w, p) + sa * row(bb, p) + vb[:, p * LANE:(p + 1) * LANE] * row(k, p)
            s_sc[b, p] = s
            o = o + lax.dot_general(rbd[:, p * LANE:(p + 1) * LANE], s.astype(jnp.bfloat16), _NT,
                                    preferred_element_type=jnp.float32)
        o_ref[b, pl.ds(pl.multiple_of(t * RWKV_HEADS, RWKV_HEADS), RWKV_HEADS), :] = o

    group = min(tc, RWKV_GROUP)

    def group_steps(gi, carry):
        t0 = pl.multiple_of(gi * group, group)
        rows = [tuple(ref[b, pl.ds(t0, group), :] for ref in (nkk_ref, w_ref, bb_ref, k_ref, r_ref)) for b in range(nb)]
        for i in range(group):
            for b in range(nb):
                one_step(b, t0, i, rows[b])
        return carry

    lax.fori_loop(0, tc // group, group_steps, 0)
    sT_ref[...] = s_sc[...]


def rwkv_scan(r, k, v, kk, a, decay, wkv0):
    b, l = r.shape[:2]
    tc = min(l, RWKV_TC)
    nb = RWKV_NB
    assert l % tc == 0 and tc % V_STEPS == 0 and b % nb == 0
    vh = v.reshape(b, l, RWKV_HEADS, HEAD_DIM)
    v1 = vh.astype(jnp.bfloat16)
    res = vh - v1.astype(jnp.float32)
    v2 = res.astype(jnp.bfloat16)
    v3 = (res - v2.astype(jnp.float32)).astype(jnp.bfloat16)
    pieces = jnp.stack([v1, v2, v3, jnp.zeros_like(v1)], 0)
    pieces = pieces.reshape(V_PIECES, b, l // V_STEPS, V_STEPS, RWKV_HEADS, HEAD_DIM)
    vt = jnp.transpose(pieces, (1, 2, 5, 0, 3, 4)).reshape(b, l // V_STEPS, HEAD_DIM, LANE)
    lane_head = jnp.arange(LANE) % RWKV_HEADS
    col_head = jnp.arange(RWKV_WIDTH) // HEAD_DIM
    e2 = (lane_head[:, None] == col_head[None, :]).astype(jnp.bfloat16)
    ebd = (jnp.arange(RWKV_HEADS)[:, None] == col_head[None, :]).astype(jnp.float32)
    to_pairs = lambda s: jnp.transpose(s.reshape(b, RWKV_PAIRS, 2, HEAD_DIM, HEAD_DIM), (0, 1, 3, 2, 4)
                                       ).reshape(b, RWKV_PAIRS, HEAD_DIM, LANE)
    s0 = to_pairs(wkv0.astype(jnp.float32))

    row_spec = pl.BlockSpec((nb, tc, RWKV_WIDTH), lambda bi, ci: (bi, ci, 0))
    state_spec = pl.BlockSpec((nb, RWKV_PAIRS, HEAD_DIM, LANE), lambda bi, ci: (bi, 0, 0, 0))
    o, s_t = pl.pallas_call(
        _rwkv_body, grid=(b // nb, l // tc),
        in_specs=[row_spec] * 5 + [
            pl.BlockSpec((nb, tc // V_STEPS, HEAD_DIM, LANE), lambda bi, ci: (bi, ci, 0, 0)),
            pl.BlockSpec((LANE, RWKV_WIDTH), lambda bi, ci: (0, 0)),
            pl.BlockSpec((RWKV_HEADS, RWKV_WIDTH), lambda bi, ci: (0, 0)),
            state_spec],
        out_specs=[pl.BlockSpec((nb, tc * RWKV_HEADS, HEAD_DIM), lambda bi, ci: (bi, ci, 0)), state_spec],
        out_shape=[jax.ShapeDtypeStruct((b, l * RWKV_HEADS, HEAD_DIM), jnp.float32),
                   jax.ShapeDtypeStruct((b, RWKV_PAIRS, HEAD_DIM, LANE), jnp.float32)],
        scratch_shapes=[pltpu.VMEM((nb, RWKV_PAIRS, HEAD_DIM, LANE), jnp.float32)],
        compiler_params=pltpu.CompilerParams(dimension_semantics=("arbitrary", "arbitrary"),
                                             vmem_limit_bytes=VMEM_LIMIT),
        name="rwkv_scan",
    )(-kk, decay, kk * a, k, r, vt, e2, ebd, s0)
    s_t = jnp.transpose(s_t.reshape(b, RWKV_PAIRS, HEAD_DIM, 2, HEAD_DIM), (0, 1, 3, 2, 4))
    return o.reshape(b, l, RWKV_HEADS, HEAD_DIM), s_t.reshape(b, RWKV_HEADS, HEAD_DIM, HEAD_DIM)


MOE_ROWS = 512


def _expert_body(blk_e_ref, n_used_ref, x_ref, wgu_ref, bgu_ref, wd_ref, bd_ref, o_ref):
    i = pl.program_id(0)

    @pl.when(i < n_used_ref[0])
    def _():
        hgu = jnp.dot(x_ref[...], wgu_ref[0], preferred_element_type=jnp.float32) + bgu_ref[0]
        gate = jnp.minimum(hgu[:, :D_FF], SWIGLU_LIMIT)
        up = jnp.clip(hgu[:, D_FF:], -SWIGLU_LIMIT, SWIGLU_LIMIT)
        act = (up + 1.0) * gate * jax.nn.sigmoid(SWIGLU_ALPHA * gate)
        o_ref[...] = jnp.dot(act.astype(jnp.bfloat16), wd_ref[0], preferred_element_type=jnp.float32) + bd_ref[0]

    @pl.when(i >= n_used_ref[0])
    def _():
        o_ref[...] = jnp.zeros(o_ref.shape, o_ref.dtype)


def expert_blocks(xb, blk_e, n_used, w_gate_up, b_gate_up, w_down, b_down):
    n_rows, d = xb.shape
    n_blocks = n_rows // MOE_ROWS
    wgu = w_gate_up.astype(jnp.bfloat16)
    wd = w_down.astype(jnp.bfloat16)
    grid_spec = pltpu.PrefetchScalarGridSpec(
        num_scalar_prefetch=2, grid=(n_blocks,),
        in_specs=[pl.BlockSpec((MOE_ROWS, d), lambda i, e, n: (i, 0)),
                  pl.BlockSpec((1, d, 2 * D_FF), lambda i, e, n: (e[i], 0, 0)),
                  pl.BlockSpec((1, 1, 2 * D_FF), lambda i, e, n: (e[i], 0, 0)),
                  pl.BlockSpec((1, D_FF, d), lambda i, e, n: (e[i], 0, 0)),
                  pl.BlockSpec((1, 1, d), lambda i, e, n: (e[i], 0, 0))],
        out_specs=pl.BlockSpec((MOE_ROWS, d), lambda i, e, n: (i, 0)))
    return pl.pallas_call(
        _expert_body, grid_spec=grid_spec,
        out_shape=jax.ShapeDtypeStruct((n_rows, d), jnp.float32),
        compiler_params=pltpu.CompilerParams(dimension_semantics=("arbitrary",), vmem_limit_bytes=VMEM_LIMIT),
        name="moe_experts",
    )(blk_e, n_used, xb, wgu, b_gate_up.reshape(N_EXPERTS, 1, 2 * D_FF), wd, b_down.reshape(N_EXPERTS, 1, d))


def rms_norm(x, g):
    xf = x.astype(jnp.float32)
    y = xf * lax.rsqrt(jnp.mean(xf * xf, -1, keepdims=True) + RMS_EPS)
    return (y * g.astype(jnp.float32)).astype(x.dtype)


def masked_softmax(s, mask):
    s = jnp.where(mask, s.astype(jnp.float32), -jnp.inf)
    m = jnp.max(s, -1, keepdims=True)
    m = jnp.where(jnp.isfinite(m), m, 0.0)
    e = jnp.where(mask, jnp.exp(s - m), 0.0)
    return e / jnp.maximum(jnp.sum(e, -1, keepdims=True), 1e-30)


def split_nsa(z):
    b, l = z.shape[:2]
    q = z[..., :NSA_WIDTH].reshape(b, l, NSA_KV_HEADS, NSA_GROUP, HEAD_DIM)
    kv = z[..., NSA_WIDTH:NSA_WIDTH + 6 * KV_WIDTH].reshape(b, l, 3, 2, NSA_KV_HEADS, HEAD_DIM)
    gates = jax.nn.sigmoid(z[..., NSA_WIDTH + 6 * KV_WIDTH:]).reshape(b, l, NSA_KV_HEADS, NSA_GROUP, 3)
    return q, kv[:, :, 0], kv[:, :, 1], kv[:, :, 2], gates


def compress_blocks(kv, cmp_pos, w_cmp):
    b, l = kv.shape[:2]
    chunks = kv.reshape(b, l // CMP_STRIDE, CMP_STRIDE, 2, NSA_KV_HEADS, HEAD_DIM)
    lo = jnp.einsum('bnlchd,cldo->bncho', chunks + cmp_pos[None, None, :CMP_STRIDE, :, None, :], w_cmp[:, :CMP_STRIDE])
    hi = jnp.einsum('bnlchd,cldo->bncho', chunks + cmp_pos[None, None, CMP_STRIDE:, :, None, :], w_cmp[:, CMP_STRIDE:])
    return lo[:, :-1] + hi[:, 1:]


def cmp_branch(qg, t, summ):
    n = summ.shape[1]
    end = jnp.arange(n) * CMP_STRIDE + (CMP_BLOCK - 1)
    mask = end[None, :] <= t[:, None]
    s = jnp.einsum('bqhgd,bnhd->bhgqn', qg, summ[:, :, 0]) * (HEAD_DIM ** -0.5)
    p = masked_softmax(s, mask)
    o = jnp.einsum('bhgqn,bnhd->bqhgd', p.astype(qg.dtype), summ[:, :, 1])
    return o, jnp.sum(p, axis=2)


def select_blocks(p_cmp, t, n_sel):
    n_cmp = p_cmp.shape[-1]
    i = jnp.arange(n_cmp)[:, None]
    j = jnp.arange(n_sel)[None, :]
    overlap = jnp.clip(jnp.minimum(i * CMP_STRIDE + CMP_BLOCK, (j + 1) * SEL_BLOCK)
                       - jnp.maximum(i * CMP_STRIDE, j * SEL_BLOCK), 0, None)
    score = jnp.einsum('bhqn,nj->bhqj', p_cmp, overlap.astype(jnp.float32) / CMP_BLOCK)
    tb = (t // SEL_BLOCK)[:, None]
    forced = (j == 0) | (j == tb) | (j == tb - 1)
    score = jnp.where(forced, FORCE_SCORE, score)
    score = jnp.where(j <= tb, score, -1.0)
    top_s, idx = lax.top_k(score, min(SEL_TOPN, n_sel))
    return idx, top_s >= 0.0


def slc_branch(qg, t, kv_sel, idx, ok):
    b, h, q, n = idx.shape
    kpos = idx[..., None] * SEL_BLOCK + jnp.arange(SEL_BLOCK)
    mask = (ok[..., None] & (kpos <= t[None, None, :, None, None])).reshape(b, h, 1, q, n * SEL_BLOCK)
    s = jnp.einsum('bqhgd,bhqnld->bhgqnl', qg, kv_sel[..., 0, :]) * (HEAD_DIM ** -0.5)
    p = masked_softmax(s.reshape(b, h, NSA_GROUP, q, n * SEL_BLOCK), mask)
    v = kv_sel[..., 1, :].reshape(b, h, q, n * SEL_BLOCK, HEAD_DIM)
    return jnp.einsum('bhgqk,bhqkd->bqhgd', p.astype(qg.dtype), v)


def win_branch(qg, t, kv_w, kpos):
    d = t[:, None] - kpos[None, :]
    mask = (d >= 0) & (d < WINDOW) & (kpos[None, :] >= 0)
    s = jnp.einsum('bqhgd,bkhd->bhgqk', qg, kv_w[:, :, 0]) * (HEAD_DIM ** -0.5)
    p = masked_softmax(s, mask)
    return jnp.einsum('bhgqk,bkhd->bqhgd', p.astype(qg.dtype), kv_w[:, :, 1])


def gate_merge(o_c, o_s, o_w, g):
    o = o_c * g[..., 0:1] + o_s * g[..., 1:2] + o_w * g[..., 2:3]
    return o.reshape(o.shape[0], o.shape[1], NSA_WIDTH)


def nsa_sample(z, cache_cmp, cache_slc, page_table, state_win, cmp_pos, w_cmp):
    q, kv_cmp, kv_slc, kv_win, gates = split_nsa(z)
    b, n = z.shape[:2]
    n_pages = page_table.shape[1]
    past = n_pages * PAGE_SIZE
    total = past + n
    t = past + jnp.arange(n)
    b_ix = jnp.arange(b)[:, None, None, None]
    h_ix = jnp.arange(NSA_KV_HEADS)[None, :, None, None]
    past_cmp = cache_cmp[page_table].reshape(b, past, 2, NSA_KV_HEADS, HEAD_DIM)
    pad = (-total) % CMP_STRIDE
    full = jnp.concatenate([past_cmp, kv_cmp, jnp.zeros((b, pad, 2, NSA_KV_HEADS, HEAD_DIM), kv_cmp.dtype)], axis=1)
    summ = compress_blocks(full, cmp_pos, w_cmp)
    o_c, p_c = cmp_branch(q, t, summ)
    n_sel = -(-total // SEL_BLOCK)
    idx, ok = select_blocks(p_c, t, n_sel)
    past_blocks = past // SEL_BLOCK
    bpp = PAGE_SIZE // SEL_BLOCK
    phys = page_table[b_ix, jnp.minimum(idx // bpp, n_pages - 1)]
    blk_row = (phys * bpp + idx % bpp)[..., None, None] * SEL_BLOCK + jnp.arange(SEL_BLOCK)[:, None]
    flat_row = (blk_row * 2 + jnp.arange(2)) * NSA_KV_HEADS + h_ix[..., None, None]
    from_past = cache_slc.reshape(-1, HEAD_DIM)[flat_row]
    n_new_blk = -(-n // SEL_BLOCK)
    new_blocks = jnp.pad(kv_slc, ((0, 0), (0, n_new_blk * SEL_BLOCK - n), (0, 0), (0, 0), (0, 0)))
    new_blocks = new_blocks.reshape(b, n_new_blk, SEL_BLOCK, 2, NSA_KV_HEADS, HEAD_DIM)
    from_new = new_blocks[b_ix, jnp.clip(idx - past_blocks, 0, n_new_blk - 1), :, :, h_ix, :]
    kv_sel = jnp.where((idx >= past_blocks)[..., None, None, None], from_new, from_past)
    o_s = slc_branch(q, t, kv_sel, idx, ok)
    wb = state_win.shape[1]
    kv_w = jnp.concatenate([state_win, kv_win], axis=1)
    o_w = win_branch(q, t, kv_w, past - wb + jnp.arange(wb + n))
    o = gate_merge(o_c, o_s, o_w, gates)
    return o, (kv_cmp, kv_slc, kv_w[:, n:])


def rwkv7_time_mix(z, prev_row, wkv0, shift_mu, w0, w2, a0, a2, g2, k_k, k_a, r_k, gn_g, gn_b):
    b, l = z.shape[:2]
    zf = z.astype(jnp.float32)
    prev = jnp.concatenate([prev_row[:, None].astype(jnp.float32), zf[:, :-1]], axis=1)
    zs = zf + shift_mu * (prev - zf)
    r, k, v, wd, ad, gd = jnp.split(zs, RWKV_SPLITS, axis=-1)
    w = -jax.nn.softplus(-(w0 + jnp.tanh(wd) @ w2)) - 0.5
    decay = jnp.exp(-jnp.exp(w))
    a = jax.nn.sigmoid(a0 + ad @ a2)
    g = jax.nn.sigmoid(gd) @ g2
    heads = lambda u: u.reshape(b, l, RWKV_HEADS, HEAD_DIM)
    kk = heads(k * k_k)
    kk = kk / jnp.maximum(jnp.sqrt(jnp.sum(kk * kk, -1, keepdims=True)), 1e-12)
    k = k * (1.0 + (a - 1.0) * k_a)
    o, s_final = rwkv_scan(r, k, v, kk.reshape(b, l, RWKV_WIDTH), a, decay, wkv0)
    r, k, v = heads(r), heads(k), heads(v)
    mu = jnp.mean(o, -1, keepdims=True)
    var = jnp.mean(jnp.square(o - mu), -1, keepdims=True)
    o = ((o - mu) * lax.rsqrt(var + RWKV_GN_EPS)).reshape(b, l, RWKV_WIDTH) * gn_g + gn_b
    bonus = (jnp.sum(r * k * r_k, -1, keepdims=True) * v).reshape(b, l, RWKV_WIDTH)
    out = (o + bonus) * g
    return out.astype(z.dtype), s_final, z[:, -1]


def moe_ffn(x, w_router, b_router, w_gate_up, b_gate_up, w_down, b_down):
    shape = x.shape
    xt = x.reshape(-1, shape[-1])
    n_tok = xt.shape[0]
    logits = (xt @ w_router).astype(jnp.float32) + b_router.astype(jnp.float32)
    top_v, top_e = lax.top_k(logits, TOP_K)
    gates = jax.nn.softmax(top_v, axis=-1)
    flat_e = top_e.reshape(-1)
    n_assign = n_tok * TOP_K
    order = jnp.argsort(flat_e)
    sorted_e = flat_e[order]
    counts = jnp.bincount(flat_e, length=N_EXPERTS)
    rows = MOE_ROWS
    padded = (counts + rows - 1) // rows * rows
    pad_end = jnp.cumsum(padded)
    pad_start = pad_end - padded
    grp_start = jnp.cumsum(counts) - counts
    dest_sorted = pad_start[sorted_e] + jnp.arange(n_assign) - grp_start[sorted_e]
    n_blocks = -(-n_assign // rows) + N_EXPERTS
    n_rows = n_blocks * rows
    row_token = jnp.full((n_rows,), n_tok, jnp.int32).at[dest_sorted].set((order // TOP_K).astype(jnp.int32))
    x_pad = jnp.concatenate([xt.astype(jnp.bfloat16), jnp.zeros((1, shape[-1]), jnp.bfloat16)], axis=0)
    xb = x_pad[row_token]
    blk_e = jnp.minimum(jnp.sum(pad_end[None, :] <= (jnp.arange(n_blocks) * rows)[:, None], axis=1), N_EXPERTS - 1)
    n_used = (pad_end[-1] // rows).astype(jnp.int32).reshape(1)
    yb = expert_blocks(xb, blk_e.astype(jnp.int32), n_used, w_gate_up, b_gate_up, w_down, b_down)
    dest = jnp.zeros((n_assign,), dest_sorted.dtype).at[order].set(dest_sorted)
    y = yb[dest].reshape(n_tok, TOP_K, shape[-1])
    return jnp.sum(y * gates[..., None], axis=1).reshape(shape)


def kernel(x_prompt, x_sample, cache_cmp, cache_slc, page_table, state_win, state_wkv, state_shift,
           norm_mix_g, w_in, cmp_pos, w_cmp, shift_mu, w0, w2, a0, a2, g2, k_k, k_a, r_k, gn_g, gn_b,
           w_out, norm_ffn_g, w_router, b_router, w_gate_up, b_gate_up, w_down, b_down, norm_final_g):
    xp, xs = x_prompt, x_sample
    l = 0
    rw = (shift_mu[l], w0[l], w2[l], a0[l], a2[l], g2[l], k_k[l], k_a[l], r_k[l], gn_g[l], gn_b[l])
    ffn = (w_router[l], b_router[l], w_gate_up[l], b_gate_up[l], w_down[l], b_down[l])
    bp, sp = xp.shape[:2]
    bs, ss = xs.shape[:2]
    zp = norm_proj(xp.reshape(bp * sp, D_MODEL), norm_mix_g[l], w_in[l]).reshape(bp, sp, IN_COLS)
    oa, (kc_p, ksl_p, kw_p) = nsa_prompt(zp[..., :NSA_COLS], cmp_pos[l], w_cmp[l])
    ob, wkv_p, sh_p = rwkv7_time_mix(zp[..., NSA_COLS:], jnp.zeros((bp, RWKV_COLS), zp.dtype),
                                     jnp.zeros((bp, RWKV_HEADS, HEAD_DIM, HEAD_DIM), jnp.float32), *rw)
    xp = xp + jnp.concatenate([oa, ob], axis=-1) @ w_out[l]

    zs = norm_proj(xs.reshape(bs * ss, D_MODEL), norm_mix_g[l], w_in[l]).reshape(bs, ss, IN_COLS)
    oa, (kc_s, ksl_s, kw_s) = nsa_sample(zs[..., :NSA_COLS], cache_cmp[l], cache_slc[l], page_table, state_win[l],
                                         cmp_pos[l], w_cmp[l])
    ob, wkv_s, sh_s = rwkv7_time_mix(zs[..., NSA_COLS:], state_shift[l], state_wkv[l], *rw)
    xs = xs + jnp.concatenate([oa, ob], axis=-1) @ w_out[l]
    x_all = jnp.concatenate([xp.reshape(bp * sp, D_MODEL), xs.reshape(bs * ss, D_MODEL)], axis=0)
    m_all = moe_ffn(rms_norm(x_all, norm_ffn_g[l]), *ffn)
    xp = xp + m_all[:bp * sp].reshape(xp.shape)
    xs = xs + m_all[bp * sp:].reshape(xs.shape)

    y_prompt = rms_norm(xp, norm_final_g)
    y_sample = rms_norm(xs, norm_final_g)
    st = lambda u: u[None]
    return (y_prompt, y_sample, st(kc_p), st(ksl_p), st(kw_p), st(wkv_p), st(sh_p),
            st(kc_s), st(ksl_s), st(kw_s), st(wkv_s), st(sh_s))
```

```python
import functools
import math

import jax
import jax.numpy as jnp
from jax import lax
from jax.experimental import pallas as pl
from jax.experimental.pallas import tpu as pltpu

D_MODEL = 1024
DEPTH = 1
PAGE_SIZE = 128

HEAD_DIM = 64
NSA_HEADS = 8
NSA_KV_HEADS = 2
NSA_GROUP = NSA_HEADS // NSA_KV_HEADS
CMP_BLOCK = 32
CMP_STRIDE = 16
SEL_BLOCK = 64
SEL_TOPN = 16
WINDOW = 512
Q_BLOCK = 128
FORCE_SCORE = 1e6
RWKV_HEADS = 8
RWKV_DECAY_LORA = 64
RWKV_A_LORA = 64
RWKV_GATE_LORA = 128
RWKV_GN_EPS = 64e-5
N_EXPERTS = 32
TOP_K = 4
D_FF = 1024
SWIGLU_LIMIT = 7.0
SWIGLU_ALPHA = 1.702
EXPERT_BLOCK = 128
RMS_EPS = 1e-5

NSA_WIDTH = NSA_HEADS * HEAD_DIM
KV_WIDTH = NSA_KV_HEADS * HEAD_DIM
RWKV_WIDTH = RWKV_HEADS * HEAD_DIM
MIX_WIDTH = NSA_WIDTH + RWKV_WIDTH
NSA_COLS = NSA_WIDTH + 6 * KV_WIDTH + 3 * NSA_HEADS
RWKV_COLS = 3 * RWKV_WIDTH + RWKV_DECAY_LORA + RWKV_A_LORA + RWKV_GATE_LORA
IN_COLS = NSA_COLS + RWKV_COLS
RWKV_SPLITS = (RWKV_WIDTH, 2 * RWKV_WIDTH, 3 * RWKV_WIDTH, 3 * RWKV_WIDTH + RWKV_DECAY_LORA,
               3 * RWKV_WIDTH + RWKV_DECAY_LORA + RWKV_A_LORA)

LANE = 128
VMEM_LIMIT = 48 * 1024 * 1024


def _norm_proj_body(x_ref, g_ref, w_ref, o_ref):
    x = x_ref[...]
    y = x * lax.rsqrt(jnp.mean(x * x, -1, keepdims=True) + RMS_EPS) * g_ref[...]
    o_ref[...] = jnp.dot(y.astype(jnp.bfloat16), w_ref[...], preferred_element_type=jnp.float32)


def norm_proj(x, g, w, block_rows=512):
    n, d = x.shape
    c = w.shape[1]
    cp = -(-c // LANE) * LANE
    wb = jnp.pad(w, ((0, 0), (0, cp - c))).astype(jnp.bfloat16)
    out = pl.pallas_call(
        _norm_proj_body,
        grid=(n // block_rows,),
        in_specs=[pl.BlockSpec((block_rows, d), lambda i: (i, 0)),
                  pl.BlockSpec((1, d), lambda i: (0, 0)),
                  pl.BlockSpec((d, cp), lambda i: (0, 0))],
        out_specs=pl.BlockSpec((block_rows, cp), lambda i: (i, 0)),
        out_shape=jax.ShapeDtypeStruct((n, cp), jnp.float32),
        compiler_params=pltpu.CompilerParams(dimension_semantics=("arbitrary",), vmem_limit_bytes=VMEM_LIMIT),
        name="norm_proj",
    )(x, g.reshape(1, d), wb)
    return out[:, :c]


QROWS = NSA_GROUP * Q_BLOCK
SLC_TK = 512
MASK_BIAS = -float(2 ** 30)
NEG_INIT = -3.0e38
V_AUG = 2 * HEAD_DIM

_NT = (((1,), (1,)), ((), ()))


def _cmp_select_body(q_ref, kc_ref, vc_ref, ov_ref, oc_ref, bias_ref):
    qb = pl.program_id(2)
    ncp = kc_ref.shape[2]
    nsel = ov_ref.shape[1]
    q = q_ref[0, 0, 0]
    s = lax.dot_general(q, kc_ref[0, 0], _NT, preferred_element_type=jnp.float32)
    row = lax.broadcasted_iota(jnp.int32, (QROWS, ncp), 0)
    t = qb * Q_BLOCK + (row & (Q_BLOCK - 1))
    n = lax.broadcasted_iota(jnp.int32, (QROWS, ncp), 1)
    mask = n * CMP_STRIDE + (CMP_BLOCK - 1) <= t
    s = jnp.where(mask, s, -jnp.inf)
    m = jnp.max(s, -1, keepdims=True)
    m = jnp.where(m == -jnp.inf, 0.0, m)
    e = jnp.exp(s - m)
    p = e / jnp.maximum(jnp.sum(e, -1, keepdims=True), 1e-30)
    oc_ref[0, 0, 0] = jnp.dot(p.astype(jnp.bfloat16), vc_ref[0, 0], preferred_element_type=jnp.float32)
    pc = p[0:Q_BLOCK]
    for g in range(1, NSA_GROUP):
        pc = pc + p[g * Q_BLOCK:(g + 1) * Q_BLOCK]
    score = jnp.dot(pc.astype(jnp.bfloat16), ov_ref[...], preferred_element_type=jnp.float32)
    j = lax.broadcasted_iota(jnp.int32, (Q_BLOCK, nsel), 1)
    tq = qb * Q_BLOCK + lax.broadcasted_iota(jnp.int32, (Q_BLOCK, nsel), 0)
    tb = tq >> (SEL_BLOCK.bit_length() - 1)
    forced = (j == 0) | (j == tb) | (j == tb - 1)
    score = jnp.where(forced, FORCE_SCORE, score)
    score = jnp.where(j <= tb, score, -1.0)
    jf = j.astype(jnp.float32)
    sel = jnp.zeros((Q_BLOCK, nsel), jnp.bool_)
    for _ in range(min(SEL_TOPN, nsel)):
        top = jnp.max(score, -1, keepdims=True)
        first = jnp.min(jnp.where(score == top, jf, float(nsel)), -1, keepdims=True)
        pick = jf == first
        sel = sel | pick
        score = jnp.where(pick, -2.0, score)
    bias_ref[0, 0] = jnp.where(sel, 0.0, MASK_BIAS).astype(jnp.bfloat16)


def _slc_body(q_ref, bias_ref, kaug_ref, vaug_ref, o_ref, qaug_sc, m_sc, acc_sc):
    qb = pl.program_id(2)
    start = qb * Q_BLOCK
    nsel = bias_ref.shape[3]
    bias = bias_ref[0, 0]
    for g in range(NSA_GROUP):
        qaug_sc[g * Q_BLOCK:(g + 1) * Q_BLOCK, 0:nsel] = bias
    qaug_sc[:, nsel:nsel + HEAD_DIM] = q_ref[0, 0, 0]
    m_sc[...] = jnp.full(m_sc.shape, NEG_INIT, jnp.float32)
    acc_sc[...] = jnp.zeros(acc_sc.shape, jnp.float32)

    def scores(kt):
        k0 = pl.multiple_of(kt * SLC_TK, SLC_TK)
        return lax.dot_general(qaug_sc[...], kaug_ref[0, 0, pl.ds(k0, SLC_TK), :], _NT,
                               preferred_element_type=jnp.float32)

    def absorb(kt, s, causal):
        k0 = pl.multiple_of(kt * SLC_TK, SLC_TK)
        slabs = [s[:, c * LANE:(c + 1) * LANE] for c in range(SLC_TK // LANE)]
        if causal:
            row = lax.broadcasted_iota(jnp.int32, (QROWS, LANE), 0)
            col = lax.broadcasted_iota(jnp.int32, (QROWS, LANE), 1)
            t_rel = start + (row & (Q_BLOCK - 1)) - k0 - col
            slabs = [jnp.where(t_rel >= c * LANE, sl, -jnp.inf) for c, sl in enumerate(slabs)]
        top = slabs[0]
        for sl in slabs[1:]:
            top = jnp.maximum(top, sl)
        m_prev = m_sc[...]
        m_new = jnp.maximum(m_prev, jnp.max(top, -1, keepdims=True))
        p = jnp.concatenate([jnp.exp(sl - m_new) for sl in slabs], axis=1).astype(jnp.bfloat16)
        acc_sc[...] = jnp.exp(m_prev - m_new) * acc_sc[...] + jnp.dot(
            p, vaug_ref[0, 0, pl.ds(k0, SLC_TK), :], preferred_element_type=jnp.float32)
        m_sc[...] = m_new

    n_full = start // SLC_TK

    def body(kt, s_cur):
        s_next = scores(kt + 1)
        absorb(kt, s_cur, False)
        return s_next

    s_last = lax.fori_loop(0, n_full, body, scores(0))
    absorb(n_full, s_last, True)
    acc = acc_sc[...]
    o_ref[0, 0, 0] = acc[:, :HEAD_DIM] / jnp.maximum(acc[:, HEAD_DIM:HEAD_DIM + 1], 1e-30)


def _win_body(q_ref, kpad_ref, vpad_ref, o_ref):
    qb = pl.program_id(2)
    start = pl.multiple_of(qb * Q_BLOCK, Q_BLOCK)
    nk = Q_BLOCK + WINDOW
    k = kpad_ref[0, 0, pl.ds(start, nk), :]
    s = lax.dot_general(q_ref[0, 0, 0], k, _NT, preferred_element_type=jnp.float32)
    r = lax.broadcasted_iota(jnp.int32, (QROWS, nk), 0) & (Q_BLOCK - 1)
    c = lax.broadcasted_iota(jnp.int32, (QROWS, nk), 1)
    d = r + WINDOW - c
    mask = (d >= 0) & (d < WINDOW) & (c >= WINDOW - start)
    s = jnp.where(mask, s, -jnp.inf)
    m = jnp.max(s, -1, keepdims=True)
    m = jnp.where(m == -jnp.inf, 0.0, m)
    p = jnp.exp(s - m)
    pv = jnp.dot(p.astype(jnp.bfloat16), vpad_ref[0, 0, pl.ds(start, nk), :], preferred_element_type=jnp.float32)
    o_ref[0, 0, 0] = pv[:, :HEAD_DIM] / jnp.maximum(pv[:, HEAD_DIM:HEAD_DIM + 1], 1e-30)


def _to_tiles(u, scale=None):
    b, s = u.shape[:2]
    u = u.reshape(b, s // Q_BLOCK, Q_BLOCK, NSA_KV_HEADS, NSA_GROUP, HEAD_DIM)
    u = jnp.transpose(u, (0, 3, 1, 4, 2, 5)).reshape(b, NSA_KV_HEADS, s // Q_BLOCK, QROWS, HEAD_DIM)
    return u if scale is None else (u * scale).astype(jnp.bfloat16)


def _from_tiles(o):
    b, _, nq, _, x = o.shape
    o = o.reshape(b, NSA_KV_HEADS, nq, NSA_GROUP, Q_BLOCK, x)
    return jnp.transpose(o, (0, 2, 4, 1, 3, 5)).reshape(b, nq * Q_BLOCK, NSA_KV_HEADS, NSA_GROUP, x)


def _head_major(kv):
    return jnp.transpose(kv, (0, 2, 1, 3))


def _with_ones(v):
    ones = jnp.ones(v.shape[:-1] + (1,), v.dtype)
    zeros = jnp.zeros(v.shape[:-1] + (V_AUG - HEAD_DIM - 1,), v.dtype)
    return jnp.concatenate([v, ones, zeros], -1).astype(jnp.bfloat16)


def nsa_prompt(z, cmp_pos, w_cmp):
    q, kv_cmp, kv_slc, kv_win, gates = split_nsa(z)
    b, s = z.shape[:2]
    nq = s // Q_BLOCK
    nsel = s // SEL_BLOCK
    assert s % SLC_TK == 0 and nsel % LANE == 0
    summ = compress_blocks(kv_cmp, cmp_pos, w_cmp)
    n_cmp = summ.shape[1]
    ncp = -(-n_cmp // LANE) * LANE
    summ = jnp.pad(summ, ((0, 0), (0, ncp - n_cmp), (0, 0), (0, 0), (0, 0))).astype(jnp.bfloat16)
    kc, vc = _head_major(summ[:, :, 0]), _head_major(summ[:, :, 1])
    i = jnp.arange(ncp)[:, None]
    jj = jnp.arange(nsel)[None, :]
    overlap = jnp.clip(jnp.minimum(i * CMP_STRIDE + CMP_BLOCK, (jj + 1) * SEL_BLOCK)
                       - jnp.maximum(i * CMP_STRIDE, jj * SEL_BLOCK), 0, None)
    ov = (overlap.astype(jnp.float32) / CMP_BLOCK).astype(jnp.bfloat16)
    qt = _to_tiles(q, HEAD_DIM ** -0.5)

    grid = (b, NSA_KV_HEADS, nq)
    params = pltpu.CompilerParams(dimension_semantics=("arbitrary",) * 3, vmem_limit_bytes=VMEM_LIMIT)
    q_spec = pl.BlockSpec((1, 1, 1, QROWS, HEAD_DIM), lambda bi, h, qi: (bi, h, qi, 0, 0))
    o_spec = pl.BlockSpec((1, 1, 1, QROWS, HEAD_DIM), lambda bi, h, qi: (bi, h, qi, 0, 0))
    o_shape = jax.ShapeDtypeStruct((b, NSA_KV_HEADS, nq, QROWS, HEAD_DIM), jnp.float32)
    seq_spec = lambda rows, width: pl.BlockSpec((1, 1, rows, width), lambda bi, h, qi: (bi, h, 0, 0))
    bias_spec = pl.BlockSpec((1, 1, Q_BLOCK, nsel), lambda bi, h, qi: (bi, h, qi, 0))

    o_c, bias = pl.pallas_call(
        _cmp_select_body, grid=grid,
        in_specs=[q_spec, seq_spec(ncp, HEAD_DIM), seq_spec(ncp, HEAD_DIM),
                  pl.BlockSpec((ncp, nsel), lambda bi, h, qi: (0, 0))],
        out_specs=[o_spec, bias_spec],
        out_shape=[o_shape, jax.ShapeDtypeStruct((b, NSA_KV_HEADS, s, nsel), jnp.bfloat16)],
        compiler_params=params, name="nsa_cmp_select",
    )(qt, kc, vc, ov)

    k_s, v_s = _head_major(kv_slc[:, :, 0]), _head_major(kv_slc[:, :, 1])
    onehot = (jnp.arange(s)[:, None] // SEL_BLOCK == jnp.arange(nsel)[None, :]).astype(jnp.bfloat16)
    kaug = jnp.concatenate([jnp.broadcast_to(onehot, (b, NSA_KV_HEADS, s, nsel)), k_s.astype(jnp.bfloat16)], -1)
    o_s = pl.pallas_call(
        _slc_body, grid=grid,
        in_specs=[q_spec, bias_spec, seq_spec(s, nsel + HEAD_DIM), seq_spec(s, V_AUG)],
        out_specs=o_spec, out_shape=o_shape,
        scratch_shapes=[pltpu.VMEM((QROWS, nsel + HEAD_DIM), jnp.bfloat16),
                        pltpu.VMEM((QROWS, LANE), jnp.float32),
                        pltpu.VMEM((QROWS, V_AUG), jnp.float32)],
        compiler_params=params, name="nsa_slc",
    )(qt, bias, kaug, _with_ones(v_s))

    front = ((0, 0), (0, 0), (WINDOW, 0), (0, 0))
    k_w = jnp.pad(_head_major(kv_win[:, :, 0]).astype(jnp.bfloat16), front)
    v_w = jnp.pad(_with_ones(_head_major(kv_win[:, :, 1])), front)
    o_w = pl.pallas_call(
        _win_body, grid=grid,
        in_specs=[q_spec, seq_spec(s + WINDOW, HEAD_DIM), seq_spec(s + WINDOW, V_AUG)],
        out_specs=o_spec, out_shape=o_shape,
        compiler_params=params, name="nsa_win",
    )(qt, k_w, v_w)

    o = (_from_tiles(o_c) * gates[..., 0:1] + _from_tiles(o_s) * gates[..., 1:2]
         + _from_tiles(o_w) * gates[..., 2:3]).reshape(b, s, NSA_WIDTH)
    return o, (kv_cmp, kv_slc, kv_win[:, s - min(WINDOW, s):])


RWKV_PAIRS = RWKV_HEADS // 2
RWKV_TC = 256
RWKV_NB = 4
RWKV_GROUP = 8
V_PIECES = 4
V_STEPS = LANE // (V_PIECES * RWKV_HEADS)


def _rwkv_body(nkk_ref, w_ref, bb_ref, k_ref, r_ref, vt_ref, e2_ref, ebd_ref, s0_ref, o_ref, sT_ref, s_sc):
    nb, tc = nkk_ref.shape[0], nkk_ref.shape[1]
    ci = pl.program_id(1)

    @pl.when(ci == 0)
    def _():
        s_sc[...] = s0_ref[...]

    lane = lax.broadcasted_iota(jnp.int32, (1, LANE), 1)
    left = lane < HEAD_DIM
    vlane = lax.broadcasted_iota(jnp.int32, (HEAD_DIM, LANE), 1)
    vstep = (vlane >> (RWKV_HEADS.bit_length() - 1)) & (V_STEPS - 1)

    def one_step(b, t0, i, rows):
        nkk, w, bb, k, r = rows
        t = t0 + i
        row = lambda u, p: u[i:i + 1, p * LANE:(p + 1) * LANE]
        vm = jnp.where(vstep == i % V_STEPS, vt_ref[b, t0 // V_STEPS + i // V_STEPS], jnp.zeros((), jnp.bfloat16))
        vb = jnp.dot(vm, e2_ref[...], preferred_element_type=jnp.float32)
        rbd = (ebd_ref[...] * r[i:i + 1, :]).astype(jnp.bfloat16)
        o = jnp.zeros((RWKV_HEADS, HEAD_DIM), jnp.float32)
        for p in range(RWKV_PAIRS):
            s = s_sc[b, p]
            c = row(nkk, p)
            c_l = jnp.where(left, c, 0.0)
            c_r = jnp.where(left, 0.0, c)
            sa = jnp.where(left, jnp.sum(s * c_l, -1, keepdims=True), jnp.sum(s * c_r, -1, keepdims=True))
            s = s * row(w, p) + sa * row(bb, p) + vb[:, p * LANE:(p + 1) * LANE] * row(k, p)
            s_sc[b, p] = s
            o = o + lax.dot_general(rbd[:, p * LANE:(p + 1) * LANE], s.astype(jnp.bfloat16), _NT,
                                    preferred_element_type=jnp.float32)
        o_ref[b, pl.ds(pl.multiple_of(t * RWKV_HEADS, RWKV_HEADS), RWKV_HEADS), :] = o

    group = min(tc, RWKV_GROUP)

    def group_steps(gi, carry):
        t0 = pl.multiple_of(gi * group, group)
        rows = [tuple(ref[b, pl.ds(t0, group), :] for ref in (nkk_ref, w_ref, bb_ref, k_ref, r_ref)) for b in range(nb)]
        for i in range(group):
            for b in range(nb):
                one_step(b, t0, i, rows[b])
        return carry

    lax.fori_loop(0, tc // group, group_steps, 0)
    sT_ref[...] = s_sc[...]


def rwkv_scan(r, k, v, kk, a, decay, wkv0):
    b, l = r.shape[:2]
    tc = min(l, RWKV_TC)
    nb = RWKV_NB
    assert l % tc == 0 and tc % V_STEPS == 0 and b % nb == 0
    vh = v.reshape(b, l, RWKV_HEADS, HEAD_DIM)
    v1 = vh.astype(jnp.bfloat16)
    res = vh - v1.astype(jnp.float32)
    v2 = res.astype(jnp.bfloat16)
    v3 = (res - v2.astype(jnp.float32)).astype(jnp.bfloat16)
    pieces = jnp.stack([v1, v2, v3, jnp.zeros_like(v1)], 0)
    pieces = pieces.reshape(V_PIECES, b, l // V_STEPS, V_STEPS, RWKV_HEADS, HEAD_DIM)
    vt = jnp.transpose(pieces, (1, 2, 5, 0, 3, 4)).reshape(b, l // V_STEPS, HEAD_DIM, LANE)
    lane_head = jnp.arange(LANE) % RWKV_HEADS
    col_head = jnp.arange(RWKV_WIDTH) // HEAD_DIM
    e2 = (lane_head[:, None] == col_head[None, :]).astype(jnp.bfloat16)
    ebd = (jnp.arange(RWKV_HEADS)[:, None] == col_head[None, :]).astype(jnp.float32)
    to_pairs = lambda s: jnp.transpose(s.reshape(b, RWKV_PAIRS, 2, HEAD_DIM, HEAD_DIM), (0, 1, 3, 2, 4)
                                       ).reshape(b, RWKV_PAIRS, HEAD_DIM, LANE)
    s0 = to_pairs(wkv0.astype(jnp.float32))

    row_spec = pl.BlockSpec((nb, tc, RWKV_WIDTH), lambda bi, ci: (bi, ci, 0))
    state_spec = pl.BlockSpec((nb, RWKV_PAIRS, HEAD_DIM, LANE), lambda bi, ci: (bi, 0, 0, 0))
    o, s_t = pl.pallas_call(
        _rwkv_body, grid=(b // nb, l // tc),
        in_specs=[row_spec] * 5 + [
            pl.BlockSpec((nb, tc // V_STEPS, HEAD_DIM, LANE), lambda bi, ci: (bi, ci, 0, 0)),
            pl.BlockSpec((LANE, RWKV_WIDTH), lambda bi, ci: (0, 0)),
            pl.BlockSpec((RWKV_HEADS, RWKV_WIDTH), lambda bi, ci: (0, 0)),
            state_spec],
        out_specs=[pl.BlockSpec((nb, tc * RWKV_HEADS, HEAD_DIM), lambda bi, ci: (bi, ci, 0)), state_spec],
        out_shape=[jax.ShapeDtypeStruct((b, l * RWKV_HEADS, HEAD_DIM), jnp.float32),
                   jax.ShapeDtypeStruct((b, RWKV_PAIRS, HEAD_DIM, LANE), jnp.float32)],
        scratch_shapes=[pltpu.VMEM((nb, RWKV_PAIRS, HEAD_DIM, LANE), jnp.float32)],
        compiler_params=pltpu.CompilerParams(dimension_semantics=("arbitrary", "arbitrary"),
                                             vmem_limit_bytes=VMEM_LIMIT),
        name="rwkv_scan",
    )(-kk, decay, kk * a, k, r, vt, e2, ebd, s0)
    s_t = jnp.transpose(s_t.reshape(b, RWKV_PAIRS, HEAD_DIM, 2, HEAD_DIM), (0, 1, 3, 2, 4))
    return o.reshape(b, l, RWKV_HEADS, HEAD_DIM), s_t.reshape(b, RWKV_HEADS, HEAD_DIM, HEAD_DIM)


MOE_ROWS = 512


def _expert_body(blk_e_ref, n_used_ref, x_ref, wgu_ref, bgu_ref, wd_ref, bd_ref, o_ref):
    i = pl.program_id(0)

    @pl.when(i < n_used_ref[0])
    def _():
        hgu = jnp.dot(x_ref[...], wgu_ref[0], preferred_element_type=jnp.float32) + bgu_ref[0]
        gate = jnp.minimum(hgu[:, :D_FF], SWIGLU_LIMIT)
        up = jnp.clip(hgu[:, D_FF:], -SWIGLU_LIMIT, SWIGLU_LIMIT)
        act = (up + 1.0) * gate * jax.nn.sigmoid(SWIGLU_ALPHA * gate)
        o_ref[...] = jnp.dot(act.astype(jnp.bfloat16), wd_ref[0], preferred_element_type=jnp.float32) + bd_ref[0]

    @pl.when(i >= n_used_ref[0])
    def _():
        o_ref[...] = jnp.zeros(o_ref.shape, o_ref.dtype)


def expert_blocks(xb, blk_e, n_used, w_gate_up, b_gate_up, w_down, b_down):
    n_rows, d = xb.shape
    n_blocks = n_rows // MOE_ROWS
    wgu = w_gate_up.astype(jnp.bfloat16)
    wd = w_down.astype(jnp.bfloat16)
    grid_spec = pltpu.PrefetchScalarGridSpec(
        num_scalar_prefetch=2, grid=(n_blocks,),
        in_specs=[pl.BlockSpec((MOE_ROWS, d), lambda i, e, n: (i, 0)),
                  pl.BlockSpec((1, d, 2 * D_FF), lambda i, e, n: (e[i], 0, 0)),
                  pl.BlockSpec((1, 1, 2 * D_FF), lambda i, e, n: (e[i], 0, 0)),
                  pl.BlockSpec((1, D_FF, d), lambda i, e, n: (e[i], 0, 0)),
                  pl.BlockSpec((1, 1, d), lambda i, e, n: (e[i], 0, 0))],
        out_specs=pl.BlockSpec((MOE_ROWS, d), lambda i, e, n: (i, 0)))
    return pl.pallas_call(
        _expert_body, grid_spec=grid_spec,
        out_shape=jax.ShapeDtypeStruct((n_rows, d), jnp.float32),
        compiler_params=pltpu.CompilerParams(dimension_semantics=("arbitrary",), vmem_limit_bytes=VMEM_LIMIT),
        name="moe_experts",
    )(blk_e, n_used, xb, wgu, b_gate_up.reshape(N_EXPERTS, 1, 2 * D_FF), wd, b_down.reshape(N_EXPERTS, 1, d))


def rms_norm(x, g):
    xf = x.astype(jnp.float32)
    y = xf * lax.rsqrt(jnp.mean(xf * xf, -1, keepdims=True) + RMS_EPS)
    return (y * g.astype(jnp.float32)).astype(x.dtype)


def masked_softmax(s, mask):
    s = jnp.where(mask, s.astype(jnp.float32), -jnp.inf)
    m = jnp.max(s, -1, keepdims=True)
    m = jnp.where(jnp.isfinite(m), m, 0.0)
    e = jnp.where(mask, jnp.exp(s - m), 0.0)
    return e / jnp.maximum(jnp.sum(e, -1, keepdims=True), 1e-30)


def split_nsa(z):
    b, l = z.shape[:2]
    q = z[..., :NSA_WIDTH].reshape(b, l, NSA_KV_HEADS, NSA_GROUP, HEAD_DIM)
    kv = z[..., NSA_WIDTH:NSA_WIDTH + 6 * KV_WIDTH].reshape(b, l, 3, 2, NSA_KV_HEADS, HEAD_DIM)
    gates = jax.nn.sigmoid(z[..., NSA_WIDTH + 6 * KV_WIDTH:]).reshape(b, l, NSA_KV_HEADS, NSA_GROUP, 3)
    return q, kv[:, :, 0], kv[:, :, 1], kv[:, :, 2], gates


CMP_PAGES = 32
PAGE_CHUNKS = PAGE_SIZE // CMP_STRIDE
KV_ROW = 2 * KV_WIDTH


def _compress_body(x_ref, pos_ref, wlo_ref, whi_ref, lo_ref, hi_ref):
    cp = x_ref.shape[0]
    for c in range(2):
        lo = jnp.zeros((cp * PAGE_CHUNKS, KV_WIDTH), jnp.float32)
        hi = jnp.zeros((cp * PAGE_CHUNKS, KV_WIDTH), jnp.float32)
        for l in range(CMP_STRIDE):
            xl = x_ref[:, pl.ds(2 * l + c, PAGE_CHUNKS, stride=2 * CMP_STRIDE), :]
            xl = xl.reshape(cp * PAGE_CHUNKS, KV_WIDTH)
            lo = lo + jnp.dot((xl + pos_ref[c, l:l + 1]).astype(jnp.bfloat16), wlo_ref[c, l],
                              preferred_element_type=jnp.float32)
            hi = hi + jnp.dot((xl + pos_ref[c, CMP_STRIDE + l:CMP_STRIDE + l + 1]).astype(jnp.bfloat16), whi_ref[c, l],
                              preferred_element_type=jnp.float32)
        lo_ref[:, :, c * KV_WIDTH:(c + 1) * KV_WIDTH] = lo.reshape(cp, PAGE_CHUNKS, KV_WIDTH)
        hi_ref[:, :, c * KV_WIDTH:(c + 1) * KV_WIDTH] = hi.reshape(cp, PAGE_CHUNKS, KV_WIDTH)


def compress_pages(pages, cmp_pos, w_cmp):
    p = pages.shape[0]
    assert p % CMP_PAGES == 0
    pos = jnp.broadcast_to(jnp.transpose(cmp_pos, (1, 0, 2))[:, :, None, :],
                           (2, CMP_BLOCK, NSA_KV_HEADS, HEAD_DIM)).reshape(2, CMP_BLOCK, KV_WIDTH)
    eye = jnp.eye(NSA_KV_HEADS, dtype=jnp.float32)
    wbd = jnp.einsum('cldo,hH->clhdHo', w_cmp, eye).reshape(2, CMP_BLOCK, KV_WIDTH, KV_WIDTH).astype(jnp.bfloat16)
    out = jax.ShapeDtypeStruct((p, PAGE_CHUNKS, KV_ROW), jnp.float32)
    spec3 = lambda a, b_: pl.BlockSpec((CMP_PAGES, a, b_), lambda i: (i, 0, 0))
    w_spec = pl.BlockSpec((2, CMP_STRIDE, KV_WIDTH, KV_WIDTH), lambda i: (0, 0, 0, 0))
    return pl.pallas_call(
        _compress_body, grid=(p // CMP_PAGES,),
        in_specs=[spec3(2 * PAGE_SIZE, KV_WIDTH),
                  pl.BlockSpec((2, CMP_BLOCK, KV_WIDTH), lambda i: (0, 0, 0)), w_spec, w_spec],
        out_specs=[spec3(PAGE_CHUNKS, KV_ROW), spec3(PAGE_CHUNKS, KV_ROW)],
        out_shape=[out, out],
        compiler_params=pltpu.CompilerParams(dimension_semantics=("arbitrary",), vmem_limit_bytes=VMEM_LIMIT),
        name="cmp_compress",
    )(pages.reshape(p, 2 * PAGE_SIZE, KV_WIDTH), pos, wbd[:, :CMP_STRIDE], wbd[:, CMP_STRIDE:])


def compress_blocks(kv, cmp_pos, w_cmp):
    b, l = kv.shape[:2]
    lo, hi = compress_pages(kv.reshape(b * l // PAGE_SIZE, PAGE_SIZE, KV_ROW), cmp_pos, w_cmp)
    lo = lo.reshape(b, l // CMP_STRIDE, 2, NSA_KV_HEADS, HEAD_DIM)
    hi = hi.reshape(b, l // CMP_STRIDE, 2, NSA_KV_HEADS, HEAD_DIM)
    return lo[:, :-1] + hi[:, 1:]


def cmp_branch(qg, t, summ):
    n = summ.shape[1]
    end = jnp.arange(n) * CMP_STRIDE + (CMP_BLOCK - 1)
    mask = end[None, :] <= t[:, None]
    s = jnp.einsum('bqhgd,bnhd->bhgqn', qg, summ[:, :, 0]) * (HEAD_DIM ** -0.5)
    p = masked_softmax(s, mask)
    o = jnp.einsum('bhgqn,bnhd->bqhgd', p.astype(qg.dtype), summ[:, :, 1])
    return o, jnp.sum(p, axis=2)


def select_blocks(p_cmp, t, n_sel):
    n_cmp = p_cmp.shape[-1]
    i = jnp.arange(n_cmp)[:, None]
    j = jnp.arange(n_sel)[None, :]
    overlap = jnp.clip(jnp.minimum(i * CMP_STRIDE + CMP_BLOCK, (j + 1) * SEL_BLOCK)
                       - jnp.maximum(i * CMP_STRIDE, j * SEL_BLOCK), 0, None)
    score = jnp.einsum('bhqn,nj->bhqj', p_cmp, overlap.astype(jnp.float32) / CMP_BLOCK)
    tb = (t // SEL_BLOCK)[:, None]
    forced = (j == 0) | (j == tb) | (j == tb - 1)
    score = jnp.where(forced, FORCE_SCORE, score)
    score = jnp.where(j <= tb, score, -1.0)
    top_s, idx = lax.top_k(score, min(SEL_TOPN, n_sel))
    return idx, top_s >= 0.0


def slc_branch(qg, t, kv_sel, idx, ok):
    b, h, q, n = idx.shape
    kpos = idx[..., None] * SEL_BLOCK + jnp.arange(SEL_BLOCK)
    mask = (ok[..., None] & (kpos <= t[None, None, :, None, None])).reshape(b, h, 1, q, n * SEL_BLOCK)
    s = jnp.einsum('bqhgd,bhqnld->bhgqnl', qg, kv_sel[..., 0, :]) * (HEAD_DIM ** -0.5)
    p = masked_softmax(s.reshape(b, h, NSA_GROUP, q, n * SEL_BLOCK), mask)
    v = kv_sel[..., 1, :].reshape(b, h, q, n * SEL_BLOCK, HEAD_DIM)
    return jnp.einsum('bhgqk,bhqkd->bqhgd', p.astype(qg.dtype), v)


def win_branch(qg, t, kv_w, kpos):
    d = t[:, None] - kpos[None, :]
    mask = (d >= 0) & (d < WINDOW) & (kpos[None, :] >= 0)
    s = jnp.einsum('bqhgd,bkhd->bhgqk', qg, kv_w[:, :, 0]) * (HEAD_DIM ** -0.5)
    p = masked_softmax(s, mask)
    return jnp.einsum('bhgqk,bkhd->bqhgd', p.astype(qg.dtype), kv_w[:, :, 1])


def gate_merge(o_c, o_s, o_w, g):
    o = o_c * g[..., 0:1] + o_s * g[..., 1:2] + o_w * g[..., 2:3]
    return o.reshape(o.shape[0], o.shape[1], NSA_WIDTH)


def nsa_sample(z, cache_cmp, cache_slc, page_table, state_win, cmp_pos, w_cmp):
    q, kv_cmp, kv_slc, kv_win, gates = split_nsa(z)
    b, n = z.shape[:2]
    n_pages = page_table.shape[1]
    past = n_pages * PAGE_SIZE
    total = past + n
    t = past + jnp.arange(n)
    b_ix = jnp.arange(b)[:, None, None, None]
    h_ix = jnp.arange(NSA_KV_HEADS)[None, :, None, None]
    assert n <= CMP_STRIDE
    lo_pool, hi_pool = compress_pages(cache_cmp.reshape(-1, PAGE_SIZE, KV_ROW), cmp_pos, w_cmp)
    as_chunks = lambda u: u[page_table].reshape(b, past // CMP_STRIDE, 2, NSA_KV_HEADS, HEAD_DIM)
    lo_past, hi_past = as_chunks(lo_pool), as_chunks(hi_pool)
    new_chunk = jnp.pad(kv_cmp, ((0, 0), (0, CMP_STRIDE - n), (0, 0), (0, 0), (0, 0)))
    hi_new = jnp.einsum('blchd,cldo->bcho', new_chunk + cmp_pos[None, CMP_STRIDE:, :, None, :], w_cmp[:, CMP_STRIDE:])
    summ = lo_past + jnp.concatenate([hi_past[:, 1:], hi_new[:, None]], axis=1)
    o_c, p_c = cmp_branch(q, t, summ)
    n_sel = -(-total // SEL_BLOCK)
    idx, ok = select_blocks(p_c, t, n_sel)
    past_blocks = past // SEL_BLOCK
    bpp = PAGE_SIZE // SEL_BLOCK
    phys = page_table[b_ix, jnp.minimum(idx // bpp, n_pages - 1)]
    pool = cache_slc.reshape(cache_slc.shape[0], bpp, SEL_BLOCK, 2, NSA_KV_HEADS, HEAD_DIM)
    from_past = pool[phys, idx % bpp, :, :, h_ix, :]
    n_new_blk = -(-n // SEL_BLOCK)
    new_blocks = jnp.pad(kv_slc, ((0, 0), (0, n_new_blk * SEL_BLOCK - n), (0, 0), (0, 0), (0, 0)))
    new_blocks = new_blocks.reshape(b, n_new_blk, SEL_BLOCK, 2, NSA_KV_HEADS, HEAD_DIM)
    from_new = new_blocks[b_ix, jnp.clip(idx - past_blocks, 0, n_new_blk - 1), :, :, h_ix, :]
    kv_sel = jnp.where((idx >= past_blocks)[..., None, None, None], from_new, from_past)
    o_s = slc_branch(q, t, kv_sel, idx, ok)
    wb = state_win.shape[1]
    kv_w = jnp.concatenate([state_win, kv_win], axis=1)
    o_w = win_branch(q, t, kv_w, past - wb + jnp.arange(wb + n))
    o = gate_merge(o_c, o_s, o_w, gates)
    return o, (kv_cmp, kv_slc, kv_w[:, n:])


def rwkv7_time_mix(z, prev_row, wkv0, shift_mu, w0, w2, a0, a2, g2, k_k, k_a, r_k, gn_g, gn_b):
    b, l = z.shape[:2]
    zf = z.astype(jnp.float32)
    prev = jnp.concatenate([prev_row[:, None].astype(jnp.float32), zf[:, :-1]], axis=1)
    zs = zf + shift_mu * (prev - zf)
    r, k, v, wd, ad, gd = jnp.split(zs, RWKV_SPLITS, axis=-1)
    w = -jax.nn.softplus(-(w0 + jnp.tanh(wd) @ w2)) - 0.5
    decay = jnp.exp(-jnp.exp(w))
    a = jax.nn.sigmoid(a0 + ad @ a2)
    g = jax.nn.sigmoid(gd) @ g2
    heads = lambda u: u.reshape(b, l, RWKV_HEADS, HEAD_DIM)
    kk = heads(k * k_k)
    kk = kk / jnp.maximum(jnp.sqrt(jnp.sum(kk * kk, -1, keepdims=True)), 1e-12)
    k = k * (1.0 + (a - 1.0) * k_a)
    o, s_final = rwkv_scan(r, k, v, kk.reshape(b, l, RWKV_WIDTH), a, decay, wkv0)
    r, k, v = heads(r), heads(k), heads(v)
    mu = jnp.mean(o, -1, keepdims=True)
    var = jnp.mean(jnp.square(o - mu), -1, keepdims=True)
    o = ((o - mu) * lax.rsqrt(var + RWKV_GN_EPS)).reshape(b, l, RWKV_WIDTH) * gn_g + gn_b
    bonus = (jnp.sum(r * k * r_k, -1, keepdims=True) * v).reshape(b, l, RWKV_WIDTH)
    out = (o + bonus) * g
    return out.astype(z.dtype), s_final, z[:, -1]


def moe_ffn(x, w_router, b_router, w_gate_up, b_gate_up, w_down, b_down):
    shape = x.shape
    xt = x.reshape(-1, shape[-1])
    n_tok = xt.shape[0]
    logits = (xt @ w_router).astype(jnp.float32) + b_router.astype(jnp.float32)
    top_v, top_e = lax.top_k(logits, TOP_K)
    gates = jax.nn.softmax(top_v, axis=-1)
    flat_e = top_e.reshape(-1)
    n_assign = n_tok * TOP_K
    order = jnp.argsort(flat_e)
    rank = jnp.argsort(order)
    counts = jnp.sum(flat_e[:, None] == jnp.arange(N_EXPERTS)[None, :], axis=0)
    rows = MOE_ROWS
    padded = (counts + rows - 1) // rows * rows
    pad_end = jnp.cumsum(padded)
    pad_start = pad_end - padded
    grp_start = jnp.cumsum(counts) - counts
    n_blocks = -(-n_assign // rows) + N_EXPERTS
    n_rows = n_blocks * rows
    blk_e = jnp.minimum(jnp.sum(pad_end[None, :] <= (jnp.arange(n_blocks) * rows)[:, None], axis=1), N_EXPERTS - 1)
    row_e = jnp.repeat(blk_e, rows)
    within = jnp.arange(n_rows) - pad_start[row_e]
    src = jnp.clip(grp_start[row_e] + within, 0, n_assign - 1)
    row_token = jnp.where(within < counts[row_e], order[src] // TOP_K, n_tok).astype(jnp.int32)
    x_pad = jnp.concatenate([xt.astype(jnp.bfloat16), jnp.zeros((1, shape[-1]), jnp.bfloat16)], axis=0)
    xb = x_pad[row_token]
    n_used = (pad_end[-1] // rows).astype(jnp.int32).reshape(1)
    yb = expert_blocks(xb, blk_e.astype(jnp.int32), n_used, w_gate_up, b_gate_up, w_down, b_down)
    dest = (pad_start[flat_e] + rank - grp_start[flat_e]).reshape(n_tok, TOP_K)
    y = gates[:, 0:1] * yb[dest[:, 0]]
    for j in range(1, TOP_K):
        y = y + gates[:, j:j + 1] * yb[dest[:, j]]
    return y.reshape(shape)


def kernel(x_prompt, x_sample, cache_cmp, cache_slc, page_table, state_win, state_wkv, state_shift,
           norm_mix_g, w_in, cmp_pos, w_cmp, shift_mu, w0, w2, a0, a2, g2, k_k, k_a, r_k, gn_g, gn_b,
           w_out, norm_ffn_g, w_router, b_router, w_gate_up, b_gate_up, w_down, b_down, norm_final_g):
    xp, xs = x_prompt, x_sample
    l = 0
    rw = (shift_mu[l], w0[l], w2[l], a0[l], a2[l], g2[l], k_k[l], k_a[l], r_k[l], gn_g[l], gn_b[l])
    ffn = (w_router[l], b_router[l], w_gate_up[l], b_gate_up[l], w_down[l], b_down[l])
    bp, sp = xp.shape[:2]
    bs, ss = xs.shape[:2]
    zp = norm_proj(xp.reshape(bp * sp, D_MODEL), norm_mix_g[l], w_in[l]).reshape(bp, sp, IN_COLS)
    oa, (kc_p, ksl_p, kw_p) = nsa_prompt(zp[..., :NSA_COLS], cmp_pos[l], w_cmp[l])
    ob, wkv_p, sh_p = rwkv7_time_mix(zp[..., NSA_COLS:], jnp.zeros((bp, RWKV_COLS), zp.dtype),
                                     jnp.zeros((bp, RWKV_HEADS, HEAD_DIM, HEAD_DIM), jnp.float32), *rw)
    xp = xp + jnp.concatenate([oa, ob], axis=-1) @ w_out[l]

    zs = norm_proj(xs.reshape(bs * ss, D_MODEL), norm_mix_g[l], w_in[l]).reshape(bs, ss, IN_COLS)
    oa, (kc_s, ksl_s, kw_s) = nsa_sample(zs[..., :NSA_COLS], cache_cmp[l], cache_slc[l], page_table, state_win[l],
                                         cmp_pos[l], w_cmp[l])
    ob, wkv_s, sh_s = rwkv7_time_mix(zs[..., NSA_COLS:], state_shift[l], state_wkv[l], *rw)
    xs = xs + jnp.concatenate([oa, ob], axis=-1) @ w_out[l]
    x_all = jnp.concatenate([xp.reshape(bp * sp, D_MODEL), xs.reshape(bs * ss, D_MODEL)], axis=0)
    m_all = moe_ffn(rms_norm(x_all, norm_ffn_g[l]), *ffn)
    xp = xp + m_all[:bp * sp].reshape(xp.shape)
    xs = xs + m_all[bp * sp:].reshape(xs.shape)

    y_prompt = rms_norm(xp, norm_final_g)
    y_sample = rms_norm(xs, norm_final_g)
    st = lambda u: u[None]
    return (y_prompt, y_sample, st(kc_p), st(ksl_p), st(kw_p), st(wkv_p), st(sh_p),
            st(kc_s), st(ksl_s), st(kw_s), st(wkv_s), st(sh_s))
```

```python
import functools
import math

import jax
import jax.numpy as jnp
from jax import lax
from jax.experimental import pallas as pl
from jax.experimental.pallas import tpu as pltpu

D_MODEL = 1024
DEPTH = 1
PAGE_SIZE = 128

HEAD_DIM = 64
NSA_HEADS = 8
NSA_KV_HEADS = 2
NSA_GROUP = NSA_HEADS // NSA_KV_HEADS
CMP_BLOCK = 32
CMP_STRIDE = 16
SEL_BLOCK = 64
SEL_TOPN = 16
WINDOW = 512
Q_BLOCK = 128
FORCE_SCORE = 1e6
RWKV_HEADS = 8
RWKV_DECAY_LORA = 64
RWKV_A_LORA = 64
RWKV_GATE_LORA = 128
RWKV_GN_EPS = 64e-5
N_EXPERTS = 32
TOP_K = 4
D_FF = 1024
SWIGLU_LIMIT = 7.0
SWIGLU_ALPHA = 1.702
EXPERT_BLOCK = 128
RMS_EPS = 1e-5

NSA_WIDTH = NSA_HEADS * HEAD_DIM
KV_WIDTH = NSA_KV_HEADS * HEAD_DIM
RWKV_WIDTH = RWKV_HEADS * HEAD_DIM
MIX_WIDTH = NSA_WIDTH + RWKV_WIDTH
NSA_COLS = NSA_WIDTH + 6 * KV_WIDTH + 3 * NSA_HEADS
RWKV_COLS = 3 * RWKV_WIDTH + RWKV_DECAY_LORA + RWKV_A_LORA + RWKV_GATE_LORA
IN_COLS = NSA_COLS + RWKV_COLS
RWKV_SPLITS = (RWKV_WIDTH, 2 * RWKV_WIDTH, 3 * RWKV_WIDTH, 3 * RWKV_WIDTH + RWKV_DECAY_LORA,
               3 * RWKV_WIDTH + RWKV_DECAY_LORA + RWKV_A_LORA)

LANE = 128
VMEM_LIMIT = 48 * 1024 * 1024


GATE_COLS = 3 * NSA_HEADS
PROJ_GROUPS = (NSA_WIDTH, 2 * KV_WIDTH, 2 * KV_WIDTH, 2 * KV_WIDTH, RWKV_COLS, LANE)


def _in_proj_body(x_ref, g_ref, w_ref, *o_refs):
    x = x_ref[...]
    y = x * lax.rsqrt(jnp.mean(x * x, -1, keepdims=True) + RMS_EPS) * g_ref[...]
    z = jnp.dot(y.astype(jnp.bfloat16), w_ref[...], preferred_element_type=jnp.float32)
    c0 = 0
    for o_ref, width in zip(o_refs, PROJ_GROUPS):
        o_ref[...] = z[:, c0:c0 + width]
        c0 += width


def in_proj(x, g, w, block_rows=512):
    b, l, d = x.shape
    n = b * l
    gate0 = NSA_WIDTH + 6 * KV_WIDTH
    wb = jnp.concatenate([w[:, :gate0], w[:, NSA_COLS:], w[:, gate0:NSA_COLS],
                          jnp.zeros((d, LANE - GATE_COLS), w.dtype)], axis=1).astype(jnp.bfloat16)
    cols = sum(PROJ_GROUPS)
    outs = pl.pallas_call(
        _in_proj_body,
        grid=(n // block_rows,),
        in_specs=[pl.BlockSpec((block_rows, d), lambda i: (i, 0)),
                  pl.BlockSpec((1, d), lambda i: (0, 0)),
                  pl.BlockSpec((d, cols), lambda i: (0, 0))],
        out_specs=[pl.BlockSpec((block_rows, width), lambda i: (i, 0)) for width in PROJ_GROUPS],
        out_shape=[jax.ShapeDtypeStruct((n, width), jnp.float32) for width in PROJ_GROUPS],
        compiler_params=pltpu.CompilerParams(dimension_semantics=("arbitrary",), vmem_limit_bytes=VMEM_LIMIT),
        name="in_proj",
    )(x.reshape(n, d), g.reshape(1, d), wb)
    q = outs[0].reshape(b, l, NSA_KV_HEADS, NSA_GROUP, HEAD_DIM)
    kv = [o.reshape(b, l, 2, NSA_KV_HEADS, HEAD_DIM) for o in outs[1:4]]
    gates = jax.nn.sigmoid(outs[5][:, :GATE_COLS]).reshape(b, l, NSA_KV_HEADS, NSA_GROUP, 3)
    return (q, kv[0], kv[1], kv[2], gates), outs[4].reshape(b, l, RWKV_COLS)


QROWS = NSA_GROUP * Q_BLOCK
SLC_TK = 512
MASK_BIAS = -float(2 ** 30)
NEG_INIT = -3.0e38
V_AUG = 2 * HEAD_DIM

_NT = (((1,), (1,)), ((), ()))


def _cmp_select_body(q_ref, kc_ref, vc_ref, ov_ref, oc_ref, bias_ref):
    qb = pl.program_id(1)
    ncp = kc_ref.shape[2]
    nsel = ov_ref.shape[1]
    row = lax.broadcasted_iota(jnp.int32, (QROWS, ncp), 0)
    t = qb * Q_BLOCK + (row & (Q_BLOCK - 1))
    n = lax.broadcasted_iota(jnp.int32, (QROWS, ncp), 1)
    mask = n * CMP_STRIDE + (CMP_BLOCK - 1) <= t
    scores = []
    for h in range(NSA_KV_HEADS):
        q = q_ref[0, h, 0]
        s = lax.dot_general(q, kc_ref[0, h], _NT, preferred_element_type=jnp.float32)
        s = jnp.where(mask, s, -jnp.inf)
        m = jnp.max(s, -1, keepdims=True)
        m = jnp.where(m == -jnp.inf, 0.0, m)
        e = jnp.exp(s - m)
        p = e / jnp.maximum(jnp.sum(e, -1, keepdims=True), 1e-30)
        oc_ref[0, h, 0] = jnp.dot(p.astype(jnp.bfloat16), vc_ref[0, h], preferred_element_type=jnp.float32)
        pc = p[0:Q_BLOCK]
        for g in range(1, NSA_GROUP):
            pc = pc + p[g * Q_BLOCK:(g + 1) * Q_BLOCK]
        scores.append(jnp.dot(pc.astype(jnp.bfloat16), ov_ref[...], preferred_element_type=jnp.float32))
    rows = NSA_KV_HEADS * Q_BLOCK
    score = jnp.concatenate(scores, axis=0)
    j = lax.broadcasted_iota(jnp.int32, (rows, nsel), 1)
    tq = qb * Q_BLOCK + (lax.broadcasted_iota(jnp.int32, (rows, nsel), 0) & (Q_BLOCK - 1))
    tb = tq >> (SEL_BLOCK.bit_length() - 1)
    forced = (j == 0) | (j == tb) | (j == tb - 1)
    score = jnp.where(forced, FORCE_SCORE, score)
    score = jnp.where(j <= tb, score, -1.0)
    jf = j.astype(jnp.float32)
    sel = jnp.zeros((rows, nsel), jnp.bool_)
    for _ in range(min(SEL_TOPN, nsel)):
        top = jnp.max(score, -1, keepdims=True)
        first = jnp.min(jnp.where(score == top, jf, float(nsel)), -1, keepdims=True)
        pick = jf == first
        sel = sel | pick
        score = jnp.where(pick, -2.0, score)
    bias = jnp.where(sel, 0.0, MASK_BIAS).astype(jnp.bfloat16)
    for h in range(NSA_KV_HEADS):
        bias_ref[0, h] = bias[h * Q_BLOCK:(h + 1) * Q_BLOCK]


def _slc_body(q_ref, bias_ref, kaug_ref, vaug_ref, o_ref, qaug_sc, m_sc, acc_sc):
    qb = pl.program_id(2)
    start = qb * Q_BLOCK
    nsel = bias_ref.shape[3]
    bias = bias_ref[0, 0]
    for g in range(NSA_GROUP):
        qaug_sc[g * Q_BLOCK:(g + 1) * Q_BLOCK, 0:nsel] = bias
    qaug_sc[:, nsel:nsel + HEAD_DIM] = q_ref[0, 0, 0]
    m_sc[...] = jnp.full(m_sc.shape, NEG_INIT, jnp.float32)
    acc_sc[...] = jnp.zeros(acc_sc.shape, jnp.float32)

    def scores(kt):
        k0 = pl.multiple_of(kt * SLC_TK, SLC_TK)
        return lax.dot_general(qaug_sc[...], kaug_ref[0, 0, pl.ds(k0, SLC_TK), :], _NT,
                               preferred_element_type=jnp.float32)

    def absorb(kt, s, causal):
        k0 = pl.multiple_of(kt * SLC_TK, SLC_TK)
        slabs = [s[:, c * LANE:(c + 1) * LANE] for c in range(SLC_TK // LANE)]
        if causal:
            row = lax.broadcasted_iota(jnp.int32, (QROWS, LANE), 0)
            col = lax.broadcasted_iota(jnp.int32, (QROWS, LANE), 1)
            t_rel = start + (row & (Q_BLOCK - 1)) - k0 - col
            slabs = [jnp.where(t_rel >= c * LANE, sl, -jnp.inf) for c, sl in enumerate(slabs)]
        top = slabs[0]
        for sl in slabs[1:]:
            top = jnp.maximum(top, sl)
        m_prev = m_sc[...]
        m_new = jnp.maximum(m_prev, jnp.max(top, -1, keepdims=True))
        p = jnp.concatenate([jnp.exp(sl - m_new) for sl in slabs], axis=1).astype(jnp.bfloat16)
        acc_sc[...] = jnp.exp(m_prev - m_new) * acc_sc[...] + jnp.dot(
            p, vaug_ref[0, 0, pl.ds(k0, SLC_TK), :], preferred_element_type=jnp.float32)
        m_sc[...] = m_new

    n_full = start // SLC_TK

    def body(kt, s_cur):
        s_next = scores(kt + 1)
        absorb(kt, s_cur, False)
        return s_next

    s_last = lax.fori_loop(0, n_full, body, scores(0))
    absorb(n_full, s_last, True)
    acc = acc_sc[...]
    o_ref[0, 0, 0] = acc[:, :HEAD_DIM] / jnp.maximum(acc[:, HEAD_DIM:HEAD_DIM + 1], 1e-30)


def _win_body(q_ref, kpad_ref, vpad_ref, o_ref):
    qb = pl.program_id(2)
    start = pl.multiple_of(qb * Q_BLOCK, Q_BLOCK)
    nk = Q_BLOCK + WINDOW
    k = kpad_ref[0, 0, pl.ds(start, nk), :]
    s = lax.dot_general(q_ref[0, 0, 0], k, _NT, preferred_element_type=jnp.float32)
    r = lax.broadcasted_iota(jnp.int32, (QROWS, nk), 0) & (Q_BLOCK - 1)
    c = lax.broadcasted_iota(jnp.int32, (QROWS, nk), 1)
    d = r + WINDOW - c
    mask = (d >= 0) & (d < WINDOW) & (c >= WINDOW - start)
    s = jnp.where(mask, s, -jnp.inf)
    m = jnp.max(s, -1, keepdims=True)
    m = jnp.where(m == -jnp.inf, 0.0, m)
    p = jnp.exp(s - m)
    pv = jnp.dot(p.astype(jnp.bfloat16), vpad_ref[0, 0, pl.ds(start, nk), :], preferred_element_type=jnp.float32)
    o_ref[0, 0, 0] = pv[:, :HEAD_DIM] / jnp.maximum(pv[:, HEAD_DIM:HEAD_DIM + 1], 1e-30)


def _to_tiles(u, scale=None):
    b, s = u.shape[:2]
    u = u.reshape(b, s // Q_BLOCK, Q_BLOCK, NSA_KV_HEADS, NSA_GROUP, HEAD_DIM)
    u = jnp.transpose(u, (0, 3, 1, 4, 2, 5)).reshape(b, NSA_KV_HEADS, s // Q_BLOCK, QROWS, HEAD_DIM)
    return u if scale is None else (u * scale).astype(jnp.bfloat16)


def _from_tiles(o):
    b, _, nq, _, x = o.shape
    o = o.reshape(b, NSA_KV_HEADS, nq, NSA_GROUP, Q_BLOCK, x)
    return jnp.transpose(o, (0, 2, 4, 1, 3, 5)).reshape(b, nq * Q_BLOCK, NSA_KV_HEADS, NSA_GROUP, x)


def _head_major(kv):
    return jnp.transpose(kv, (0, 2, 1, 3))


def _with_ones(v):
    ones = jnp.ones(v.shape[:-1] + (1,), v.dtype)
    zeros = jnp.zeros(v.shape[:-1] + (V_AUG - HEAD_DIM - 1,), v.dtype)
    return jnp.concatenate([v, ones, zeros], -1).astype(jnp.bfloat16)


def nsa_prompt(parts, cmp_pos, w_cmp):
    q, kv_cmp, kv_slc, kv_win, gates = parts
    b, s = q.shape[:2]
    nq = s // Q_BLOCK
    nsel = s // SEL_BLOCK
    assert s % SLC_TK == 0 and nsel % LANE == 0
    summ = compress_blocks(kv_cmp, cmp_pos, w_cmp)
    n_cmp = summ.shape[1]
    ncp = -(-n_cmp // LANE) * LANE
    summ = jnp.pad(summ, ((0, 0), (0, ncp - n_cmp), (0, 0), (0, 0), (0, 0))).astype(jnp.bfloat16)
    kc, vc = _head_major(summ[:, :, 0]), _head_major(summ[:, :, 1])
    i = jnp.arange(ncp)[:, None]
    jj = jnp.arange(nsel)[None, :]
    overlap = jnp.clip(jnp.minimum(i * CMP_STRIDE + CMP_BLOCK, (jj + 1) * SEL_BLOCK)
                       - jnp.maximum(i * CMP_STRIDE, jj * SEL_BLOCK), 0, None)
    ov = (overlap.astype(jnp.float32) / CMP_BLOCK).astype(jnp.bfloat16)
    qt = _to_tiles(q, HEAD_DIM ** -0.5)

    grid = (b, NSA_KV_HEADS, nq)
    params = pltpu.CompilerParams(dimension_semantics=("arbitrary",) * 3, vmem_limit_bytes=VMEM_LIMIT)
    q_spec = pl.BlockSpec((1, 1, 1, QROWS, HEAD_DIM), lambda bi, h, qi: (bi, h, qi, 0, 0))
    o_spec = pl.BlockSpec((1, 1, 1, QROWS, HEAD_DIM), lambda bi, h, qi: (bi, h, qi, 0, 0))
    o_shape = jax.ShapeDtypeStruct((b, NSA_KV_HEADS, nq, QROWS, HEAD_DIM), jnp.float32)
    seq_spec = lambda rows, width: pl.BlockSpec((1, 1, rows, width), lambda bi, h, qi: (bi, h, 0, 0))
    bias_spec = pl.BlockSpec((1, 1, Q_BLOCK, nsel), lambda bi, h, qi: (bi, h, qi, 0))

    kvh = NSA_KV_HEADS
    both_q = pl.BlockSpec((1, kvh, 1, QROWS, HEAD_DIM), lambda bi, qi: (bi, 0, qi, 0, 0))
    both_seq = pl.BlockSpec((1, kvh, ncp, HEAD_DIM), lambda bi, qi: (bi, 0, 0, 0))
    o_c, bias = pl.pallas_call(
        _cmp_select_body, grid=(b, nq),
        in_specs=[both_q, both_seq, both_seq, pl.BlockSpec((ncp, nsel), lambda bi, qi: (0, 0))],
        out_specs=[both_q, pl.BlockSpec((1, kvh, Q_BLOCK, nsel), lambda bi, qi: (bi, 0, qi, 0))],
        out_shape=[o_shape, jax.ShapeDtypeStruct((b, kvh, s, nsel), jnp.bfloat16)],
        compiler_params=pltpu.CompilerParams(dimension_semantics=("arbitrary",) * 2, vmem_limit_bytes=VMEM_LIMIT),
        name="nsa_cmp_select",
    )(qt, kc, vc, ov)

    k_s, v_s = _head_major(kv_slc[:, :, 0]), _head_major(kv_slc[:, :, 1])
    onehot = (jnp.arange(s)[:, None] // SEL_BLOCK == jnp.arange(nsel)[None, :]).astype(jnp.bfloat16)
    kaug = jnp.concatenate([jnp.broadcast_to(onehot, (b, NSA_KV_HEADS, s, nsel)), k_s.astype(jnp.bfloat16)], -1)
    o_s = pl.pallas_call(
        _slc_body, grid=grid,
        in_specs=[q_spec, bias_spec, seq_spec(s, nsel + HEAD_DIM), seq_spec(s, V_AUG)],
        out_specs=o_spec, out_shape=o_shape,
        scratch_shapes=[pltpu.VMEM((QROWS, nsel + HEAD_DIM), jnp.bfloat16),
                        pltpu.VMEM((QROWS, LANE), jnp.float32),
                        pltpu.VMEM((QROWS, V_AUG), jnp.float32)],
        compiler_params=params, name="nsa_slc",
    )(qt, bias, kaug, _with_ones(v_s))

    front = ((0, 0), (0, 0), (WINDOW, 0), (0, 0))
    k_w = jnp.pad(_head_major(kv_win[:, :, 0]).astype(jnp.bfloat16), front)
    v_w = jnp.pad(_with_ones(_head_major(kv_win[:, :, 1])), front)
    o_w = pl.pallas_call(
        _win_body, grid=grid,
        in_specs=[q_spec, seq_spec(s + WINDOW, HEAD_DIM), seq_spec(s + WINDOW, V_AUG)],
        out_specs=o_spec, out_shape=o_shape,
        compiler_params=params, name="nsa_win",
    )(qt, k_w, v_w)

    o = (_from_tiles(o_c) * gates[..., 0:1] + _from_tiles(o_s) * gates[..., 1:2]
         + _from_tiles(o_w) * gates[..., 2:3]).reshape(b, s, NSA_WIDTH)
    return o, (kv_cmp, kv_slc, kv_win[:, s - min(WINDOW, s):])


RWKV_PAIRS = RWKV_HEADS // 2
RWKV_TC = 256
RWKV_NB = 4
RWKV_GROUP = 8
V_PIECES = 4
V_STEPS = LANE // (V_PIECES * RWKV_HEADS)


def _rwkv_body(nkk_ref, w_ref, bb_ref, k_ref, r_ref, vt_ref, e2_ref, ebd_ref, s0_ref, o_ref, sT_ref, s_sc):
    nb, tc = nkk_ref.shape[0], nkk_ref.shape[1]
    ci = pl.program_id(1)

    @pl.when(ci == 0)
    def _():
        s_sc[...] = s0_ref[...]

    lane = lax.broadcasted_iota(jnp.int32, (1, LANE), 1)
    left = lane < HEAD_DIM
    vlane = lax.broadcasted_iota(jnp.int32, (HEAD_DIM, LANE), 1)
    vstep = (vlane >> (RWKV_HEADS.bit_length() - 1)) & (V_STEPS - 1)

    def one_step(b, t0, i, rows):
        nkk, w, bb, k, r = rows
        t = t0 + i
        row = lambda u, p: u[i:i + 1, p * LANE:(p + 1) * LANE]
        vm = jnp.where(vstep == i % V_STEPS, vt_ref[b, t0 // V_STEPS + i // V_STEPS], jnp.zeros((), jnp.bfloat16))
        vb = jnp.dot(vm, e2_ref[...], preferred_element_type=jnp.float32)
        rbd = (ebd_ref[...] * r[i:i + 1, :]).astype(jnp.bfloat16)
        o = jnp.zeros((RWKV_HEADS, HEAD_DIM), jnp.float32)
        for p in range(RWKV_PAIRS):
            s = s_sc[b, p]
            c = row(nkk, p)
            c_l = jnp.where(left, c, 0.0)
            c_r = jnp.where(left, 0.0, c)
            sa = jnp.where(left, jnp.sum(s * c_l, -1, keepdims=True), jnp.sum(s * c_r, -1, keepdims=True))
            s = s * row(w, p) + sa * row(bb, p) + vb[:, p * LANE:(p + 1) * LANE] * row(k, p)
            s_sc[b, p] = s
            o = o + lax.dot_general(rbd[:, p * LANE:(p + 1) * LANE], s.astype(jnp.bfloat16), _NT,
                                    preferred_element_type=jnp.float32)
        o_ref[b, pl.ds(pl.multiple_of(t * RWKV_HEADS, RWKV_HEADS), RWKV_HEADS), :] = o

    group = min(tc, RWKV_GROUP)

    def group_steps(gi, carry):
        t0 = pl.multiple_of(gi * group, group)
        rows = [tuple(ref[b, pl.ds(t0, group), :] for ref in (nkk_ref, w_ref, bb_ref, k_ref, r_ref)) for b in range(nb)]
        for i in range(group):
            for b in range(nb):
                one_step(b, t0, i, rows[b])
        return carry

    lax.fori_loop(0, tc // group, group_steps, 0)
    sT_ref[...] = s_sc[...]


def rwkv_scan(r, k, v, kk, a, decay, wkv0):
    b, l = r.shape[:2]
    tc = min(l, RWKV_TC)
    nb = RWKV_NB
    assert l % tc == 0 and tc % V_STEPS == 0 and b % nb == 0
    vh = v.reshape(b, l, RWKV_HEADS, HEAD_DIM)
    v1 = vh.astype(jnp.bfloat16)
    res = vh - v1.astype(jnp.float32)
    v2 = res.astype(jnp.bfloat16)
    v3 = (res - v2.astype(jnp.float32)).astype(jnp.bfloat16)
    pieces = jnp.stack([v1, v2, v3, jnp.zeros_like(v1)], 0)
    pieces = pieces.reshape(V_PIECES, b, l // V_STEPS, V_STEPS, RWKV_HEADS, HEAD_DIM)
    vt = jnp.transpose(pieces, (1, 2, 5, 0, 3, 4)).reshape(b, l // V_STEPS, HEAD_DIM, LANE)
    lane_head = jnp.arange(LANE) % RWKV_HEADS
    col_head = jnp.arange(RWKV_WIDTH) // HEAD_DIM
    e2 = (lane_head[:, None] == col_head[None, :]).astype(jnp.bfloat16)
    ebd = (jnp.arange(RWKV_HEADS)[:, None] == col_head[None, :]).astype(jnp.float32)
    to_pairs = lambda s: jnp.transpose(s.reshape(b, RWKV_PAIRS, 2, HEAD_DIM, HEAD_DIM), (0, 1, 3, 2, 4)
                                       ).reshape(b, RWKV_PAIRS, HEAD_DIM, LANE)
    s0 = to_pairs(wkv0.astype(jnp.float32))

    row_spec = pl.BlockSpec((nb, tc, RWKV_WIDTH), lambda bi, ci: (bi, ci, 0))
    state_spec = pl.BlockSpec((nb, RWKV_PAIRS, HEAD_DIM, LANE), lambda bi, ci: (bi, 0, 0, 0))
    o, s_t = pl.pallas_call(
        _rwkv_body, grid=(b // nb, l // tc),
        in_specs=[row_spec] * 5 + [
            pl.BlockSpec((nb, tc // V_STEPS, HEAD_DIM, LANE), lambda bi, ci: (bi, ci, 0, 0)),
            pl.BlockSpec((LANE, RWKV_WIDTH), lambda bi, ci: (0, 0)),
            pl.BlockSpec((RWKV_HEADS, RWKV_WIDTH), lambda bi, ci: (0, 0)),
            state_spec],
        out_specs=[pl.BlockSpec((nb, tc * RWKV_HEADS, HEAD_DIM), lambda bi, ci: (bi, ci, 0)), state_spec],
        out_shape=[jax.ShapeDtypeStruct((b, l * RWKV_HEADS, HEAD_DIM), jnp.float32),
                   jax.ShapeDtypeStruct((b, RWKV_PAIRS, HEAD_DIM, LANE), jnp.float32)],
        scratch_shapes=[pltpu.VMEM((nb, RWKV_PAIRS, HEAD_DIM, LANE), jnp.float32)],
        compiler_params=pltpu.CompilerParams(dimension_semantics=("arbitrary", "arbitrary"),
                                             vmem_limit_bytes=VMEM_LIMIT),
        name="rwkv_scan",
    )(-kk, decay, kk * a, k, r, vt, e2, ebd, s0)
    s_t = jnp.transpose(s_t.reshape(b, RWKV_PAIRS, HEAD_DIM, 2, HEAD_DIM), (0, 1, 3, 2, 4))
    return o.reshape(b, l, RWKV_HEADS, HEAD_DIM), s_t.reshape(b, RWKV_HEADS, HEAD_DIM, HEAD_DIM)


MOE_ROWS = 512


def _expert_body(blk_e_ref, n_used_ref, x_ref, wgu_ref, bgu_ref, wd_ref, bd_ref, o_ref):
    i = pl.program_id(0)

    @pl.when(i < n_used_ref[0])
    def _():
        hgu = jnp.dot(x_ref[...], wgu_ref[0], preferred_element_type=jnp.float32) + bgu_ref[0]
        gate = jnp.minimum(hgu[:, :D_FF], SWIGLU_LIMIT)
        up = jnp.clip(hgu[:, D_FF:], -SWIGLU_LIMIT, SWIGLU_LIMIT)
        act = (up + 1.0) * gate * jax.nn.sigmoid(SWIGLU_ALPHA * gate)
        o_ref[...] = jnp.dot(act.astype(jnp.bfloat16), wd_ref[0], preferred_element_type=jnp.float32) + bd_ref[0]

    @pl.when(i >= n_used_ref[0])
    def _():
        o_ref[...] = jnp.zeros(o_ref.shape, o_ref.dtype)


def expert_blocks(xb, blk_e, n_used, w_gate_up, b_gate_up, w_down, b_down):
    n_rows, d = xb.shape
    n_blocks = n_rows // MOE_ROWS
    wgu = w_gate_up.astype(jnp.bfloat16)
    wd = w_down.astype(jnp.bfloat16)
    grid_spec = pltpu.PrefetchScalarGridSpec(
        num_scalar_prefetch=2, grid=(n_blocks,),
        in_specs=[pl.BlockSpec((MOE_ROWS, d), lambda i, e, n: (i, 0)),
                  pl.BlockSpec((1, d, 2 * D_FF), lambda i, e, n: (e[i], 0, 0)),
                  pl.BlockSpec((1, 1, 2 * D_FF), lambda i, e, n: (e[i], 0, 0)),
                  pl.BlockSpec((1, D_FF, d), lambda i, e, n: (e[i], 0, 0)),
                  pl.BlockSpec((1, 1, d), lambda i, e, n: (e[i], 0, 0))],
        out_specs=pl.BlockSpec((MOE_ROWS, d), lambda i, e, n: (i, 0)))
    return pl.pallas_call(
        _expert_body, grid_spec=grid_spec,
        out_shape=jax.ShapeDtypeStruct((n_rows, d), jnp.float32),
        compiler_params=pltpu.CompilerParams(dimension_semantics=("arbitrary",), vmem_limit_bytes=VMEM_LIMIT),
        name="moe_experts",
    )(blk_e, n_used, xb, wgu, b_gate_up.reshape(N_EXPERTS, 1, 2 * D_FF), wd, b_down.reshape(N_EXPERTS, 1, d))


def rms_norm(x, g):
    xf = x.astype(jnp.float32)
    y = xf * lax.rsqrt(jnp.mean(xf * xf, -1, keepdims=True) + RMS_EPS)
    return (y * g.astype(jnp.float32)).astype(x.dtype)


def masked_softmax(s, mask):
    s = jnp.where(mask, s.astype(jnp.float32), -jnp.inf)
    m = jnp.max(s, -1, keepdims=True)
    m = jnp.where(jnp.isfinite(m), m, 0.0)
    e = jnp.where(mask, jnp.exp(s - m), 0.0)
    return e / jnp.maximum(jnp.sum(e, -1, keepdims=True), 1e-30)


CMP_PAGES = 32
PAGE_CHUNKS = PAGE_SIZE // CMP_STRIDE
KV_ROW = 2 * KV_WIDTH


def _compress_body(x_ref, pos_ref, wlo_ref, whi_ref, lo_ref, hi_ref, xt_sc):
    cp = x_ref.shape[0]

    def to_rows(pi, carry):
        for c in range(2):
            xt_sc[pi, c] = x_ref[pi, c].reshape(KV_WIDTH, PAGE_SIZE).T
        return carry

    lax.fori_loop(0, cp, to_rows, 0, unroll=4)
    for c in range(2):
        lo = jnp.zeros((cp * PAGE_CHUNKS, KV_WIDTH), jnp.float32)
        hi = jnp.zeros((cp * PAGE_CHUNKS, KV_WIDTH), jnp.float32)
        for l in range(CMP_STRIDE):
            xl = xt_sc[:, c, pl.ds(l, PAGE_CHUNKS, stride=CMP_STRIDE), :]
            xl = xl.reshape(cp * PAGE_CHUNKS, KV_WIDTH)
            lo = lo + jnp.dot((xl + pos_ref[c, l:l + 1]).astype(jnp.bfloat16), wlo_ref[c, l],
                              preferred_element_type=jnp.float32)
            hi = hi + jnp.dot((xl + pos_ref[c, CMP_STRIDE + l:CMP_STRIDE + l + 1]).astype(jnp.bfloat16), whi_ref[c, l],
                              preferred_element_type=jnp.float32)
        lo_ref[:, :, c * KV_WIDTH:(c + 1) * KV_WIDTH] = lo.reshape(cp, PAGE_CHUNKS, KV_WIDTH)
        hi_ref[:, :, c * KV_WIDTH:(c + 1) * KV_WIDTH] = hi.reshape(cp, PAGE_CHUNKS, KV_WIDTH)


def compress_pages(pages, cmp_pos, w_cmp):
    p = pages.shape[0]
    assert p % CMP_PAGES == 0
    pos = jnp.broadcast_to(jnp.transpose(cmp_pos, (1, 0, 2))[:, :, None, :],
                           (2, CMP_BLOCK, NSA_KV_HEADS, HEAD_DIM)).reshape(2, CMP_BLOCK, KV_WIDTH)
    eye = jnp.eye(NSA_KV_HEADS, dtype=jnp.float32)
    wbd = jnp.einsum('cldo,hH->clhdHo', w_cmp, eye).reshape(2, CMP_BLOCK, KV_WIDTH, KV_WIDTH).astype(jnp.bfloat16)
    out = jax.ShapeDtypeStruct((p, PAGE_CHUNKS, KV_ROW), jnp.float32)
    spec3 = lambda a, b_: pl.BlockSpec((CMP_PAGES, a, b_), lambda i: (i, 0, 0))
    w_spec = pl.BlockSpec((2, CMP_STRIDE, KV_WIDTH, KV_WIDTH), lambda i: (0, 0, 0, 0))
    return pl.pallas_call(
        _compress_body, grid=(p // CMP_PAGES,),
        in_specs=[pl.BlockSpec((CMP_PAGES, 2, NSA_KV_HEADS, HEAD_DIM, PAGE_SIZE), lambda i: (i, 0, 0, 0, 0)),
                  pl.BlockSpec((2, CMP_BLOCK, KV_WIDTH), lambda i: (0, 0, 0)), w_spec, w_spec],
        out_specs=[spec3(PAGE_CHUNKS, KV_ROW), spec3(PAGE_CHUNKS, KV_ROW)],
        out_shape=[out, out],
        scratch_shapes=[pltpu.VMEM((CMP_PAGES, 2, PAGE_SIZE, KV_WIDTH), jnp.float32)],
        compiler_params=pltpu.CompilerParams(dimension_semantics=("arbitrary",), vmem_limit_bytes=VMEM_LIMIT),
        name="cmp_compress",
    )(pages, pos, wbd[:, :CMP_STRIDE], wbd[:, CMP_STRIDE:])


def position_minor(kv):
    return jnp.transpose(kv, (0, 2, 3, 4, 1))


def compress_blocks(kv, cmp_pos, w_cmp):
    b, l = kv.shape[:2]
    pages = kv.reshape(b * l // PAGE_SIZE, PAGE_SIZE, 2, NSA_KV_HEADS, HEAD_DIM)
    lo, hi = compress_pages(position_minor(pages), cmp_pos, w_cmp)
    lo = lo.reshape(b, l // CMP_STRIDE, 2, NSA_KV_HEADS, HEAD_DIM)
    hi = hi.reshape(b, l // CMP_STRIDE, 2, NSA_KV_HEADS, HEAD_DIM)
    return lo[:, :-1] + hi[:, 1:]


def cmp_branch(qg, t, summ):
    n = summ.shape[1]
    end = jnp.arange(n) * CMP_STRIDE + (CMP_BLOCK - 1)
    mask = end[None, :] <= t[:, None]
    s = jnp.einsum('bqhgd,bnhd->bhgqn', qg, summ[:, :, 0]) * (HEAD_DIM ** -0.5)
    p = masked_softmax(s, mask)
    o = jnp.einsum('bhgqn,bnhd->bqhgd', p.astype(qg.dtype), summ[:, :, 1])
    return o, jnp.sum(p, axis=2)


def select_blocks(p_cmp, t, n_sel):
    n_cmp = p_cmp.shape[-1]
    i = jnp.arange(n_cmp)[:, None]
    j = jnp.arange(n_sel)[None, :]
    overlap = jnp.clip(jnp.minimum(i * CMP_STRIDE + CMP_BLOCK, (j + 1) * SEL_BLOCK)
                       - jnp.maximum(i * CMP_STRIDE, j * SEL_BLOCK), 0, None)
    score = jnp.einsum('bhqn,nj->bhqj', p_cmp, overlap.astype(jnp.float32) / CMP_BLOCK)
    tb = (t // SEL_BLOCK)[:, None]
    forced = (j == 0) | (j == tb) | (j == tb - 1)
    score = jnp.where(forced, FORCE_SCORE, score)
    score = jnp.where(j <= tb, score, -1.0)
    top_s, idx = lax.top_k(score, min(SEL_TOPN, n_sel))
    return idx, top_s >= 0.0


def slc_branch(qg, t, kv_sel, idx, ok):
    b, h, q, n = idx.shape
    kpos = idx[..., None] * SEL_BLOCK + jnp.arange(SEL_BLOCK)
    mask = (ok[..., None] & (kpos <= t[None, None, :, None, None])).reshape(b, h, 1, q, n * SEL_BLOCK)
    s = jnp.einsum('bqhgd,bhqnld->bhgqnl', qg, kv_sel[..., 0, :]) * (HEAD_DIM ** -0.5)
    p = masked_softmax(s.reshape(b, h, NSA_GROUP, q, n * SEL_BLOCK), mask)
    v = kv_sel[..., 1, :].reshape(b, h, q, n * SEL_BLOCK, HEAD_DIM)
    return jnp.einsum('bhgqk,bhqkd->bqhgd', p.astype(qg.dtype), v)


def win_branch(qg, t, kv_w, kpos):
    d = t[:, None] - kpos[None, :]
    mask = (d >= 0) & (d < WINDOW) & (kpos[None, :] >= 0)
    s = jnp.einsum('bqhgd,bkhd->bhgqk', qg, kv_w[:, :, 0]) * (HEAD_DIM ** -0.5)
    p = masked_softmax(s, mask)
    return jnp.einsum('bhgqk,bkhd->bqhgd', p.astype(qg.dtype), kv_w[:, :, 1])


def gate_merge(o_c, o_s, o_w, g):
    o = o_c * g[..., 0:1] + o_s * g[..., 1:2] + o_w * g[..., 2:3]
    return o.reshape(o.shape[0], o.shape[1], NSA_WIDTH)


def nsa_sample(parts, cache_cmp, cache_slc, page_table, state_win, cmp_pos, w_cmp):
    q, kv_cmp, kv_slc, kv_win, gates = parts
    b, n = q.shape[:2]
    n_pages = page_table.shape[1]
    past = n_pages * PAGE_SIZE
    total = past + n
    t = past + jnp.arange(n)
    b_ix = jnp.arange(b)[:, None, None, None]
    h_ix = jnp.arange(NSA_KV_HEADS)[None, :, None, None]
    assert n <= CMP_STRIDE
    lo_pool, hi_pool = compress_pages(position_minor(cache_cmp), cmp_pos, w_cmp)
    as_chunks = lambda u: u[page_table].reshape(b, past // CMP_STRIDE, 2, NSA_KV_HEADS, HEAD_DIM)
    lo_past, hi_past = as_chunks(lo_pool), as_chunks(hi_pool)
    new_chunk = jnp.pad(kv_cmp, ((0, 0), (0, CMP_STRIDE - n), (0, 0), (0, 0), (0, 0)))
    hi_new = jnp.einsum('blchd,cldo->bcho', new_chunk + cmp_pos[None, CMP_STRIDE:, :, None, :], w_cmp[:, CMP_STRIDE:])
    summ = lo_past + jnp.concatenate([hi_past[:, 1:], hi_new[:, None]], axis=1)
    o_c, p_c = cmp_branch(q, t, summ)
    n_sel = -(-total // SEL_BLOCK)
    idx, ok = select_blocks(p_c, t, n_sel)
    past_blocks = past // SEL_BLOCK
    bpp = PAGE_SIZE // SEL_BLOCK
    phys = page_table[b_ix, jnp.minimum(idx // bpp, n_pages - 1)]
    pool = cache_slc.reshape(cache_slc.shape[0], bpp, SEL_BLOCK, 2, NSA_KV_HEADS, HEAD_DIM)
    from_past = pool[phys, idx % bpp, :, :, h_ix, :]
    n_new_blk = -(-n // SEL_BLOCK)
    new_blocks = jnp.pad(kv_slc, ((0, 0), (0, n_new_blk * SEL_BLOCK - n), (0, 0), (0, 0), (0, 0)))
    new_blocks = new_blocks.reshape(b, n_new_blk, SEL_BLOCK, 2, NSA_KV_HEADS, HEAD_DIM)
    from_new = new_blocks[b_ix, jnp.clip(idx - past_blocks, 0, n_new_blk - 1), :, :, h_ix, :]
    kv_sel = jnp.where((idx >= past_blocks)[..., None, None, None], from_new, from_past)
    o_s = slc_branch(q, t, kv_sel, idx, ok)
    wb = state_win.shape[1]
    kv_w = jnp.concatenate([state_win, kv_win], axis=1)
    o_w = win_branch(q, t, kv_w, past - wb + jnp.arange(wb + n))
    o = gate_merge(o_c, o_s, o_w, gates)
    return o, (kv_cmp, kv_slc, kv_w[:, n:])


def rwkv7_time_mix(z, prev_row, wkv0, shift_mu, w0, w2, a0, a2, g2, k_k, k_a, r_k, gn_g, gn_b):
    b, l = z.shape[:2]
    zf = z.astype(jnp.float32)
    prev = jnp.concatenate([prev_row[:, None].astype(jnp.float32), zf[:, :-1]], axis=1)
    zs = zf + shift_mu * (prev - zf)
    r, k, v, wd, ad, gd = jnp.split(zs, RWKV_SPLITS, axis=-1)
    w = -jax.nn.softplus(-(w0 + jnp.tanh(wd) @ w2)) - 0.5
    decay = jnp.exp(-jnp.exp(w))
    a = jax.nn.sigmoid(a0 + ad @ a2)
    g = jax.nn.sigmoid(gd) @ g2
    heads = lambda u: u.reshape(b, l, RWKV_HEADS, HEAD_DIM)
    kk = heads(k * k_k)
    kk = kk / jnp.maximum(jnp.sqrt(jnp.sum(kk * kk, -1, keepdims=True)), 1e-12)
    k = k * (1.0 + (a - 1.0) * k_a)
    o, s_final = rwkv_scan(r, k, v, kk.reshape(b, l, RWKV_WIDTH), a, decay, wkv0)
    r, k, v = heads(r), heads(k), heads(v)
    mu = jnp.mean(o, -1, keepdims=True)
    var = jnp.mean(jnp.square(o - mu), -1, keepdims=True)
    o = ((o - mu) * lax.rsqrt(var + RWKV_GN_EPS)).reshape(b, l, RWKV_WIDTH) * gn_g + gn_b
    bonus = (jnp.sum(r * k * r_k, -1, keepdims=True) * v).reshape(b, l, RWKV_WIDTH)
    out = (o + bonus) * g
    return out.astype(z.dtype), s_final, z[:, -1]


def moe_ffn(x, w_router, b_router, w_gate_up, b_gate_up, w_down, b_down):
    shape = x.shape
    xt = x.reshape(-1, shape[-1])
    n_tok = xt.shape[0]
    logits = (xt @ w_router).astype(jnp.float32) + b_router.astype(jnp.float32)
    top_v, top_e = lax.top_k(logits, TOP_K)
    gates = jax.nn.softmax(top_v, axis=-1)
    flat_e = top_e.reshape(-1)
    n_assign = n_tok * TOP_K
    order = jnp.argsort(flat_e)
    rank = jnp.argsort(order)
    counts = jnp.sum(flat_e[:, None] == jnp.arange(N_EXPERTS)[None, :], axis=0)
    rows = MOE_ROWS
    padded = (counts + rows - 1) // rows * rows
    pad_end = jnp.cumsum(padded)
    pad_start = pad_end - padded
    grp_start = jnp.cumsum(counts) - counts
    n_blocks = -(-n_assign // rows) + N_EXPERTS
    n_rows = n_blocks * rows
    blk_e = jnp.minimum(jnp.sum(pad_end[None, :] <= (jnp.arange(n_blocks) * rows)[:, None], axis=1), N_EXPERTS - 1)
    row_e = jnp.repeat(blk_e, rows)
    within = jnp.arange(n_rows) - pad_start[row_e]
    src = jnp.clip(grp_start[row_e] + within, 0, n_assign - 1)
    row_token = jnp.where(within < counts[row_e], order[src] // TOP_K, n_tok).astype(jnp.int32)
    x_pad = jnp.concatenate([xt.astype(jnp.bfloat16), jnp.zeros((1, shape[-1]), jnp.bfloat16)], axis=0)
    xb = x_pad[row_token]
    n_used = (pad_end[-1] // rows).astype(jnp.int32).reshape(1)
    yb = expert_blocks(xb, blk_e.astype(jnp.int32), n_used, w_gate_up, b_gate_up, w_down, b_down)
    dest = (pad_start[flat_e] + rank - grp_start[flat_e]).reshape(n_tok, TOP_K)
    y = gates[:, 0:1] * yb[dest[:, 0]]
    for j in range(1, TOP_K):
        y = y + gates[:, j:j + 1] * yb[dest[:, j]]
    return y.reshape(shape)


def kernel(x_prompt, x_sample, cache_cmp, cache_slc, page_table, state_win, state_wkv, state_shift,
           norm_mix_g, w_in, cmp_pos, w_cmp, shift_mu, w0, w2, a0, a2, g2, k_k, k_a, r_k, gn_g, gn_b,
           w_out, norm_ffn_g, w_router, b_router, w_gate_up, b_gate_up, w_down, b_down, norm_final_g):
    xp, xs = x_prompt, x_sample
    l = 0
    rw = (shift_mu[l], w0[l], w2[l], a0[l], a2[l], g2[l], k_k[l], k_a[l], r_k[l], gn_g[l], gn_b[l])
    ffn = (w_router[l], b_router[l], w_gate_up[l], b_gate_up[l], w_down[l], b_down[l])
    bp, sp = xp.shape[:2]
    bs, ss = xs.shape[:2]
    nsa_p, zr_p = in_proj(xp, norm_mix_g[l], w_in[l])
    oa, (kc_p, ksl_p, kw_p) = nsa_prompt(nsa_p, cmp_pos[l], w_cmp[l])
    ob, wkv_p, sh_p = rwkv7_time_mix(zr_p, jnp.zeros((bp, RWKV_COLS), zr_p.dtype),
                                     jnp.zeros((bp, RWKV_HEADS, HEAD_DIM, HEAD_DIM), jnp.float32), *rw)
    xp = xp + jnp.concatenate([oa, ob], axis=-1) @ w_out[l]

    nsa_s, zr_s = in_proj(xs, norm_mix_g[l], w_in[l])
    oa, (kc_s, ksl_s, kw_s) = nsa_sample(nsa_s, cache_cmp[l], cache_slc[l], page_table, state_win[l],
                                         cmp_pos[l], w_cmp[l])
    ob, wkv_s, sh_s = rwkv7_time_mix(zr_s, state_shift[l], state_wkv[l], *rw)
    xs = xs + jnp.concatenate([oa, ob], axis=-1) @ w_out[l]
    x_all = jnp.concatenate([xp.reshape(bp * sp, D_MODEL), xs.reshape(bs * ss, D_MODEL)], axis=0)
    m_all = moe_ffn(rms_norm(x_all, norm_ffn_g[l]), *ffn)
    xp = xp + m_all[:bp * sp].reshape(xp.shape)
    xs = xs + m_all[bp * sp:].reshape(xs.shape)

    y_prompt = rms_norm(xp, norm_final_g)
    y_sample = rms_norm(xs, norm_final_g)
    st = lambda u: u[None]
    return (y_prompt, y_sample, st(kc_p), st(ksl_p), st(kw_p), st(wkv_p), st(sh_p),
            st(kc_s), st(ksl_s), st(kw_s), st(wkv_s), st(sh_s))
```

```python
import functools
import math

import jax
import jax.numpy as jnp
from jax import lax
from jax.experimental import pallas as pl
from jax.experimental.pallas import tpu as pltpu

D_MODEL = 1024
DEPTH = 1
PAGE_SIZE = 128

HEAD_DIM = 64
NSA_HEADS = 8
NSA_KV_HEADS = 2
NSA_GROUP = NSA_HEADS // NSA_KV_HEADS
CMP_BLOCK = 32
CMP_STRIDE = 16
SEL_BLOCK = 64
SEL_TOPN = 16
WINDOW = 512
Q_BLOCK = 128
FORCE_SCORE = 1e6
RWKV_HEADS = 8
RWKV_DECAY_LORA = 64
RWKV_A_LORA = 64
RWKV_GATE_LORA = 128
RWKV_GN_EPS = 64e-5
N_EXPERTS = 32
TOP_K = 4
D_FF = 1024
SWIGLU_LIMIT = 7.0
SWIGLU_ALPHA = 1.702
EXPERT_BLOCK = 128
RMS_EPS = 1e-5

NSA_WIDTH = NSA_HEADS * HEAD_DIM
KV_WIDTH = NSA_KV_HEADS * HEAD_DIM
RWKV_WIDTH = RWKV_HEADS * HEAD_DIM
MIX_WIDTH = NSA_WIDTH + RWKV_WIDTH
NSA_COLS = NSA_WIDTH + 6 * KV_WIDTH + 3 * NSA_HEADS
RWKV_COLS = 3 * RWKV_WIDTH + RWKV_DECAY_LORA + RWKV_A_LORA + RWKV_GATE_LORA
IN_COLS = NSA_COLS + RWKV_COLS
RWKV_SPLITS = (RWKV_WIDTH, 2 * RWKV_WIDTH, 3 * RWKV_WIDTH, 3 * RWKV_WIDTH + RWKV_DECAY_LORA,
               3 * RWKV_WIDTH + RWKV_DECAY_LORA + RWKV_A_LORA)

LANE = 128
VMEM_LIMIT = 48 * 1024 * 1024


GATE_COLS = 3 * NSA_HEADS
PROJ_GROUPS = (NSA_WIDTH, 2 * KV_WIDTH, 2 * KV_WIDTH, 2 * KV_WIDTH, RWKV_COLS, LANE)


def _in_proj_body(x_ref, g_ref, w_ref, *o_refs):
    x = x_ref[...]
    y = x * lax.rsqrt(jnp.mean(x * x, -1, keepdims=True) + RMS_EPS) * g_ref[...]
    z = jnp.dot(y.astype(jnp.bfloat16), w_ref[...], preferred_element_type=jnp.float32)
    c0 = 0
    for o_ref, width in zip(o_refs, PROJ_GROUPS):
        o_ref[...] = z[:, c0:c0 + width]
        c0 += width


def in_proj(x, g, w, block_rows=512):
    b, l, d = x.shape
    n = b * l
    gate0 = NSA_WIDTH + 6 * KV_WIDTH
    wb = jnp.concatenate([w[:, :gate0], w[:, NSA_COLS:], w[:, gate0:NSA_COLS],
                          jnp.zeros((d, LANE - GATE_COLS), w.dtype)], axis=1).astype(jnp.bfloat16)
    cols = sum(PROJ_GROUPS)
    outs = pl.pallas_call(
        _in_proj_body,
        grid=(n // block_rows,),
        in_specs=[pl.BlockSpec((block_rows, d), lambda i: (i, 0)),
                  pl.BlockSpec((1, d), lambda i: (0, 0)),
                  pl.BlockSpec((d, cols), lambda i: (0, 0))],
        out_specs=[pl.BlockSpec((block_rows, width), lambda i: (i, 0)) for width in PROJ_GROUPS],
        out_shape=[jax.ShapeDtypeStruct((n, width), jnp.float32) for width in PROJ_GROUPS],
        compiler_params=pltpu.CompilerParams(dimension_semantics=("arbitrary",), vmem_limit_bytes=VMEM_LIMIT),
        name="in_proj",
    )(x.reshape(n, d), g.reshape(1, d), wb)
    q = outs[0].reshape(b, l, NSA_KV_HEADS, NSA_GROUP, HEAD_DIM)
    kv = [o.reshape(b, l, KV_ROW) for o in outs[1:4]]
    gates = jax.nn.sigmoid(outs[5][:, :GATE_COLS]).reshape(b, l, NSA_KV_HEADS, NSA_GROUP, 3)
    return (q, kv[0], kv[1], kv[2], gates), outs[4].reshape(b, l, RWKV_COLS), outs[5].reshape(b, l, LANE)


QROWS = NSA_GROUP * Q_BLOCK
SLC_TK = 512
MASK_BIAS = -float(2 ** 30)
NEG_INIT = -3.0e38
V_AUG = 2 * HEAD_DIM

_NT = (((1,), (1,)), ((), ()))


def _cmp_select_body(q_ref, kc_ref, vc_ref, ov_ref, oc_ref, bias_ref):
    qb = pl.program_id(1)
    ncp = kc_ref.shape[2]
    nsel = ov_ref.shape[1]
    row = lax.broadcasted_iota(jnp.int32, (QROWS, ncp), 0)
    t = qb * Q_BLOCK + (row & (Q_BLOCK - 1))
    n = lax.broadcasted_iota(jnp.int32, (QROWS, ncp), 1)
    mask = n * CMP_STRIDE + (CMP_BLOCK - 1) <= t
    scores = []
    for h in range(NSA_KV_HEADS):
        q = q_ref[0, h, 0]
        s = lax.dot_general(q, kc_ref[0, h], _NT, preferred_element_type=jnp.float32)
        s = jnp.where(mask, s, -jnp.inf)
        m = jnp.max(s, -1, keepdims=True)
        m = jnp.where(m == -jnp.inf, 0.0, m)
        e = jnp.exp(s - m)
        p = e / jnp.maximum(jnp.sum(e, -1, keepdims=True), 1e-30)
        oc_ref[0, h, 0] = jnp.dot(p.astype(jnp.bfloat16), vc_ref[0, h], preferred_element_type=jnp.float32)
        pc = p[0:Q_BLOCK]
        for g in range(1, NSA_GROUP):
            pc = pc + p[g * Q_BLOCK:(g + 1) * Q_BLOCK]
        scores.append(jnp.dot(pc.astype(jnp.bfloat16), ov_ref[...], preferred_element_type=jnp.float32))
    rows = NSA_KV_HEADS * Q_BLOCK
    score = jnp.concatenate(scores, axis=0)
    j = lax.broadcasted_iota(jnp.int32, (rows, nsel), 1)
    tq = qb * Q_BLOCK + (lax.broadcasted_iota(jnp.int32, (rows, nsel), 0) & (Q_BLOCK - 1))
    tb = tq >> (SEL_BLOCK.bit_length() - 1)
    forced = (j == 0) | (j == tb) | (j == tb - 1)
    score = jnp.where(forced, FORCE_SCORE, score)
    score = jnp.where(j <= tb, score, -1.0)
    jf = j.astype(jnp.float32)
    sel = jnp.zeros((rows, nsel), jnp.bool_)
    for _ in range(min(SEL_TOPN, nsel)):
        top = jnp.max(score, -1, keepdims=True)
        first = jnp.min(jnp.where(score == top, jf, float(nsel)), -1, keepdims=True)
        pick = jf == first
        sel = sel | pick
        score = jnp.where(pick, -2.0, score)
    bias = jnp.where(sel, 0.0, MASK_BIAS).astype(jnp.bfloat16)
    for h in range(NSA_KV_HEADS):
        bias_ref[0, h] = bias[h * Q_BLOCK:(h + 1) * Q_BLOCK]


def _slc_body(q_ref, bias_ref, kaug_ref, vaug_ref, o_ref, qaug_sc, m_sc, acc_sc):
    qb = pl.program_id(2)
    start = qb * Q_BLOCK
    nsel = bias_ref.shape[3]
    bias = bias_ref[0, 0]
    for g in range(NSA_GROUP):
        qaug_sc[g * Q_BLOCK:(g + 1) * Q_BLOCK, 0:nsel] = bias
    qaug_sc[:, nsel:nsel + HEAD_DIM] = q_ref[0, 0, 0]
    m_sc[...] = jnp.full(m_sc.shape, NEG_INIT, jnp.float32)
    acc_sc[...] = jnp.zeros(acc_sc.shape, jnp.float32)

    def scores(kt):
        k0 = pl.multiple_of(kt * SLC_TK, SLC_TK)
        return lax.dot_general(qaug_sc[...], kaug_ref[0, 0, pl.ds(k0, SLC_TK), :], _NT,
                               preferred_element_type=jnp.float32)

    def absorb(kt, s, causal):
        k0 = pl.multiple_of(kt * SLC_TK, SLC_TK)
        slabs = [s[:, c * LANE:(c + 1) * LANE] for c in range(SLC_TK // LANE)]
        if causal:
            row = lax.broadcasted_iota(jnp.int32, (QROWS, LANE), 0)
            col = lax.broadcasted_iota(jnp.int32, (QROWS, LANE), 1)
            t_rel = start + (row & (Q_BLOCK - 1)) - k0 - col
            slabs = [jnp.where(t_rel >= c * LANE, sl, -jnp.inf) for c, sl in enumerate(slabs)]
        top = slabs[0]
        for sl in slabs[1:]:
            top = jnp.maximum(top, sl)
        m_prev = m_sc[...]
        m_new = jnp.maximum(m_prev, jnp.max(top, -1, keepdims=True))
        p = jnp.concatenate([jnp.exp(sl - m_new) for sl in slabs], axis=1).astype(jnp.bfloat16)
        acc_sc[...] = jnp.exp(m_prev - m_new) * acc_sc[...] + jnp.dot(
            p, vaug_ref[0, 0, pl.ds(k0, SLC_TK), :], preferred_element_type=jnp.float32)
        m_sc[...] = m_new

    n_full = start // SLC_TK

    def body(kt, s_cur):
        s_next = scores(kt + 1)
        absorb(kt, s_cur, False)
        return s_next

    s_last = lax.fori_loop(0, n_full, body, scores(0))
    absorb(n_full, s_last, True)
    acc = acc_sc[...]
    o_ref[0, 0, 0] = acc[:, :HEAD_DIM] / jnp.maximum(acc[:, HEAD_DIM:HEAD_DIM + 1], 1e-30)


def _win_body(q_ref, kpad_ref, vpad_ref, o_ref):
    qb = pl.program_id(2)
    start = pl.multiple_of(qb * Q_BLOCK, Q_BLOCK)
    nk = Q_BLOCK + WINDOW
    k = kpad_ref[0, 0, pl.ds(start, nk), :]
    s = lax.dot_general(q_ref[0, 0, 0], k, _NT, preferred_element_type=jnp.float32)
    r = lax.broadcasted_iota(jnp.int32, (QROWS, nk), 0) & (Q_BLOCK - 1)
    c = lax.broadcasted_iota(jnp.int32, (QROWS, nk), 1)
    d = r + WINDOW - c
    mask = (d >= 0) & (d < WINDOW) & (c >= WINDOW - start)
    s = jnp.where(mask, s, -jnp.inf)
    m = jnp.max(s, -1, keepdims=True)
    m = jnp.where(m == -jnp.inf, 0.0, m)
    p = jnp.exp(s - m)
    pv = jnp.dot(p.astype(jnp.bfloat16), vpad_ref[0, 0, pl.ds(start, nk), :], preferred_element_type=jnp.float32)
    o_ref[0, 0, 0] = pv[:, :HEAD_DIM] / jnp.maximum(pv[:, HEAD_DIM:HEAD_DIM + 1], 1e-30)


def _to_tiles(u, scale=None):
    b, s = u.shape[:2]
    u = u.reshape(b, s // Q_BLOCK, Q_BLOCK, NSA_KV_HEADS, NSA_GROUP, HEAD_DIM)
    u = jnp.transpose(u, (0, 3, 1, 4, 2, 5)).reshape(b, NSA_KV_HEADS, s // Q_BLOCK, QROWS, HEAD_DIM)
    return u if scale is None else (u * scale).astype(jnp.bfloat16)


def _head_major(kv):
    return jnp.transpose(kv, (0, 2, 1, 3))


def _with_ones(v):
    ones = jnp.ones(v.shape[:-1] + (1,), v.dtype)
    zeros = jnp.zeros(v.shape[:-1] + (V_AUG - HEAD_DIM - 1,), v.dtype)
    return jnp.concatenate([v, ones, zeros], -1).astype(jnp.bfloat16)


def _split_heads(kv_rows):
    piece = lambda c: jnp.stack([kv_rows[..., (c * NSA_KV_HEADS + h) * HEAD_DIM:(c * NSA_KV_HEADS + h + 1) * HEAD_DIM]
                                 for h in range(NSA_KV_HEADS)], axis=1)
    return piece(0), piece(1)


def nsa_prompt(parts, cmp_pos, w_cmp):
    q, kv_cmp, kv_slc, kv_win, gates = parts
    b, s = q.shape[:2]
    nq = s // Q_BLOCK
    nsel = s // SEL_BLOCK
    assert s % SLC_TK == 0 and nsel % LANE == 0
    summ = compress_blocks(kv_cmp, cmp_pos, w_cmp)
    n_cmp = summ.shape[1]
    ncp = -(-n_cmp // LANE) * LANE
    summ = jnp.pad(summ, ((0, 0), (0, ncp - n_cmp), (0, 0), (0, 0), (0, 0))).astype(jnp.bfloat16)
    kc, vc = _head_major(summ[:, :, 0]), _head_major(summ[:, :, 1])
    i = jnp.arange(ncp)[:, None]
    jj = jnp.arange(nsel)[None, :]
    overlap = jnp.clip(jnp.minimum(i * CMP_STRIDE + CMP_BLOCK, (jj + 1) * SEL_BLOCK)
                       - jnp.maximum(i * CMP_STRIDE, jj * SEL_BLOCK), 0, None)
    ov = (overlap.astype(jnp.float32) / CMP_BLOCK).astype(jnp.bfloat16)
    qt = _to_tiles(q, HEAD_DIM ** -0.5)

    grid = (b, NSA_KV_HEADS, nq)
    params = pltpu.CompilerParams(dimension_semantics=("arbitrary",) * 3, vmem_limit_bytes=VMEM_LIMIT)
    q_spec = pl.BlockSpec((1, 1, 1, QROWS, HEAD_DIM), lambda bi, h, qi: (bi, h, qi, 0, 0))
    o_spec = pl.BlockSpec((1, 1, 1, QROWS, HEAD_DIM), lambda bi, h, qi: (bi, h, qi, 0, 0))
    o_shape = jax.ShapeDtypeStruct((b, NSA_KV_HEADS, nq, QROWS, HEAD_DIM), jnp.float32)
    seq_spec = lambda rows, width: pl.BlockSpec((1, 1, rows, width), lambda bi, h, qi: (bi, h, 0, 0))
    bias_spec = pl.BlockSpec((1, 1, Q_BLOCK, nsel), lambda bi, h, qi: (bi, h, qi, 0))

    kvh = NSA_KV_HEADS
    both_q = pl.BlockSpec((1, kvh, 1, QROWS, HEAD_DIM), lambda bi, qi: (bi, 0, qi, 0, 0))
    both_seq = pl.BlockSpec((1, kvh, ncp, HEAD_DIM), lambda bi, qi: (bi, 0, 0, 0))
    o_c, bias = pl.pallas_call(
        _cmp_select_body, grid=(b, nq),
        in_specs=[both_q, both_seq, both_seq, pl.BlockSpec((ncp, nsel), lambda bi, qi: (0, 0))],
        out_specs=[both_q, pl.BlockSpec((1, kvh, Q_BLOCK, nsel), lambda bi, qi: (bi, 0, qi, 0))],
        out_shape=[o_shape, jax.ShapeDtypeStruct((b, kvh, s, nsel), jnp.bfloat16)],
        compiler_params=pltpu.CompilerParams(dimension_semantics=("arbitrary",) * 2, vmem_limit_bytes=VMEM_LIMIT),
        name="nsa_cmp_select",
    )(qt, kc, vc, ov)

    k_s, v_s = _split_heads(kv_slc)
    onehot = (jnp.arange(s)[:, None] // SEL_BLOCK == jnp.arange(nsel)[None, :]).astype(jnp.bfloat16)
    kaug = jnp.concatenate([jnp.broadcast_to(onehot, (b, NSA_KV_HEADS, s, nsel)), k_s.astype(jnp.bfloat16)], -1)
    o_s = pl.pallas_call(
        _slc_body, grid=grid,
        in_specs=[q_spec, bias_spec, seq_spec(s, nsel + HEAD_DIM), seq_spec(s, V_AUG)],
        out_specs=o_spec, out_shape=o_shape,
        scratch_shapes=[pltpu.VMEM((QROWS, nsel + HEAD_DIM), jnp.bfloat16),
                        pltpu.VMEM((QROWS, LANE), jnp.float32),
                        pltpu.VMEM((QROWS, V_AUG), jnp.float32)],
        compiler_params=params, name="nsa_slc",
    )(qt, bias, kaug, _with_ones(v_s))

    front = ((0, 0), (0, 0), (WINDOW, 0), (0, 0))
    k_w, v_w = _split_heads(kv_win)
    k_w = jnp.pad(k_w.astype(jnp.bfloat16), front)
    v_w = jnp.pad(_with_ones(v_w), front)
    o_w = pl.pallas_call(
        _win_body, grid=grid,
        in_specs=[q_spec, seq_spec(s + WINDOW, HEAD_DIM), seq_spec(s + WINDOW, V_AUG)],
        out_specs=o_spec, out_shape=o_shape,
        compiler_params=params, name="nsa_win",
    )(qt, k_w, v_w)

    as_rows = lambda u: u.reshape(u.shape[0], u.shape[1], 2, NSA_KV_HEADS, HEAD_DIM)
    return (o_c, o_s, o_w), (as_rows(kv_cmp), as_rows(kv_slc), as_rows(kv_win[:, s - min(WINDOW, s):]))


RWKV_PAIRS = RWKV_HEADS // 2
RWKV_TC = 256
RWKV_NB = 4
RWKV_GROUP = 8
V_PIECES = 4
V_STEPS = LANE // (V_PIECES * RWKV_HEADS)


def _rwkv_body(nkk_ref, w_ref, bb_ref, k_ref, r_ref, vt_ref, e2_ref, ebd_ref, s0_ref, o_ref, sT_ref, s_sc):
    nb, tc = nkk_ref.shape[0], nkk_ref.shape[1]
    ci = pl.program_id(1)

    @pl.when(ci == 0)
    def _():
        s_sc[...] = s0_ref[...]

    lane = lax.broadcasted_iota(jnp.int32, (1, LANE), 1)
    left = lane < HEAD_DIM
    vlane = lax.broadcasted_iota(jnp.int32, (HEAD_DIM, LANE), 1)
    vstep = (vlane >> (RWKV_HEADS.bit_length() - 1)) & (V_STEPS - 1)

    def one_step(b, t0, i, rows):
        nkk, w, bb, k, r = rows
        t = t0 + i
        row = lambda u, p: u[i:i + 1, p * LANE:(p + 1) * LANE]
        vm = jnp.where(vstep == i % V_STEPS, vt_ref[b, t0 // V_STEPS + i // V_STEPS], jnp.zeros((), jnp.bfloat16))
        vb = jnp.dot(vm, e2_ref[...], preferred_element_type=jnp.float32)
        rbd = (ebd_ref[...] * r[i:i + 1, :]).astype(jnp.bfloat16)
        o = jnp.zeros((RWKV_HEADS, HEAD_DIM), jnp.float32)
        for p in range(RWKV_PAIRS):
            s = s_sc[b, p]
            c = row(nkk, p)
            c_l = jnp.where(left, c, 0.0)
            c_r = jnp.where(left, 0.0, c)
            sa = jnp.where(left, jnp.sum(s * c_l, -1, keepdims=True), jnp.sum(s * c_r, -1, keepdims=True))
            s = s * row(w, p) + sa * row(bb, p) + vb[:, p * LANE:(p + 1) * LANE] * row(k, p)
            s_sc[b, p] = s
            o = o + lax.dot_general(rbd[:, p * LANE:(p + 1) * LANE], s.astype(jnp.bfloat16), _NT,
                                    preferred_element_type=jnp.float32)
        o_ref[b, pl.ds(pl.multiple_of(t * RWKV_HEADS, RWKV_HEADS), RWKV_HEADS), :] = o

    group = min(tc, RWKV_GROUP)

    def group_steps(gi, carry):
        t0 = pl.multiple_of(gi * group, group)
        rows = [tuple(ref[b, pl.ds(t0, group), :] for ref in (nkk_ref, w_ref, bb_ref, k_ref, r_ref)) for b in range(nb)]
        for i in range(group):
            for b in range(nb):
                one_step(b, t0, i, rows[b])
        return carry

    lax.fori_loop(0, tc // group, group_steps, 0)
    sT_ref[...] = s_sc[...]


def rwkv_scan(r, k, v, nkk, bb, decay, wkv0):
    b, l = r.shape[:2]
    tc = min(l, RWKV_TC)
    nb = RWKV_NB
    assert l % tc == 0 and tc % V_STEPS == 0 and b % nb == 0
    vh = v.reshape(b, l, RWKV_HEADS, HEAD_DIM)
    v1 = vh.astype(jnp.bfloat16)
    res = vh - v1.astype(jnp.float32)
    v2 = res.astype(jnp.bfloat16)
    v3 = (res - v2.astype(jnp.float32)).astype(jnp.bfloat16)
    pieces = jnp.stack([v1, v2, v3, jnp.zeros_like(v1)], 0)
    pieces = pieces.reshape(V_PIECES, b, l // V_STEPS, V_STEPS, RWKV_HEADS, HEAD_DIM)
    vt = jnp.transpose(pieces, (1, 2, 5, 0, 3, 4)).reshape(b, l // V_STEPS, HEAD_DIM, LANE)
    lane_head = jnp.arange(LANE) % RWKV_HEADS
    col_head = jnp.arange(RWKV_WIDTH) // HEAD_DIM
    e2 = (lane_head[:, None] == col_head[None, :]).astype(jnp.bfloat16)
    ebd = (jnp.arange(RWKV_HEADS)[:, None] == col_head[None, :]).astype(jnp.float32)
    to_pairs = lambda s: jnp.transpose(s.reshape(b, RWKV_PAIRS, 2, HEAD_DIM, HEAD_DIM), (0, 1, 3, 2, 4)
                                       ).reshape(b, RWKV_PAIRS, HEAD_DIM, LANE)
    s0 = to_pairs(wkv0.astype(jnp.float32))

    row_spec = pl.BlockSpec((nb, tc, RWKV_WIDTH), lambda bi, ci: (bi, ci, 0))
    state_spec = pl.BlockSpec((nb, RWKV_PAIRS, HEAD_DIM, LANE), lambda bi, ci: (bi, 0, 0, 0))
    o, s_t = pl.pallas_call(
        _rwkv_body, grid=(b // nb, l // tc),
        in_specs=[row_spec] * 5 + [
            pl.BlockSpec((nb, tc // V_STEPS, HEAD_DIM, LANE), lambda bi, ci: (bi, ci, 0, 0)),
            pl.BlockSpec((LANE, RWKV_WIDTH), lambda bi, ci: (0, 0)),
            pl.BlockSpec((RWKV_HEADS, RWKV_WIDTH), lambda bi, ci: (0, 0)),
            state_spec],
        out_specs=[pl.BlockSpec((nb, tc * RWKV_HEADS, HEAD_DIM), lambda bi, ci: (bi, ci, 0)), state_spec],
        out_shape=[jax.ShapeDtypeStruct((b, l * RWKV_HEADS, HEAD_DIM), jnp.float32),
                   jax.ShapeDtypeStruct((b, RWKV_PAIRS, HEAD_DIM, LANE), jnp.float32)],
        scratch_shapes=[pltpu.VMEM((nb, RWKV_PAIRS, HEAD_DIM, LANE), jnp.float32)],
        compiler_params=pltpu.CompilerParams(dimension_semantics=("arbitrary", "arbitrary"),
                                             vmem_limit_bytes=VMEM_LIMIT),
        name="rwkv_scan",
    )(nkk, decay, bb, k, r, vt, e2, ebd, s0)
    s_t = jnp.transpose(s_t.reshape(b, RWKV_PAIRS, HEAD_DIM, 2, HEAD_DIM), (0, 1, 3, 2, 4))
    return o.reshape(b, l, RWKV_HEADS, HEAD_DIM), s_t.reshape(b, RWKV_HEADS, HEAD_DIM, HEAD_DIM)


MOE_ROWS = 512


def _expert_body(blk_e_ref, n_used_ref, x_ref, wgu_ref, bgu_ref, wd_ref, bd_ref, o_ref):
    i = pl.program_id(0)

    @pl.when(i < n_used_ref[0])
    def _():
        hgu = jnp.dot(x_ref[...], wgu_ref[0], preferred_element_type=jnp.float32) + bgu_ref[0]
        gate = jnp.minimum(hgu[:, :D_FF], SWIGLU_LIMIT)
        up = jnp.clip(hgu[:, D_FF:], -SWIGLU_LIMIT, SWIGLU_LIMIT)
        act = (up + 1.0) * gate * jax.nn.sigmoid(SWIGLU_ALPHA * gate)
        o_ref[...] = jnp.dot(act.astype(jnp.bfloat16), wd_ref[0], preferred_element_type=jnp.float32) + bd_ref[0]

    @pl.when(i >= n_used_ref[0])
    def _():
        o_ref[...] = jnp.zeros(o_ref.shape, o_ref.dtype)


def expert_blocks(xb, blk_e, n_used, w_gate_up, b_gate_up, w_down, b_down):
    n_rows, d = xb.shape
    n_blocks = n_rows // MOE_ROWS
    wgu = w_gate_up.astype(jnp.bfloat16)
    wd = w_down.astype(jnp.bfloat16)
    grid_spec = pltpu.PrefetchScalarGridSpec(
        num_scalar_prefetch=2, grid=(n_blocks,),
        in_specs=[pl.BlockSpec((MOE_ROWS, d), lambda i, e, n: (i, 0)),
                  pl.BlockSpec((1, d, 2 * D_FF), lambda i, e, n: (e[i], 0, 0)),
                  pl.BlockSpec((1, 1, 2 * D_FF), lambda i, e, n: (e[i], 0, 0)),
                  pl.BlockSpec((1, D_FF, d), lambda i, e, n: (e[i], 0, 0)),
                  pl.BlockSpec((1, 1, d), lambda i, e, n: (e[i], 0, 0))],
        out_specs=pl.BlockSpec((MOE_ROWS, d), lambda i, e, n: (i, 0)))
    return pl.pallas_call(
        _expert_body, grid_spec=grid_spec,
        out_shape=jax.ShapeDtypeStruct((n_rows, d), jnp.float32),
        compiler_params=pltpu.CompilerParams(dimension_semantics=("arbitrary",), vmem_limit_bytes=VMEM_LIMIT),
        name="moe_experts",
    )(blk_e, n_used, xb, wgu, b_gate_up.reshape(N_EXPERTS, 1, 2 * D_FF), wd, b_down.reshape(N_EXPERTS, 1, d))


def rms_norm(x, g):
    xf = x.astype(jnp.float32)
    y = xf * lax.rsqrt(jnp.mean(xf * xf, -1, keepdims=True) + RMS_EPS)
    return (y * g.astype(jnp.float32)).astype(x.dtype)


def masked_softmax(s, mask):
    s = jnp.where(mask, s.astype(jnp.float32), -jnp.inf)
    m = jnp.max(s, -1, keepdims=True)
    m = jnp.where(jnp.isfinite(m), m, 0.0)
    e = jnp.where(mask, jnp.exp(s - m), 0.0)
    return e / jnp.maximum(jnp.sum(e, -1, keepdims=True), 1e-30)


CMP_PAGES = 32
PAGE_CHUNKS = PAGE_SIZE // CMP_STRIDE
KV_ROW = 2 * KV_WIDTH


def _compress_body(x_ref, pos_ref, wlo_ref, whi_ref, lo_ref, hi_ref, xt_sc):
    cp = x_ref.shape[0]

    def to_rows(pi, carry):
        for c in range(2):
            xt_sc[pi, c] = x_ref[pi, c].reshape(KV_WIDTH, PAGE_SIZE).T
        return carry

    lax.fori_loop(0, cp, to_rows, 0, unroll=4)
    for c in range(2):
        lo = jnp.zeros((cp * PAGE_CHUNKS, KV_WIDTH), jnp.float32)
        hi = jnp.zeros((cp * PAGE_CHUNKS, KV_WIDTH), jnp.float32)
        for l in range(CMP_STRIDE):
            xl = xt_sc[:, c, pl.ds(l, PAGE_CHUNKS, stride=CMP_STRIDE), :]
            xl = xl.reshape(cp * PAGE_CHUNKS, KV_WIDTH)
            lo = lo + jnp.dot((xl + pos_ref[c, l:l + 1]).astype(jnp.bfloat16), wlo_ref[c, l],
                              preferred_element_type=jnp.float32)
            hi = hi + jnp.dot((xl + pos_ref[c, CMP_STRIDE + l:CMP_STRIDE + l + 1]).astype(jnp.bfloat16), whi_ref[c, l],
                              preferred_element_type=jnp.float32)
        lo_ref[:, :, c * KV_WIDTH:(c + 1) * KV_WIDTH] = lo.reshape(cp, PAGE_CHUNKS, KV_WIDTH)
        hi_ref[:, :, c * KV_WIDTH:(c + 1) * KV_WIDTH] = hi.reshape(cp, PAGE_CHUNKS, KV_WIDTH)


def compress_pages(pages, cmp_pos, w_cmp):
    p = pages.shape[0]
    assert p % CMP_PAGES == 0
    pos = jnp.broadcast_to(jnp.transpose(cmp_pos, (1, 0, 2))[:, :, None, :],
                           (2, CMP_BLOCK, NSA_KV_HEADS, HEAD_DIM)).reshape(2, CMP_BLOCK, KV_WIDTH)
    eye = jnp.eye(NSA_KV_HEADS, dtype=jnp.float32)
    wbd = jnp.einsum('cldo,hH->clhdHo', w_cmp, eye).reshape(2, CMP_BLOCK, KV_WIDTH, KV_WIDTH).astype(jnp.bfloat16)
    out = jax.ShapeDtypeStruct((p, PAGE_CHUNKS, KV_ROW), jnp.float32)
    spec3 = lambda a, b_: pl.BlockSpec((CMP_PAGES, a, b_), lambda i: (i, 0, 0))
    w_spec = pl.BlockSpec((2, CMP_STRIDE, KV_WIDTH, KV_WIDTH), lambda i: (0, 0, 0, 0))
    return pl.pallas_call(
        _compress_body, grid=(p // CMP_PAGES,),
        in_specs=[pl.BlockSpec((CMP_PAGES, 2, NSA_KV_HEADS, HEAD_DIM, PAGE_SIZE), lambda i: (i, 0, 0, 0, 0)),
                  pl.BlockSpec((2, CMP_BLOCK, KV_WIDTH), lambda i: (0, 0, 0)), w_spec, w_spec],
        out_specs=[spec3(PAGE_CHUNKS, KV_ROW), spec3(PAGE_CHUNKS, KV_ROW)],
        out_shape=[out, out],
        scratch_shapes=[pltpu.VMEM((CMP_PAGES, 2, PAGE_SIZE, KV_WIDTH), jnp.float32)],
        compiler_params=pltpu.CompilerParams(dimension_semantics=("arbitrary",), vmem_limit_bytes=VMEM_LIMIT),
        name="cmp_compress",
    )(pages, pos, wbd[:, :CMP_STRIDE], wbd[:, CMP_STRIDE:])


def position_minor(kv):
    return jnp.transpose(kv, (0, 2, 3, 4, 1))


def compress_blocks(kv, cmp_pos, w_cmp):
    b, l = kv.shape[:2]
    n_pages = b * l // PAGE_SIZE
    pages = jnp.transpose(kv.reshape(n_pages, PAGE_SIZE, KV_ROW), (0, 2, 1))
    lo, hi = compress_pages(pages.reshape(n_pages, 2, NSA_KV_HEADS, HEAD_DIM, PAGE_SIZE), cmp_pos, w_cmp)
    lo = lo.reshape(b, l // CMP_STRIDE, 2, NSA_KV_HEADS, HEAD_DIM)
    hi = hi.reshape(b, l // CMP_STRIDE, 2, NSA_KV_HEADS, HEAD_DIM)
    return lo[:, :-1] + hi[:, 1:]


def cmp_branch(qg, t, summ):
    n = summ.shape[1]
    end = jnp.arange(n) * CMP_STRIDE + (CMP_BLOCK - 1)
    mask = end[None, :] <= t[:, None]
    s = jnp.einsum('bqhgd,bnhd->bhgqn', qg, summ[:, :, 0]) * (HEAD_DIM ** -0.5)
    p = masked_softmax(s, mask)
    o = jnp.einsum('bhgqn,bnhd->bqhgd', p.astype(qg.dtype), summ[:, :, 1])
    return o, jnp.sum(p, axis=2)


def select_blocks(p_cmp, t, n_sel):
    n_cmp = p_cmp.shape[-1]
    i = jnp.arange(n_cmp)[:, None]
    j = jnp.arange(n_sel)[None, :]
    overlap = jnp.clip(jnp.minimum(i * CMP_STRIDE + CMP_BLOCK, (j + 1) * SEL_BLOCK)
                       - jnp.maximum(i * CMP_STRIDE, j * SEL_BLOCK), 0, None)
    score = jnp.einsum('bhqn,nj->bhqj', p_cmp, overlap.astype(jnp.float32) / CMP_BLOCK)
    tb = (t // SEL_BLOCK)[:, None]
    forced = (j == 0) | (j == tb) | (j == tb - 1)
    score = jnp.where(forced, FORCE_SCORE, score)
    score = jnp.where(j <= tb, score, -1.0)
    top_s, idx = lax.top_k(score, min(SEL_TOPN, n_sel))
    return idx, top_s >= 0.0


def slc_branch(qg, t, kv_sel, idx, ok):
    b, h, q, n = idx.shape
    kpos = idx[..., None] * SEL_BLOCK + jnp.arange(SEL_BLOCK)
    mask = (ok[..., None] & (kpos <= t[None, None, :, None, None])).reshape(b, h, 1, q, n * SEL_BLOCK)
    s = jnp.einsum('bqhgd,bhqnld->bhgqnl', qg, kv_sel[..., 0, :]) * (HEAD_DIM ** -0.5)
    p = masked_softmax(s.reshape(b, h, NSA_GROUP, q, n * SEL_BLOCK), mask)
    v = kv_sel[..., 1, :].reshape(b, h, q, n * SEL_BLOCK, HEAD_DIM)
    return jnp.einsum('bhgqk,bhqkd->bqhgd', p.astype(qg.dtype), v)


def win_branch(qg, t, kv_w, kpos):
    d = t[:, None] - kpos[None, :]
    mask = (d >= 0) & (d < WINDOW) & (kpos[None, :] >= 0)
    s = jnp.einsum('bqhgd,bkhd->bhgqk', qg, kv_w[:, :, 0]) * (HEAD_DIM ** -0.5)
    p = masked_softmax(s, mask)
    return jnp.einsum('bhgqk,bkhd->bqhgd', p.astype(qg.dtype), kv_w[:, :, 1])


def gate_merge(o_c, o_s, o_w, g):
    o = o_c * g[..., 0:1] + o_s * g[..., 1:2] + o_w * g[..., 2:3]
    return o.reshape(o.shape[0], o.shape[1], NSA_WIDTH)


def nsa_sample(parts, cache_cmp, cache_slc, page_table, state_win, cmp_pos, w_cmp):
    b, n = parts[0].shape[:2]
    q, gates = parts[0], parts[4]
    kv_cmp, kv_slc, kv_win = (u.reshape(b, n, 2, NSA_KV_HEADS, HEAD_DIM) for u in parts[1:4])
    n_pages = page_table.shape[1]
    past = n_pages * PAGE_SIZE
    total = past + n
    t = past + jnp.arange(n)
    b_ix = jnp.arange(b)[:, None, None, None]
    h_ix = jnp.arange(NSA_KV_HEADS)[None, :, None, None]
    assert n <= CMP_STRIDE
    lo_pool, hi_pool = compress_pages(position_minor(cache_cmp), cmp_pos, w_cmp)
    as_chunks = lambda u: u[page_table].reshape(b, past // CMP_STRIDE, 2, NSA_KV_HEADS, HEAD_DIM)
    lo_past, hi_past = as_chunks(lo_pool), as_chunks(hi_pool)
    new_chunk = jnp.pad(kv_cmp, ((0, 0), (0, CMP_STRIDE - n), (0, 0), (0, 0), (0, 0)))
    hi_new = jnp.einsum('blchd,cldo->bcho', new_chunk + cmp_pos[None, CMP_STRIDE:, :, None, :], w_cmp[:, CMP_STRIDE:])
    summ = lo_past + jnp.concatenate([hi_past[:, 1:], hi_new[:, None]], axis=1)
    o_c, p_c = cmp_branch(q, t, summ)
    n_sel = -(-total // SEL_BLOCK)
    idx, ok = select_blocks(p_c, t, n_sel)
    past_blocks = past // SEL_BLOCK
    bpp = PAGE_SIZE // SEL_BLOCK
    phys = page_table[b_ix, jnp.minimum(idx // bpp, n_pages - 1)]
    pool = cache_slc.reshape(cache_slc.shape[0], bpp, SEL_BLOCK, 2, NSA_KV_HEADS, HEAD_DIM)
    from_past = pool[phys, idx % bpp, :, :, h_ix, :]
    n_new_blk = -(-n // SEL_BLOCK)
    new_blocks = jnp.pad(kv_slc, ((0, 0), (0, n_new_blk * SEL_BLOCK - n), (0, 0), (0, 0), (0, 0)))
    new_blocks = new_blocks.reshape(b, n_new_blk, SEL_BLOCK, 2, NSA_KV_HEADS, HEAD_DIM)
    from_new = new_blocks[b_ix, jnp.clip(idx - past_blocks, 0, n_new_blk - 1), :, :, h_ix, :]
    kv_sel = jnp.where((idx >= past_blocks)[..., None, None, None], from_new, from_past)
    o_s = slc_branch(q, t, kv_sel, idx, ok)
    wb = state_win.shape[1]
    kv_w = jnp.concatenate([state_win, kv_win], axis=1)
    o_w = win_branch(q, t, kv_w, past - wb + jnp.arange(wb + n))
    o = gate_merge(o_c, o_s, o_w, gates)
    return o, (kv_cmp, kv_slc, kv_w[:, n:])


PREP_ROWS = 512


def _head_sums(x):
    left = lax.broadcasted_iota(jnp.int32, (1, LANE), 1) < HEAD_DIM
    out = []
    for p in range(RWKV_PAIRS):
        xp = x[:, p * LANE:(p + 1) * LANE]
        s_l = jnp.sum(jnp.where(left, xp, 0.0), -1, keepdims=True)
        s_r = jnp.sum(jnp.where(left, 0.0, xp), -1, keepdims=True)
        out.append(jnp.where(left, s_l, s_r))
    return jnp.concatenate(out, axis=1)


def _rwkv_prep_body(z_ref, prev_ref, mu_ref, w0_ref, w2_ref, a0_ref, a2_ref, g2_ref, kk_ref, ka_ref,
                    r_out, k_out, v_out, nkk_out, dec_out, bb_out, g_out, last_sc):
    ci = pl.program_id(1)

    @pl.when(ci == 0)
    def _():
        last_sc[...] = prev_ref[0]

    zf = z_ref[0]
    first = lax.broadcasted_iota(jnp.int32, zf.shape, 0) == 0
    prev = jnp.where(first, last_sc[...], pltpu.roll(zf, 1, 0))
    last_sc[...] = zf[zf.shape[0] - 1:, :]
    zs = zf + mu_ref[...] * (prev - zf)
    r, k, v = (zs[:, i * RWKV_WIDTH:(i + 1) * RWKV_WIDTH] for i in range(3))
    wd = zs[:, RWKV_SPLITS[2]:RWKV_SPLITS[3]]
    ad = zs[:, RWKV_SPLITS[3]:RWKV_SPLITS[4]]
    gd = zs[:, RWKV_SPLITS[4]:]
    bdot = lambda u, w_ref: jnp.dot(u.astype(jnp.bfloat16), w_ref[...], preferred_element_type=jnp.float32)
    w = -jax.nn.softplus(-(w0_ref[...] + bdot(jnp.tanh(wd), w2_ref))) - 0.5
    a = jax.nn.sigmoid(a0_ref[...] + bdot(ad, a2_ref))
    kk = k * kk_ref[...]
    kk = kk / jnp.maximum(jnp.sqrt(_head_sums(kk * kk)), 1e-12)
    r_out[0] = r
    k_out[0] = k * (1.0 + (a - 1.0) * ka_ref[...])
    v_out[0] = v
    nkk_out[0] = -kk
    dec_out[0] = jnp.exp(-jnp.exp(w))
    bb_out[0] = kk * a
    g_out[0] = bdot(jax.nn.sigmoid(gd), g2_ref)


def rwkv_operands(z, prev_row, shift_mu, w0, w2, a0, a2, g2, k_k, k_a):
    b, l = z.shape[:2]
    rows = PREP_ROWS
    if l % rows:
        zf = z.astype(jnp.float32)
        prev = jnp.concatenate([prev_row[:, None].astype(jnp.float32), zf[:, :-1]], axis=1)
        zs = zf + shift_mu * (prev - zf)
        r, k, v, wd, ad, gd = jnp.split(zs, RWKV_SPLITS, axis=-1)
        w = -jax.nn.softplus(-(w0 + jnp.tanh(wd) @ w2)) - 0.5
        a = jax.nn.sigmoid(a0 + ad @ a2)
        g = jax.nn.sigmoid(gd) @ g2
        kk = (k * k_k).reshape(b, l, RWKV_HEADS, HEAD_DIM)
        kk = (kk / jnp.maximum(jnp.sqrt(jnp.sum(kk * kk, -1, keepdims=True)), 1e-12)).reshape(b, l, RWKV_WIDTH)
        return r, k * (1.0 + (a - 1.0) * k_a), v, -kk, jnp.exp(-jnp.exp(w)), kk * a, g
    row = lambda u: u.reshape(1, -1)
    const = lambda shape: pl.BlockSpec(shape, lambda bi, ci: (0,) * len(shape))
    seq = lambda width: pl.BlockSpec((1, rows, width), lambda bi, ci: (bi, ci, 0))
    out = jax.ShapeDtypeStruct((b, l, RWKV_WIDTH), jnp.float32)
    bf = lambda u: u.astype(jnp.bfloat16)
    return pl.pallas_call(
        _rwkv_prep_body, grid=(b, l // rows),
        in_specs=[seq(RWKV_COLS), pl.BlockSpec((1, 1, RWKV_COLS), lambda bi, ci: (bi, 0, 0)),
                  const((1, RWKV_COLS)), const((1, RWKV_WIDTH)), const(w2.shape), const((1, RWKV_WIDTH)),
                  const(a2.shape), const(g2.shape), const((1, RWKV_WIDTH)), const((1, RWKV_WIDTH))],
        out_specs=[seq(RWKV_WIDTH)] * 7, out_shape=[out] * 7,
        scratch_shapes=[pltpu.VMEM((1, RWKV_COLS), jnp.float32)],
        compiler_params=pltpu.CompilerParams(dimension_semantics=("arbitrary", "arbitrary"),
                                             vmem_limit_bytes=VMEM_LIMIT),
        name="rwkv_prep",
    )(z, prev_row.astype(jnp.float32).reshape(b, 1, RWKV_COLS), row(shift_mu), row(w0), bf(w2), row(a0), bf(a2), bf(g2),
      row(k_k), row(k_a))


def rwkv7_recurrence(z, prev_row, wkv0, shift_mu, w0, w2, a0, a2, g2, k_k, k_a):
    b, l = z.shape[:2]
    r, k, v, nkk, decay, bb, g = rwkv_operands(z, prev_row, shift_mu, w0, w2, a0, a2, g2, k_k, k_a)
    o, s_final = rwkv_scan(r, k, v, nkk, bb, decay, wkv0)
    return o.reshape(b, l, RWKV_WIDTH), (r, k, v, g), s_final, z[:, -1]


def rwkv7_output(o, rkvg, r_k, gn_g, gn_b):
    b, l = o.shape[:2]
    heads = lambda u: u.reshape(b, l, RWKV_HEADS, HEAD_DIM)
    r, k, v, g = rkvg
    o, r, k, v = heads(o), heads(r), heads(k), heads(v)
    mu = jnp.mean(o, -1, keepdims=True)
    var = jnp.mean(jnp.square(o - mu), -1, keepdims=True)
    o = ((o - mu) * lax.rsqrt(var + RWKV_GN_EPS)).reshape(b, l, RWKV_WIDTH) * gn_g + gn_b
    bonus = (jnp.sum(r * k * r_k, -1, keepdims=True) * v).reshape(b, l, RWKV_WIDTH)
    return (o + bonus) * g


def _mix_out_body(x_ref, oc_ref, os_ref, ow_ref, gl_ref, o_ref, r_ref, k_ref, v_ref, g_ref, rk_ref, gng_ref, gnb_ref,
                  w_ref, y_ref):
    acc = x_ref[0]
    gate = jax.nn.sigmoid(gl_ref[0])
    for h in range(NSA_KV_HEADS):
        for g in range(NSA_GROUP):
            hg = h * NSA_GROUP + g
            rows = slice(g * Q_BLOCK, (g + 1) * Q_BLOCK)
            merged = (oc_ref[0, h, 0, rows, :] * gate[:, 3 * hg:3 * hg + 1]
                      + os_ref[0, h, 0, rows, :] * gate[:, 3 * hg + 1:3 * hg + 2]
                      + ow_ref[0, h, 0, rows, :] * gate[:, 3 * hg + 2:3 * hg + 3])
            acc = acc + jnp.dot(merged.astype(jnp.bfloat16), w_ref[hg * HEAD_DIM:(hg + 1) * HEAD_DIM, :],
                                preferred_element_type=jnp.float32)
    o = o_ref[0]
    mu = _head_sums(o) * (1.0 / HEAD_DIM)
    var = _head_sums(jnp.square(o - mu)) * (1.0 / HEAD_DIM)
    o = (o - mu) * lax.rsqrt(var + RWKV_GN_EPS) * gng_ref[...] + gnb_ref[...]
    bonus = _head_sums(r_ref[0] * k_ref[0] * rk_ref[...]) * v_ref[0]
    mixed = (o + bonus) * g_ref[0]
    y_ref[0] = acc + jnp.dot(mixed.astype(jnp.bfloat16), w_ref[NSA_WIDTH:, :], preferred_element_type=jnp.float32)


def mix_out(x, nsa_tiles, gate_logits, o, rkvg, r_k, gn_g, gn_b, w_out):
    b, s, d = x.shape
    tile = pl.BlockSpec((1, NSA_KV_HEADS, 1, QROWS, HEAD_DIM), lambda bi, qi: (bi, 0, qi, 0, 0))
    seq = lambda width: pl.BlockSpec((1, Q_BLOCK, width), lambda bi, qi: (bi, qi, 0))
    const = lambda shape: pl.BlockSpec(shape, lambda bi, qi: (0,) * len(shape))
    row = lambda u: u.reshape(1, RWKV_WIDTH)
    return pl.pallas_call(
        _mix_out_body, grid=(b, s // Q_BLOCK),
        in_specs=[seq(d), tile, tile, tile, seq(LANE)] + [seq(RWKV_WIDTH)] * 5 + [const((1, RWKV_WIDTH))] * 3
                 + [const((MIX_WIDTH, d))],
        out_specs=seq(d), out_shape=jax.ShapeDtypeStruct((b, s, d), jnp.float32),
        compiler_params=pltpu.CompilerParams(dimension_semantics=("arbitrary", "arbitrary"),
                                             vmem_limit_bytes=VMEM_LIMIT),
        name="mix_out",
    )(x, *nsa_tiles, gate_logits, o, *rkvg, row(r_k), row(gn_g), row(gn_b), w_out.astype(jnp.bfloat16))


def moe_ffn(x, w_router, b_router, w_gate_up, b_gate_up, w_down, b_down):
    shape = x.shape
    xt = x.reshape(-1, shape[-1])
    n_tok = xt.shape[0]
    logits = (xt @ w_router).astype(jnp.float32) + b_router.astype(jnp.float32)
    top_v, top_e = lax.top_k(logits, TOP_K)
    gates = jax.nn.softmax(top_v, axis=-1)
    flat_e = top_e.reshape(-1)
    n_assign = n_tok * TOP_K
    order = jnp.argsort(flat_e)
    rank = jnp.argsort(order)
    counts = jnp.sum(flat_e[:, None] == jnp.arange(N_EXPERTS)[None, :], axis=0)
    rows = MOE_ROWS
    padded = (counts + rows - 1) // rows * rows
    pad_end = jnp.cumsum(padded)
    pad_start = pad_end - padded
    grp_start = jnp.cumsum(counts) - counts
    n_blocks = -(-n_assign // rows) + N_EXPERTS
    n_rows = n_blocks * rows
    blk_e = jnp.minimum(jnp.sum(pad_end[None, :] <= (jnp.arange(n_blocks) * rows)[:, None], axis=1), N_EXPERTS - 1)
    row_e = jnp.repeat(blk_e, rows)
    within = jnp.arange(n_rows) - pad_start[row_e]
    src = jnp.clip(grp_start[row_e] + within, 0, n_assign - 1)
    row_token = jnp.where(within < counts[row_e], order[src] // TOP_K, n_tok).astype(jnp.int32)
    x_pad = jnp.concatenate([xt.astype(jnp.bfloat16), jnp.zeros((1, shape[-1]), jnp.bfloat16)], axis=0)
    xb = x_pad[row_token]
    n_used = (pad_end[-1] // rows).astype(jnp.int32).reshape(1)
    yb = expert_blocks(xb, blk_e.astype(jnp.int32), n_used, w_gate_up, b_gate_up, w_down, b_down)
    dest = (pad_start[flat_e] + rank - grp_start[flat_e]).reshape(n_tok, TOP_K)
    y = gates[:, 0:1] * yb[dest[:, 0]]
    for j in range(1, TOP_K):
        y = y + gates[:, j:j + 1] * yb[dest[:, j]]
    return y.reshape(shape)


def kernel(x_prompt, x_sample, cache_cmp, cache_slc, page_table, state_win, state_wkv, state_shift,
           norm_mix_g, w_in, cmp_pos, w_cmp, shift_mu, w0, w2, a0, a2, g2, k_k, k_a, r_k, gn_g, gn_b,
           w_out, norm_ffn_g, w_router, b_router, w_gate_up, b_gate_up, w_down, b_down, norm_final_g):
    xp, xs = x_prompt, x_sample
    l = 0
    rw = (shift_mu[l], w0[l], w2[l], a0[l], a2[l], g2[l], k_k[l], k_a[l], r_k[l], gn_g[l], gn_b[l])
    ffn = (w_router[l], b_router[l], w_gate_up[l], b_gate_up[l], w_down[l], b_down[l])
    bp, sp = xp.shape[:2]
    bs, ss = xs.shape[:2]
    rw_in, rw_out = rw[:8], rw[8:]
    nsa_p, zr_p, gl_p = in_proj(xp, norm_mix_g[l], w_in[l])
    tiles, (kc_p, ksl_p, kw_p) = nsa_prompt(nsa_p, cmp_pos[l], w_cmp[l])
    o_p, rkvg_p, wkv_p, sh_p = rwkv7_recurrence(zr_p, jnp.zeros((bp, RWKV_COLS), zr_p.dtype),
                                                jnp.zeros((bp, RWKV_HEADS, HEAD_DIM, HEAD_DIM), jnp.float32), *rw_in)
    xp = mix_out(xp, tiles, gl_p, o_p, rkvg_p, *rw_out, w_out[l])

    nsa_s, zr_s, _ = in_proj(xs, norm_mix_g[l], w_in[l])
    oa, (kc_s, ksl_s, kw_s) = nsa_sample(nsa_s, cache_cmp[l], cache_slc[l], page_table, state_win[l],
                                         cmp_pos[l], w_cmp[l])
    o_s, rkvg_s, wkv_s, sh_s = rwkv7_recurrence(zr_s, state_shift[l], state_wkv[l], *rw_in)
    ob = rwkv7_output(o_s, rkvg_s, *rw_out)
    xs = xs + jnp.concatenate([oa, ob], axis=-1) @ w_out[l]
    x_all = jnp.concatenate([xp.reshape(bp * sp, D_MODEL), xs.reshape(bs * ss, D_MODEL)], axis=0)
    m_all = moe_ffn(rms_norm(x_all, norm_ffn_g[l]), *ffn)
    xp = xp + m_all[:bp * sp].reshape(xp.shape)
    xs = xs + m_all[bp * sp:].reshape(xs.shape)

    y_prompt = rms_norm(xp, norm_final_g)
    y_sample = rms_norm(xs, norm_final_g)
    st = lambda u: u[None]
    return (y_prompt, y_sample, st(kc_p), st(ksl_p), st(kw_p), st(wkv_p), st(sh_p),
            st(kc_s), st(ksl_s), st(kw_s), st(wkv_s), st(sh_s))
```

```python
import functools
import math

import jax
import jax.numpy as jnp
from jax import lax
from jax.experimental import pallas as pl
from jax.experimental.pallas import tpu as pltpu

D_MODEL = 1024
DEPTH = 1
PAGE_SIZE = 128

HEAD_DIM = 64
NSA_HEADS = 8
NSA_KV_HEADS = 2
NSA_GROUP = NSA_HEADS // NSA_KV_HEADS
CMP_BLOCK = 32
CMP_STRIDE = 16
SEL_BLOCK = 64
SEL_TOPN = 16
WINDOW = 512
Q_BLOCK = 128
FORCE_SCORE = 1e6
RWKV_HEADS = 8
RWKV_DECAY_LORA = 64
RWKV_A_LORA = 64
RWKV_GATE_LORA = 128
RWKV_GN_EPS = 64e-5
N_EXPERTS = 32
TOP_K = 4
D_FF = 1024
SWIGLU_LIMIT = 7.0
SWIGLU_ALPHA = 1.702
EXPERT_BLOCK = 128
RMS_EPS = 1e-5

NSA_WIDTH = NSA_HEADS * HEAD_DIM
KV_WIDTH = NSA_KV_HEADS * HEAD_DIM
RWKV_WIDTH = RWKV_HEADS * HEAD_DIM
MIX_WIDTH = NSA_WIDTH + RWKV_WIDTH
NSA_COLS = NSA_WIDTH + 6 * KV_WIDTH + 3 * NSA_HEADS
RWKV_COLS = 3 * RWKV_WIDTH + RWKV_DECAY_LORA + RWKV_A_LORA + RWKV_GATE_LORA
IN_COLS = NSA_COLS + RWKV_COLS
RWKV_SPLITS = (RWKV_WIDTH, 2 * RWKV_WIDTH, 3 * RWKV_WIDTH, 3 * RWKV_WIDTH + RWKV_DECAY_LORA,
               3 * RWKV_WIDTH + RWKV_DECAY_LORA + RWKV_A_LORA)

LANE = 128
VMEM_LIMIT = 48 * 1024 * 1024


GATE_COLS = 3 * NSA_HEADS
PROJ_GROUPS = (NSA_WIDTH, 2 * KV_WIDTH, 2 * KV_WIDTH, 2 * KV_WIDTH, RWKV_COLS, LANE)


def _in_proj_body(x_ref, g_ref, w_ref, *o_refs):
    x = x_ref[...]
    y = x * lax.rsqrt(jnp.mean(x * x, -1, keepdims=True) + RMS_EPS) * g_ref[...]
    z = jnp.dot(y.astype(jnp.bfloat16), w_ref[...], preferred_element_type=jnp.float32)
    c0 = 0
    for o_ref, width in zip(o_refs, PROJ_GROUPS):
        o_ref[...] = z[:, c0:c0 + width]
        c0 += width


def in_proj(x, g, w, block_rows=512):
    b, l, d = x.shape
    n = b * l
    gate0 = NSA_WIDTH + 6 * KV_WIDTH
    wb = jnp.concatenate([w[:, :gate0], w[:, NSA_COLS:], w[:, gate0:NSA_COLS],
                          jnp.zeros((d, LANE - GATE_COLS), w.dtype)], axis=1).astype(jnp.bfloat16)
    cols = sum(PROJ_GROUPS)
    outs = pl.pallas_call(
        _in_proj_body,
        grid=(n // block_rows,),
        in_specs=[pl.BlockSpec((block_rows, d), lambda i: (i, 0)),
                  pl.BlockSpec((1, d), lambda i: (0, 0)),
                  pl.BlockSpec((d, cols), lambda i: (0, 0))],
        out_specs=[pl.BlockSpec((block_rows, width), lambda i: (i, 0)) for width in PROJ_GROUPS],
        out_shape=[jax.ShapeDtypeStruct((n, width), jnp.float32) for width in PROJ_GROUPS],
        compiler_params=pltpu.CompilerParams(dimension_semantics=("arbitrary",), vmem_limit_bytes=VMEM_LIMIT),
        name="in_proj",
    )(x.reshape(n, d), g.reshape(1, d), wb)
    q = outs[0].reshape(b, l, NSA_KV_HEADS, NSA_GROUP, HEAD_DIM)
    kv = [o.reshape(b, l, KV_ROW) for o in outs[1:4]]
    gates = jax.nn.sigmoid(outs[5][:, :GATE_COLS]).reshape(b, l, NSA_KV_HEADS, NSA_GROUP, 3)
    return (q, kv[0], kv[1], kv[2], gates), outs[4].reshape(b, l, RWKV_COLS), outs[5].reshape(b, l, LANE)


QROWS = NSA_GROUP * Q_BLOCK
SLC_TK = 512
MASK_BIAS = -float(2 ** 30)
NEG_INIT = -3.0e38
V_AUG = 2 * HEAD_DIM

_NT = (((1,), (1,)), ((), ()))


def _cmp_select_body(q_ref, kc_ref, vc_ref, ov_ref, oc_ref, bias_ref):
    qb = pl.program_id(1)
    ncp = kc_ref.shape[2]
    nsel = ov_ref.shape[1]
    row = lax.broadcasted_iota(jnp.int32, (QROWS, ncp), 0)
    t = qb * Q_BLOCK + (row & (Q_BLOCK - 1))
    n = lax.broadcasted_iota(jnp.int32, (QROWS, ncp), 1)
    mask = n * CMP_STRIDE + (CMP_BLOCK - 1) <= t
    scores = []
    for h in range(NSA_KV_HEADS):
        q = q_ref[0, h, 0]
        s = lax.dot_general(q, kc_ref[0, h], _NT, preferred_element_type=jnp.float32)
        s = jnp.where(mask, s, -jnp.inf)
        m = jnp.max(s, -1, keepdims=True)
        m = jnp.where(m == -jnp.inf, 0.0, m)
        e = jnp.exp(s - m)
        p = e / jnp.maximum(jnp.sum(e, -1, keepdims=True), 1e-30)
        oc_ref[0, h, 0] = jnp.dot(p.astype(jnp.bfloat16), vc_ref[0, h], preferred_element_type=jnp.float32)
        pc = p[0:Q_BLOCK]
        for g in range(1, NSA_GROUP):
            pc = pc + p[g * Q_BLOCK:(g + 1) * Q_BLOCK]
        scores.append(jnp.dot(pc.astype(jnp.bfloat16), ov_ref[...], preferred_element_type=jnp.float32))
    rows = NSA_KV_HEADS * Q_BLOCK
    score = jnp.concatenate(scores, axis=0)
    j = lax.broadcasted_iota(jnp.int32, (rows, nsel), 1)
    tq = qb * Q_BLOCK + (lax.broadcasted_iota(jnp.int32, (rows, nsel), 0) & (Q_BLOCK - 1))
    tb = tq >> (SEL_BLOCK.bit_length() - 1)
    forced = (j == 0) | (j == tb) | (j == tb - 1)
    score = jnp.where(forced, FORCE_SCORE, score)
    score = jnp.where(j <= tb, score, -1.0)
    jf = j.astype(jnp.float32)
    sel = jnp.zeros((rows, nsel), jnp.bool_)
    for _ in range(min(SEL_TOPN, nsel)):
        top = jnp.max(score, -1, keepdims=True)
        first = jnp.min(jnp.where(score == top, jf, float(nsel)), -1, keepdims=True)
        pick = jf == first
        sel = sel | pick
        score = jnp.where(pick, -2.0, score)
    bias = jnp.where(sel, 0.0, MASK_BIAS).astype(jnp.bfloat16)
    for h in range(NSA_KV_HEADS):
        bias_ref[0, h] = bias[h * Q_BLOCK:(h + 1) * Q_BLOCK]


def _slc_body(q_ref, bias_ref, kaug_ref, vaug_ref, o_ref, qaug_sc, m_sc, acc_sc):
    qb = pl.program_id(2)
    start = qb * Q_BLOCK
    nsel = bias_ref.shape[3]
    bias = bias_ref[0, 0]
    for g in range(NSA_GROUP):
        qaug_sc[g * Q_BLOCK:(g + 1) * Q_BLOCK, 0:nsel] = bias
    qaug_sc[:, nsel:nsel + HEAD_DIM] = q_ref[0, 0, 0]
    m_sc[...] = jnp.full(m_sc.shape, NEG_INIT, jnp.float32)
    acc_sc[...] = jnp.zeros(acc_sc.shape, jnp.float32)

    def scores(kt):
        k0 = pl.multiple_of(kt * SLC_TK, SLC_TK)
        return lax.dot_general(qaug_sc[...], kaug_ref[0, 0, pl.ds(k0, SLC_TK), :], _NT,
                               preferred_element_type=jnp.float32)

    def absorb(kt, s, causal):
        k0 = pl.multiple_of(kt * SLC_TK, SLC_TK)
        slabs = [s[:, c * LANE:(c + 1) * LANE] for c in range(SLC_TK // LANE)]
        if causal:
            row = lax.broadcasted_iota(jnp.int32, (QROWS, LANE), 0)
            col = lax.broadcasted_iota(jnp.int32, (QROWS, LANE), 1)
            t_rel = start + (row & (Q_BLOCK - 1)) - k0 - col
            slabs = [jnp.where(t_rel >= c * LANE, sl, -jnp.inf) for c, sl in enumerate(slabs)]
        top = slabs[0]
        for sl in slabs[1:]:
            top = jnp.maximum(top, sl)
        m_prev = m_sc[...]
        m_new = jnp.maximum(m_prev, jnp.max(top, -1, keepdims=True))
        p = jnp.concatenate([jnp.exp(sl - m_new) for sl in slabs], axis=1).astype(jnp.bfloat16)
        acc_sc[...] = jnp.exp(m_prev - m_new) * acc_sc[...] + jnp.dot(
            p, vaug_ref[0, 0, pl.ds(k0, SLC_TK), :], preferred_element_type=jnp.float32)
        m_sc[...] = m_new

    n_full = start // SLC_TK

    def body(kt, s_cur):
        s_next = scores(kt + 1)
        absorb(kt, s_cur, False)
        return s_next

    s_last = lax.fori_loop(0, n_full, body, scores(0))
    absorb(n_full, s_last, True)
    acc = acc_sc[...]
    o_ref[0, 0, 0] = acc[:, :HEAD_DIM] / jnp.maximum(acc[:, HEAD_DIM:HEAD_DIM + 1], 1e-30)


def _win_body(q_ref, kpad_ref, vpad_ref, o_ref):
    qb = pl.program_id(2)
    start = pl.multiple_of(qb * Q_BLOCK, Q_BLOCK)
    nk = Q_BLOCK + WINDOW
    k = kpad_ref[0, 0, pl.ds(start, nk), :]
    s = lax.dot_general(q_ref[0, 0, 0], k, _NT, preferred_element_type=jnp.float32)
    r = lax.broadcasted_iota(jnp.int32, (QROWS, nk), 0) & (Q_BLOCK - 1)
    c = lax.broadcasted_iota(jnp.int32, (QROWS, nk), 1)
    d = r + WINDOW - c
    mask = (d >= 0) & (d < WINDOW) & (c >= WINDOW - start)
    s = jnp.where(mask, s, -jnp.inf)
    m = jnp.max(s, -1, keepdims=True)
    m = jnp.where(m == -jnp.inf, 0.0, m)
    p = jnp.exp(s - m)
    pv = jnp.dot(p.astype(jnp.bfloat16), vpad_ref[0, 0, pl.ds(start, nk), :], preferred_element_type=jnp.float32)
    o_ref[0, 0, 0] = pv[:, :HEAD_DIM] / jnp.maximum(pv[:, HEAD_DIM:HEAD_DIM + 1], 1e-30)


def _to_tiles(u, scale=None):
    b, s = u.shape[:2]
    u = u.reshape(b, s // Q_BLOCK, Q_BLOCK, NSA_KV_HEADS, NSA_GROUP, HEAD_DIM)
    u = jnp.transpose(u, (0, 3, 1, 4, 2, 5)).reshape(b, NSA_KV_HEADS, s // Q_BLOCK, QROWS, HEAD_DIM)
    return u if scale is None else (u * scale).astype(jnp.bfloat16)


def _head_major(kv):
    return jnp.transpose(kv, (0, 2, 1, 3))


def _with_ones(v):
    ones = jnp.ones(v.shape[:-1] + (1,), v.dtype)
    zeros = jnp.zeros(v.shape[:-1] + (V_AUG - HEAD_DIM - 1,), v.dtype)
    return jnp.concatenate([v, ones, zeros], -1).astype(jnp.bfloat16)


def _split_heads(kv_rows):
    piece = lambda c: jnp.stack([kv_rows[..., (c * NSA_KV_HEADS + h) * HEAD_DIM:(c * NSA_KV_HEADS + h + 1) * HEAD_DIM]
                                 for h in range(NSA_KV_HEADS)], axis=1)
    return piece(0), piece(1)


def nsa_prompt(parts, cmp_pos, w_cmp):
    q, kv_cmp, kv_slc, kv_win, gates = parts
    b, s = q.shape[:2]
    nq = s // Q_BLOCK
    nsel = s // SEL_BLOCK
    assert s % SLC_TK == 0 and nsel % LANE == 0
    summ = compress_blocks(kv_cmp, cmp_pos, w_cmp)
    n_cmp = summ.shape[1]
    ncp = -(-n_cmp // LANE) * LANE
    summ = jnp.pad(summ, ((0, 0), (0, ncp - n_cmp), (0, 0), (0, 0), (0, 0))).astype(jnp.bfloat16)
    kc, vc = _head_major(summ[:, :, 0]), _head_major(summ[:, :, 1])
    i = jnp.arange(ncp)[:, None]
    jj = jnp.arange(nsel)[None, :]
    overlap = jnp.clip(jnp.minimum(i * CMP_STRIDE + CMP_BLOCK, (jj + 1) * SEL_BLOCK)
                       - jnp.maximum(i * CMP_STRIDE, jj * SEL_BLOCK), 0, None)
    ov = (overlap.astype(jnp.float32) / CMP_BLOCK).astype(jnp.bfloat16)
    qt = _to_tiles(q, HEAD_DIM ** -0.5)

    grid = (b, NSA_KV_HEADS, nq)
    params = pltpu.CompilerParams(dimension_semantics=("arbitrary",) * 3, vmem_limit_bytes=VMEM_LIMIT)
    q_spec = pl.BlockSpec((1, 1, 1, QROWS, HEAD_DIM), lambda bi, h, qi: (bi, h, qi, 0, 0))
    o_spec = pl.BlockSpec((1, 1, 1, QROWS, HEAD_DIM), lambda bi, h, qi: (bi, h, qi, 0, 0))
    o_shape = jax.ShapeDtypeStruct((b, NSA_KV_HEADS, nq, QROWS, HEAD_DIM), jnp.float32)
    seq_spec = lambda rows, width: pl.BlockSpec((1, 1, rows, width), lambda bi, h, qi: (bi, h, 0, 0))
    bias_spec = pl.BlockSpec((1, 1, Q_BLOCK, nsel), lambda bi, h, qi: (bi, h, qi, 0))

    kvh = NSA_KV_HEADS
    both_q = pl.BlockSpec((1, kvh, 1, QROWS, HEAD_DIM), lambda bi, qi: (bi, 0, qi, 0, 0))
    both_seq = pl.BlockSpec((1, kvh, ncp, HEAD_DIM), lambda bi, qi: (bi, 0, 0, 0))
    o_c, bias = pl.pallas_call(
        _cmp_select_body, grid=(b, nq),
        in_specs=[both_q, both_seq, both_seq, pl.BlockSpec((ncp, nsel), lambda bi, qi: (0, 0))],
        out_specs=[both_q, pl.BlockSpec((1, kvh, Q_BLOCK, nsel), lambda bi, qi: (bi, 0, qi, 0))],
        out_shape=[o_shape, jax.ShapeDtypeStruct((b, kvh, s, nsel), jnp.bfloat16)],
        compiler_params=pltpu.CompilerParams(dimension_semantics=("arbitrary",) * 2, vmem_limit_bytes=VMEM_LIMIT),
        name="nsa_cmp_select",
    )(qt, kc, vc, ov)

    k_s, v_s = _split_heads(kv_slc)
    onehot = (jnp.arange(s)[:, None] // SEL_BLOCK == jnp.arange(nsel)[None, :]).astype(jnp.bfloat16)
    kaug = jnp.concatenate([jnp.broadcast_to(onehot, (b, NSA_KV_HEADS, s, nsel)), k_s.astype(jnp.bfloat16)], -1)
    o_s = pl.pallas_call(
        _slc_body, grid=grid,
        in_specs=[q_spec, bias_spec, seq_spec(s, nsel + HEAD_DIM), seq_spec(s, V_AUG)],
        out_specs=o_spec, out_shape=o_shape,
        scratch_shapes=[pltpu.VMEM((QROWS, nsel + HEAD_DIM), jnp.bfloat16),
                        pltpu.VMEM((QROWS, LANE), jnp.float32),
                        pltpu.VMEM((QROWS, V_AUG), jnp.float32)],
        compiler_params=params, name="nsa_slc",
    )(qt, bias, kaug, _with_ones(v_s))

    front = ((0, 0), (0, 0), (WINDOW, 0), (0, 0))
    k_w, v_w = _split_heads(kv_win)
    k_w = jnp.pad(k_w.astype(jnp.bfloat16), front)
    v_w = jnp.pad(_with_ones(v_w), front)
    o_w = pl.pallas_call(
        _win_body, grid=grid,
        in_specs=[q_spec, seq_spec(s + WINDOW, HEAD_DIM), seq_spec(s + WINDOW, V_AUG)],
        out_specs=o_spec, out_shape=o_shape,
        compiler_params=params, name="nsa_win",
    )(qt, k_w, v_w)

    as_rows = lambda u: u.reshape(u.shape[0], u.shape[1], 2, NSA_KV_HEADS, HEAD_DIM)
    return (o_c, o_s, o_w), (as_rows(kv_cmp), as_rows(kv_slc), as_rows(kv_win[:, s - min(WINDOW, s):]))


RWKV_PAIRS = RWKV_HEADS // 2
RWKV_TC = 256
RWKV_NB = 4
RWKV_GROUP = 8
V_PIECES = 4
V_STEPS = LANE // (V_PIECES * RWKV_HEADS)


def _rwkv_body(nkk_ref, w_ref, bb_ref, k_ref, r_ref, vt_ref, e2_ref, ebd_ref, s0_ref, o_ref, sT_ref, s_sc):
    nb, tc = nkk_ref.shape[0], nkk_ref.shape[1]
    ci = pl.program_id(1)

    @pl.when(ci == 0)
    def _():
        s_sc[...] = s0_ref[...]

    lane = lax.broadcasted_iota(jnp.int32, (1, LANE), 1)
    left = lane < HEAD_DIM
    vlane = lax.broadcasted_iota(jnp.int32, (HEAD_DIM, LANE), 1)
    vstep = (vlane >> (RWKV_HEADS.bit_length() - 1)) & (V_STEPS - 1)

    def one_step(b, t0, i, rows):
        nkk, w, bb, k, r = rows
        t = t0 + i
        row = lambda u, p: u[i:i + 1, p * LANE:(p + 1) * LANE]
        vm = jnp.where(vstep == i % V_STEPS, vt_ref[b, t0 // V_STEPS + i // V_STEPS], jnp.zeros((), jnp.bfloat16))
        vb = jnp.dot(vm, e2_ref[...], preferred_element_type=jnp.float32)
        rbd = (ebd_ref[...] * r[i:i + 1, :]).astype(jnp.bfloat16)
        o = jnp.zeros((RWKV_HEADS, HEAD_DIM), jnp.float32)
        for p in range(RWKV_PAIRS):
            s = s_sc[b, p]
            c = row(nkk, p)
            c_l = jnp.where(left, c, 0.0)
            c_r = jnp.where(left, 0.0, c)
            sa = jnp.where(left, jnp.sum(s * c_l, -1, keepdims=True), jnp.sum(s * c_r, -1, keepdims=True))
            s = s * row(w, p) + sa * row(bb, p) + vb[:, p * LANE:(p + 1) * LANE] * row(k, p)
            s_sc[b, p] = s
            o = o + lax.dot_general(rbd[:, p * LANE:(p + 1) * LANE], s.astype(jnp.bfloat16), _NT,
                                    preferred_element_type=jnp.float32)
        o_ref[b, pl.ds(pl.multiple_of(t * RWKV_HEADS, RWKV_HEADS), RWKV_HEADS), :] = o

    group = min(tc, RWKV_GROUP)

    def group_steps(gi, carry):
        t0 = pl.multiple_of(gi * group, group)
        rows = [tuple(ref[b, pl.ds(t0, group), :] for ref in (nkk_ref, w_ref, bb_ref, k_ref, r_ref)) for b in range(nb)]
        for i in range(group):
            for b in range(nb):
                one_step(b, t0, i, rows[b])
        return carry

    lax.fori_loop(0, tc // group, group_steps, 0)
    sT_ref[...] = s_sc[...]


def rwkv_scan(r, k, v, nkk, bb, decay, wkv0):
    b, l = r.shape[:2]
    tc = min(l, RWKV_TC)
    nb = RWKV_NB
    assert l % tc == 0 and tc % V_STEPS == 0 and b % nb == 0
    vh = v.reshape(b, l, RWKV_HEADS, HEAD_DIM)
    v1 = vh.astype(jnp.bfloat16)
    res = vh - v1.astype(jnp.float32)
    v2 = res.astype(jnp.bfloat16)
    v3 = (res - v2.astype(jnp.float32)).astype(jnp.bfloat16)
    pieces = jnp.stack([v1, v2, v3, jnp.zeros_like(v1)], 0)
    pieces = pieces.reshape(V_PIECES, b, l // V_STEPS, V_STEPS, RWKV_HEADS, HEAD_DIM)
    vt = jnp.transpose(pieces, (1, 2, 5, 0, 3, 4)).reshape(b, l // V_STEPS, HEAD_DIM, LANE)
    lane_head = jnp.arange(LANE) % RWKV_HEADS
    col_head = jnp.arange(RWKV_WIDTH) // HEAD_DIM
    e2 = (lane_head[:, None] == col_head[None, :]).astype(jnp.bfloat16)
    ebd = (jnp.arange(RWKV_HEADS)[:, None] == col_head[None, :]).astype(jnp.float32)
    to_pairs = lambda s: jnp.transpose(s.reshape(b, RWKV_PAIRS, 2, HEAD_DIM, HEAD_DIM), (0, 1, 3, 2, 4)
                                       ).reshape(b, RWKV_PAIRS, HEAD_DIM, LANE)
    s0 = to_pairs(wkv0.astype(jnp.float32))

    row_spec = pl.BlockSpec((nb, tc, RWKV_WIDTH), lambda bi, ci: (bi, ci, 0))
    state_spec = pl.BlockSpec((nb, RWKV_PAIRS, HEAD_DIM, LANE), lambda bi, ci: (bi, 0, 0, 0))
    o, s_t = pl.pallas_call(
        _rwkv_body, grid=(b // nb, l // tc),
        in_specs=[row_spec] * 5 + [
            pl.BlockSpec((nb, tc // V_STEPS, HEAD_DIM, LANE), lambda bi, ci: (bi, ci, 0, 0)),
            pl.BlockSpec((LANE, RWKV_WIDTH), lambda bi, ci: (0, 0)),
            pl.BlockSpec((RWKV_HEADS, RWKV_WIDTH), lambda bi, ci: (0, 0)),
            state_spec],
        out_specs=[pl.BlockSpec((nb, tc * RWKV_HEADS, HEAD_DIM), lambda bi, ci: (bi, ci, 0)), state_spec],
        out_shape=[jax.ShapeDtypeStruct((b, l * RWKV_HEADS, HEAD_DIM), jnp.float32),
                   jax.ShapeDtypeStruct((b, RWKV_PAIRS, HEAD_DIM, LANE), jnp.float32)],
        scratch_shapes=[pltpu.VMEM((nb, RWKV_PAIRS, HEAD_DIM, LANE), jnp.float32)],
        compiler_params=pltpu.CompilerParams(dimension_semantics=("arbitrary", "arbitrary"),
                                             vmem_limit_bytes=VMEM_LIMIT),
        name="rwkv_scan",
    )(nkk, decay, bb, k, r, vt, e2, ebd, s0)
    s_t = jnp.transpose(s_t.reshape(b, RWKV_PAIRS, HEAD_DIM, 2, HEAD_DIM), (0, 1, 3, 2, 4))
    return o.reshape(b, l, RWKV_HEADS, HEAD_DIM), s_t.reshape(b, RWKV_HEADS, HEAD_DIM, HEAD_DIM)


MOE_ROWS = 512


def _expert_body(blk_e_ref, n_used_ref, x_ref, wgu_ref, bgu_ref, wd_ref, bd_ref, o_ref):
    i = pl.program_id(0)

    @pl.when(i < n_used_ref[0])
    def _():
        hgu = jnp.dot(x_ref[...], wgu_ref[0], preferred_element_type=jnp.float32) + bgu_ref[0]
        gate = jnp.minimum(hgu[:, :D_FF], SWIGLU_LIMIT)
        up = jnp.clip(hgu[:, D_FF:], -SWIGLU_LIMIT, SWIGLU_LIMIT)
        act = (up + 1.0) * gate * jax.nn.sigmoid(SWIGLU_ALPHA * gate)
        o_ref[...] = jnp.dot(act.astype(jnp.bfloat16), wd_ref[0], preferred_element_type=jnp.float32) + bd_ref[0]

    @pl.when(i >= n_used_ref[0])
    def _():
        o_ref[...] = jnp.zeros(o_ref.shape, o_ref.dtype)


def expert_blocks(xb, blk_e, n_used, w_gate_up, b_gate_up, w_down, b_down):
    n_rows, d = xb.shape
    n_blocks = n_rows // MOE_ROWS
    wgu = w_gate_up.astype(jnp.bfloat16)
    wd = w_down.astype(jnp.bfloat16)
    grid_spec = pltpu.PrefetchScalarGridSpec(
        num_scalar_prefetch=2, grid=(n_blocks,),
        in_specs=[pl.BlockSpec((MOE_ROWS, d), lambda i, e, n: (i, 0)),
                  pl.BlockSpec((1, d, 2 * D_FF), lambda i, e, n: (e[i], 0, 0)),
                  pl.BlockSpec((1, 1, 2 * D_FF), lambda i, e, n: (e[i], 0, 0)),
                  pl.BlockSpec((1, D_FF, d), lambda i, e, n: (e[i], 0, 0)),
                  pl.BlockSpec((1, 1, d), lambda i, e, n: (e[i], 0, 0))],
        out_specs=pl.BlockSpec((MOE_ROWS, d), lambda i, e, n: (i, 0)))
    return pl.pallas_call(
        _expert_body, grid_spec=grid_spec,
        out_shape=jax.ShapeDtypeStruct((n_rows, d), jnp.float32),
        compiler_params=pltpu.CompilerParams(dimension_semantics=("arbitrary",), vmem_limit_bytes=VMEM_LIMIT),
        name="moe_experts",
    )(blk_e, n_used, xb, wgu, b_gate_up.reshape(N_EXPERTS, 1, 2 * D_FF), wd, b_down.reshape(N_EXPERTS, 1, d))


def rms_norm(x, g):
    xf = x.astype(jnp.float32)
    y = xf * lax.rsqrt(jnp.mean(xf * xf, -1, keepdims=True) + RMS_EPS)
    return (y * g.astype(jnp.float32)).astype(x.dtype)


def masked_softmax(s, mask):
    s = jnp.where(mask, s.astype(jnp.float32), -jnp.inf)
    m = jnp.max(s, -1, keepdims=True)
    m = jnp.where(jnp.isfinite(m), m, 0.0)
    e = jnp.where(mask, jnp.exp(s - m), 0.0)
    return e / jnp.maximum(jnp.sum(e, -1, keepdims=True), 1e-30)


CMP_PAGES = 32
PAGE_CHUNKS = PAGE_SIZE // CMP_STRIDE
KV_ROW = 2 * KV_WIDTH


def _compress_body(x_ref, pos_ref, wlo_ref, whi_ref, lo_ref, hi_ref, xt_sc):
    cp = x_ref.shape[0]

    def to_rows(pi, carry):
        for c in range(2):
            xt_sc[pi, c] = x_ref[pi, c].reshape(KV_WIDTH, PAGE_SIZE).T
        return carry

    lax.fori_loop(0, cp, to_rows, 0, unroll=4)
    for c in range(2):
        lo = jnp.zeros((cp * PAGE_CHUNKS, KV_WIDTH), jnp.float32)
        hi = jnp.zeros((cp * PAGE_CHUNKS, KV_WIDTH), jnp.float32)
        for l in range(CMP_STRIDE):
            xl = xt_sc[:, c, pl.ds(l, PAGE_CHUNKS, stride=CMP_STRIDE), :]
            xl = xl.reshape(cp * PAGE_CHUNKS, KV_WIDTH)
            lo = lo + jnp.dot((xl + pos_ref[c, l:l + 1]).astype(jnp.bfloat16), wlo_ref[c, l],
                              preferred_element_type=jnp.float32)
            hi = hi + jnp.dot((xl + pos_ref[c, CMP_STRIDE + l:CMP_STRIDE + l + 1]).astype(jnp.bfloat16), whi_ref[c, l],
                              preferred_element_type=jnp.float32)
        lo_ref[:, :, c * KV_WIDTH:(c + 1) * KV_WIDTH] = lo.reshape(cp, PAGE_CHUNKS, KV_WIDTH)
        hi_ref[:, :, c * KV_WIDTH:(c + 1) * KV_WIDTH] = hi.reshape(cp, PAGE_CHUNKS, KV_WIDTH)


def compress_pages(pages, cmp_pos, w_cmp):
    p = pages.shape[0]
    assert p % CMP_PAGES == 0
    pos = jnp.broadcast_to(jnp.transpose(cmp_pos, (1, 0, 2))[:, :, None, :],
                           (2, CMP_BLOCK, NSA_KV_HEADS, HEAD_DIM)).reshape(2, CMP_BLOCK, KV_WIDTH)
    eye = jnp.eye(NSA_KV_HEADS, dtype=jnp.float32)
    wbd = jnp.einsum('cldo,hH->clhdHo', w_cmp, eye).reshape(2, CMP_BLOCK, KV_WIDTH, KV_WIDTH).astype(jnp.bfloat16)
    out = jax.ShapeDtypeStruct((p, PAGE_CHUNKS, KV_ROW), jnp.float32)
    spec3 = lambda a, b_: pl.BlockSpec((CMP_PAGES, a, b_), lambda i: (i, 0, 0))
    w_spec = pl.BlockSpec((2, CMP_STRIDE, KV_WIDTH, KV_WIDTH), lambda i: (0, 0, 0, 0))
    return pl.pallas_call(
        _compress_body, grid=(p // CMP_PAGES,),
        in_specs=[pl.BlockSpec((CMP_PAGES, 2, NSA_KV_HEADS, HEAD_DIM, PAGE_SIZE), lambda i: (i, 0, 0, 0, 0)),
                  pl.BlockSpec((2, CMP_BLOCK, KV_WIDTH), lambda i: (0, 0, 0)), w_spec, w_spec],
        out_specs=[spec3(PAGE_CHUNKS, KV_ROW), spec3(PAGE_CHUNKS, KV_ROW)],
        out_shape=[out, out],
        scratch_shapes=[pltpu.VMEM((CMP_PAGES, 2, PAGE_SIZE, KV_WIDTH), jnp.float32)],
        compiler_params=pltpu.CompilerParams(dimension_semantics=("arbitrary",), vmem_limit_bytes=VMEM_LIMIT),
        name="cmp_compress",
    )(pages, pos, wbd[:, :CMP_STRIDE], wbd[:, CMP_STRIDE:])


def position_minor(kv):
    return jnp.transpose(kv, (0, 2, 3, 4, 1))


def compress_blocks(kv, cmp_pos, w_cmp):
    b, l = kv.shape[:2]
    n_pages = b * l // PAGE_SIZE
    pages = jnp.transpose(kv.reshape(n_pages, PAGE_SIZE, KV_ROW), (0, 2, 1))
    lo, hi = compress_pages(pages.reshape(n_pages, 2, NSA_KV_HEADS, HEAD_DIM, PAGE_SIZE), cmp_pos, w_cmp)
    lo = lo.reshape(b, l // CMP_STRIDE, 2, NSA_KV_HEADS, HEAD_DIM)
    hi = hi.reshape(b, l // CMP_STRIDE, 2, NSA_KV_HEADS, HEAD_DIM)
    return lo[:, :-1] + hi[:, 1:]


def cmp_branch(qg, t, summ):
    n = summ.shape[1]
    end = jnp.arange(n) * CMP_STRIDE + (CMP_BLOCK - 1)
    mask = end[None, :] <= t[:, None]
    s = jnp.einsum('bqhgd,bnhd->bhgqn', qg, summ[:, :, 0]) * (HEAD_DIM ** -0.5)
    p = masked_softmax(s, mask)
    o = jnp.einsum('bhgqn,bnhd->bqhgd', p.astype(qg.dtype), summ[:, :, 1])
    return o, jnp.sum(p, axis=2)


def select_blocks(p_cmp, t, n_sel):
    n_cmp = p_cmp.shape[-1]
    i = jnp.arange(n_cmp)[:, None]
    j = jnp.arange(n_sel)[None, :]
    overlap = jnp.clip(jnp.minimum(i * CMP_STRIDE + CMP_BLOCK, (j + 1) * SEL_BLOCK)
                       - jnp.maximum(i * CMP_STRIDE, j * SEL_BLOCK), 0, None)
    score = jnp.einsum('bhqn,nj->bhqj', p_cmp, overlap.astype(jnp.float32) / CMP_BLOCK)
    tb = (t // SEL_BLOCK)[:, None]
    forced = (j == 0) | (j == tb) | (j == tb - 1)
    score = jnp.where(forced, FORCE_SCORE, score)
    score = jnp.where(j <= tb, score, -1.0)
    top_s, idx = lax.top_k(score, min(SEL_TOPN, n_sel))
    return idx, top_s >= 0.0


def slc_branch(qg, t, kv_sel, idx, ok):
    b, h, q, n = idx.shape
    kpos = idx[..., None] * SEL_BLOCK + jnp.arange(SEL_BLOCK)
    mask = (ok[..., None] & (kpos <= t[None, None, :, None, None])).reshape(b, h, 1, q, n * SEL_BLOCK)
    s = jnp.einsum('bqhgd,bhqnld->bhgqnl', qg, kv_sel[..., 0, :]) * (HEAD_DIM ** -0.5)
    p = masked_softmax(s.reshape(b, h, NSA_GROUP, q, n * SEL_BLOCK), mask)
    v = kv_sel[..., 1, :].reshape(b, h, q, n * SEL_BLOCK, HEAD_DIM)
    return jnp.einsum('bhgqk,bhqkd->bqhgd', p.astype(qg.dtype), v)


def win_branch(qg, t, kv_w, kpos):
    d = t[:, None] - kpos[None, :]
    mask = (d >= 0) & (d < WINDOW) & (kpos[None, :] >= 0)
    s = jnp.einsum('bqhgd,bkhd->bhgqk', qg, kv_w[:, :, 0]) * (HEAD_DIM ** -0.5)
    p = masked_softmax(s, mask)
    return jnp.einsum('bhgqk,bkhd->bqhgd', p.astype(qg.dtype), kv_w[:, :, 1])


def gate_merge(o_c, o_s, o_w, g):
    o = o_c * g[..., 0:1] + o_s * g[..., 1:2] + o_w * g[..., 2:3]
    return o.reshape(o.shape[0], o.shape[1], NSA_WIDTH)


def nsa_sample(parts, cache_cmp, cache_slc, page_table, state_win, cmp_pos, w_cmp):
    b, n = parts[0].shape[:2]
    q, gates = parts[0], parts[4]
    kv_cmp, kv_slc, kv_win = (u.reshape(b, n, 2, NSA_KV_HEADS, HEAD_DIM) for u in parts[1:4])
    n_pages = page_table.shape[1]
    past = n_pages * PAGE_SIZE
    total = past + n
    t = past + jnp.arange(n)
    b_ix = jnp.arange(b)[:, None, None, None]
    h_ix = jnp.arange(NSA_KV_HEADS)[None, :, None, None]
    assert n <= CMP_STRIDE
    lo_pool, hi_pool = compress_pages(position_minor(cache_cmp), cmp_pos, w_cmp)
    as_chunks = lambda u: u[page_table].reshape(b, past // CMP_STRIDE, 2, NSA_KV_HEADS, HEAD_DIM)
    lo_past, hi_past = as_chunks(lo_pool), as_chunks(hi_pool)
    new_chunk = jnp.pad(kv_cmp, ((0, 0), (0, CMP_STRIDE - n), (0, 0), (0, 0), (0, 0)))
    hi_new = jnp.einsum('blchd,cldo->bcho', new_chunk + cmp_pos[None, CMP_STRIDE:, :, None, :], w_cmp[:, CMP_STRIDE:])
    summ = lo_past + jnp.concatenate([hi_past[:, 1:], hi_new[:, None]], axis=1)
    o_c, p_c = cmp_branch(q, t, summ)
    n_sel = -(-total // SEL_BLOCK)
    idx, ok = select_blocks(p_c, t, n_sel)
    past_blocks = past // SEL_BLOCK
    bpp = PAGE_SIZE // SEL_BLOCK
    phys = page_table[b_ix, jnp.minimum(idx // bpp, n_pages - 1)]
    slabs = position_minor(cache_slc)[phys, :, h_ix]
    slabs = slabs.reshape(slabs.shape[:-1] + (bpp, SEL_BLOCK))
    from_past = slabs[..., 0, :]
    for half in range(1, bpp):
        from_past = jnp.where((idx % bpp == half)[..., None, None, None], slabs[..., half, :], from_past)
    n_new_blk = -(-n // SEL_BLOCK)
    new_blocks = jnp.pad(kv_slc, ((0, 0), (0, n_new_blk * SEL_BLOCK - n), (0, 0), (0, 0), (0, 0)))
    new_blocks = new_blocks.reshape(b, n_new_blk, SEL_BLOCK, 2, NSA_KV_HEADS, HEAD_DIM)
    new_blocks = jnp.transpose(new_blocks, (0, 1, 3, 4, 5, 2))
    from_new = new_blocks[b_ix, jnp.clip(idx - past_blocks, 0, n_new_blk - 1), :, h_ix]
    kv_sel = jnp.where((idx >= past_blocks)[..., None, None, None], from_new, from_past)
    kpos = idx[..., None] * SEL_BLOCK + jnp.arange(SEL_BLOCK)
    mask = (ok[..., None] & (kpos <= t[None, None, :, None, None])).reshape(b, NSA_KV_HEADS, 1, n, -1)
    s_sel = jnp.einsum('bqhgd,bhqndl->bhgqnl', q, kv_sel[..., 0, :, :]) * (HEAD_DIM ** -0.5)
    p_sel = masked_softmax(s_sel.reshape(b, NSA_KV_HEADS, NSA_GROUP, n, -1), mask).reshape(s_sel.shape)
    o_s = jnp.einsum('bhgqnl,bhqndl->bqhgd', p_sel.astype(q.dtype), kv_sel[..., 1, :, :])
    wb = state_win.shape[1]
    kv_w = jnp.concatenate([state_win, kv_win], axis=1)
    o_w = win_branch(q, t, kv_w, past - wb + jnp.arange(wb + n))
    o = gate_merge(o_c, o_s, o_w, gates)
    return o, (kv_cmp, kv_slc, kv_w[:, n:])


PREP_ROWS = 512


def _head_sums(x):
    left = lax.broadcasted_iota(jnp.int32, (1, LANE), 1) < HEAD_DIM
    out = []
    for p in range(RWKV_PAIRS):
        xp = x[:, p * LANE:(p + 1) * LANE]
        s_l = jnp.sum(jnp.where(left, xp, 0.0), -1, keepdims=True)
        s_r = jnp.sum(jnp.where(left, 0.0, xp), -1, keepdims=True)
        out.append(jnp.where(left, s_l, s_r))
    return jnp.concatenate(out, axis=1)


def _rwkv_prep_body(z_ref, prev_ref, mu_ref, w0_ref, w2_ref, a0_ref, a2_ref, g2_ref, kk_ref, ka_ref,
                    r_out, k_out, v_out, nkk_out, dec_out, bb_out, g_out, last_sc):
    ci = pl.program_id(1)

    @pl.when(ci == 0)
    def _():
        last_sc[...] = prev_ref[0]

    zf = z_ref[0]
    first = lax.broadcasted_iota(jnp.int32, zf.shape, 0) == 0
    prev = jnp.where(first, last_sc[...], pltpu.roll(zf, 1, 0))
    last_sc[...] = zf[zf.shape[0] - 1:, :]
    zs = zf + mu_ref[...] * (prev - zf)
    r, k, v = (zs[:, i * RWKV_WIDTH:(i + 1) * RWKV_WIDTH] for i in range(3))
    wd = zs[:, RWKV_SPLITS[2]:RWKV_SPLITS[3]]
    ad = zs[:, RWKV_SPLITS[3]:RWKV_SPLITS[4]]
    gd = zs[:, RWKV_SPLITS[4]:]
    bdot = lambda u, w_ref: jnp.dot(u.astype(jnp.bfloat16), w_ref[...], preferred_element_type=jnp.float32)
    w = -jax.nn.softplus(-(w0_ref[...] + bdot(jnp.tanh(wd), w2_ref))) - 0.5
    a = jax.nn.sigmoid(a0_ref[...] + bdot(ad, a2_ref))
    kk = k * kk_ref[...]
    kk = kk / jnp.maximum(jnp.sqrt(_head_sums(kk * kk)), 1e-12)
    r_out[0] = r
    k_out[0] = k * (1.0 + (a - 1.0) * ka_ref[...])
    v_out[0] = v
    nkk_out[0] = -kk
    dec_out[0] = jnp.exp(-jnp.exp(w))
    bb_out[0] = kk * a
    g_out[0] = bdot(jax.nn.sigmoid(gd), g2_ref)


def rwkv_operands(z, prev_row, shift_mu, w0, w2, a0, a2, g2, k_k, k_a):
    b, l = z.shape[:2]
    rows = PREP_ROWS
    if l % rows:
        zf = z.astype(jnp.float32)
        prev = jnp.concatenate([prev_row[:, None].astype(jnp.float32), zf[:, :-1]], axis=1)
        zs = zf + shift_mu * (prev - zf)
        r, k, v, wd, ad, gd = jnp.split(zs, RWKV_SPLITS, axis=-1)
        w = -jax.nn.softplus(-(w0 + jnp.tanh(wd) @ w2)) - 0.5
        a = jax.nn.sigmoid(a0 + ad @ a2)
        g = jax.nn.sigmoid(gd) @ g2
        kk = (k * k_k).reshape(b, l, RWKV_HEADS, HEAD_DIM)
        kk = (kk / jnp.maximum(jnp.sqrt(jnp.sum(kk * kk, -1, keepdims=True)), 1e-12)).reshape(b, l, RWKV_WIDTH)
        return r, k * (1.0 + (a - 1.0) * k_a), v, -kk, jnp.exp(-jnp.exp(w)), kk * a, g
    row = lambda u: u.reshape(1, -1)
    const = lambda shape: pl.BlockSpec(shape, lambda bi, ci: (0,) * len(shape))
    seq = lambda width: pl.BlockSpec((1, rows, width), lambda bi, ci: (bi, ci, 0))
    out = jax.ShapeDtypeStruct((b, l, RWKV_WIDTH), jnp.float32)
    bf = lambda u: u.astype(jnp.bfloat16)
    return pl.pallas_call(
        _rwkv_prep_body, grid=(b, l // rows),
        in_specs=[seq(RWKV_COLS), pl.BlockSpec((1, 1, RWKV_COLS), lambda bi, ci: (bi, 0, 0)),
                  const((1, RWKV_COLS)), const((1, RWKV_WIDTH)), const(w2.shape), const((1, RWKV_WIDTH)),
                  const(a2.shape), const(g2.shape), const((1, RWKV_WIDTH)), const((1, RWKV_WIDTH))],
        out_specs=[seq(RWKV_WIDTH)] * 7, out_shape=[out] * 7,
        scratch_shapes=[pltpu.VMEM((1, RWKV_COLS), jnp.float32)],
        compiler_params=pltpu.CompilerParams(dimension_semantics=("arbitrary", "arbitrary"),
                                             vmem_limit_bytes=VMEM_LIMIT),
        name="rwkv_prep",
    )(z, prev_row.astype(jnp.float32).reshape(b, 1, RWKV_COLS), row(shift_mu), row(w0), bf(w2), row(a0), bf(a2), bf(g2),
      row(k_k), row(k_a))


def rwkv7_recurrence(z, prev_row, wkv0, shift_mu, w0, w2, a0, a2, g2, k_k, k_a):
    b, l = z.shape[:2]
    r, k, v, nkk, decay, bb, g = rwkv_operands(z, prev_row, shift_mu, w0, w2, a0, a2, g2, k_k, k_a)
    o, s_final = rwkv_scan(r, k, v, nkk, bb, decay, wkv0)
    return o.reshape(b, l, RWKV_WIDTH), (r, k, v, g), s_final, z[:, -1]


def rwkv7_output(o, rkvg, r_k, gn_g, gn_b):
    b, l = o.shape[:2]
    heads = lambda u: u.reshape(b, l, RWKV_HEADS, HEAD_DIM)
    r, k, v, g = rkvg
    o, r, k, v = heads(o), heads(r), heads(k), heads(v)
    mu = jnp.mean(o, -1, keepdims=True)
    var = jnp.mean(jnp.square(o - mu), -1, keepdims=True)
    o = ((o - mu) * lax.rsqrt(var + RWKV_GN_EPS)).reshape(b, l, RWKV_WIDTH) * gn_g + gn_b
    bonus = (jnp.sum(r * k * r_k, -1, keepdims=True) * v).reshape(b, l, RWKV_WIDTH)
    return (o + bonus) * g


def _mix_out_body(x_ref, oc_ref, os_ref, ow_ref, gl_ref, o_ref, r_ref, k_ref, v_ref, g_ref, rk_ref, gng_ref, gnb_ref,
                  w_ref, gf_ref, wr_ref, y_ref, yn_ref, lg_ref):
    acc = x_ref[0]
    gate = jax.nn.sigmoid(gl_ref[0])
    for h in range(NSA_KV_HEADS):
        for g in range(NSA_GROUP):
            hg = h * NSA_GROUP + g
            rows = slice(g * Q_BLOCK, (g + 1) * Q_BLOCK)
            merged = (oc_ref[0, h, 0, rows, :] * gate[:, 3 * hg:3 * hg + 1]
                      + os_ref[0, h, 0, rows, :] * gate[:, 3 * hg + 1:3 * hg + 2]
                      + ow_ref[0, h, 0, rows, :] * gate[:, 3 * hg + 2:3 * hg + 3])
            acc = acc + jnp.dot(merged.astype(jnp.bfloat16), w_ref[hg * HEAD_DIM:(hg + 1) * HEAD_DIM, :],
                                preferred_element_type=jnp.float32)
    o = o_ref[0]
    mu = _head_sums(o) * (1.0 / HEAD_DIM)
    var = _head_sums(jnp.square(o - mu)) * (1.0 / HEAD_DIM)
    o = (o - mu) * lax.rsqrt(var + RWKV_GN_EPS) * gng_ref[...] + gnb_ref[...]
    bonus = _head_sums(r_ref[0] * k_ref[0] * rk_ref[...]) * v_ref[0]
    mixed = (o + bonus) * g_ref[0]
    y = acc + jnp.dot(mixed.astype(jnp.bfloat16), w_ref[NSA_WIDTH:, :], preferred_element_type=jnp.float32)
    y_ref[0] = y
    yn = y * lax.rsqrt(jnp.mean(y * y, -1, keepdims=True) + RMS_EPS) * gf_ref[...]
    yn_ref[0] = yn.astype(jnp.bfloat16)
    lg_ref[0] = jnp.dot(yn, wr_ref[...], preferred_element_type=jnp.float32, precision=lax.Precision.HIGHEST)


def mix_out(x, nsa_tiles, gate_logits, o, rkvg, r_k, gn_g, gn_b, w_out, ffn_norm_g, w_router):
    b, s, d = x.shape
    tile = pl.BlockSpec((1, NSA_KV_HEADS, 1, QROWS, HEAD_DIM), lambda bi, qi: (bi, 0, qi, 0, 0))
    seq = lambda width: pl.BlockSpec((1, Q_BLOCK, width), lambda bi, qi: (bi, qi, 0))
    const = lambda shape: pl.BlockSpec(shape, lambda bi, qi: (0,) * len(shape))
    row = lambda u: u.reshape(1, RWKV_WIDTH)
    return pl.pallas_call(
        _mix_out_body, grid=(b, s // Q_BLOCK),
        in_specs=[seq(d), tile, tile, tile, seq(LANE)] + [seq(RWKV_WIDTH)] * 5 + [const((1, RWKV_WIDTH))] * 3
                 + [const((MIX_WIDTH, d)), const((1, d)), const((d, LANE))],
        out_specs=[seq(d), seq(d), seq(LANE)],
        out_shape=[jax.ShapeDtypeStruct((b, s, d), jnp.float32), jax.ShapeDtypeStruct((b, s, d), jnp.bfloat16),
                   jax.ShapeDtypeStruct((b, s, LANE), jnp.float32)],
        compiler_params=pltpu.CompilerParams(dimension_semantics=("arbitrary", "arbitrary"),
                                             vmem_limit_bytes=VMEM_LIMIT),
        name="mix_out",
    )(x, *nsa_tiles, gate_logits, o, *rkvg, row(r_k), row(gn_g), row(gn_b), w_out.astype(jnp.bfloat16),
      ffn_norm_g.reshape(1, d), jnp.pad(w_router, ((0, 0), (0, LANE - w_router.shape[1]))))


def moe_ffn(x, router_logits, b_router, w_gate_up, b_gate_up, w_down, b_down):
    shape = x.shape
    xt = x.reshape(-1, shape[-1])
    n_tok = xt.shape[0]
    logits = router_logits.astype(jnp.float32) + b_router.astype(jnp.float32)
    top_v, top_e = lax.top_k(logits, TOP_K)
    gates = jax.nn.softmax(top_v, axis=-1)
    flat_e = top_e.reshape(-1)
    n_assign = n_tok * TOP_K
    order = jnp.argsort(flat_e)
    rank = jnp.argsort(order)
    counts = jnp.sum(flat_e[:, None] == jnp.arange(N_EXPERTS)[None, :], axis=0)
    rows = MOE_ROWS
    padded = (counts + rows - 1) // rows * rows
    pad_end = jnp.cumsum(padded)
    pad_start = pad_end - padded
    grp_start = jnp.cumsum(counts) - counts
    n_blocks = -(-n_assign // rows) + N_EXPERTS
    n_rows = n_blocks * rows
    blk_e = jnp.minimum(jnp.sum(pad_end[None, :] <= (jnp.arange(n_blocks) * rows)[:, None], axis=1), N_EXPERTS - 1)
    row_e = jnp.repeat(blk_e, rows)
    within = jnp.arange(n_rows) - pad_start[row_e]
    src = jnp.clip(grp_start[row_e] + within, 0, n_assign - 1)
    row_token = jnp.where(within < counts[row_e], order[src] // TOP_K, n_tok).astype(jnp.int32)
    x_pad = jnp.concatenate([xt.astype(jnp.bfloat16), jnp.zeros((1, shape[-1]), jnp.bfloat16)], axis=0)
    xb = x_pad[row_token]
    n_used = (pad_end[-1] // rows).astype(jnp.int32).reshape(1)
    yb = expert_blocks(xb, blk_e.astype(jnp.int32), n_used, w_gate_up, b_gate_up, w_down, b_down)
    dest = (pad_start[flat_e] + rank - grp_start[flat_e]).reshape(n_tok, TOP_K)
    y = gates[:, 0:1] * yb[dest[:, 0]]
    for j in range(1, TOP_K):
        y = y + gates[:, j:j + 1] * yb[dest[:, j]]
    return y.reshape(shape)


def kernel(x_prompt, x_sample, cache_cmp, cache_slc, page_table, state_win, state_wkv, state_shift,
           norm_mix_g, w_in, cmp_pos, w_cmp, shift_mu, w0, w2, a0, a2, g2, k_k, k_a, r_k, gn_g, gn_b,
           w_out, norm_ffn_g, w_router, b_router, w_gate_up, b_gate_up, w_down, b_down, norm_final_g):
    xp, xs = x_prompt, x_sample
    l = 0
    rw = (shift_mu[l], w0[l], w2[l], a0[l], a2[l], g2[l], k_k[l], k_a[l], r_k[l], gn_g[l], gn_b[l])
    ffn = (w_router[l], b_router[l], w_gate_up[l], b_gate_up[l], w_down[l], b_down[l])
    bp, sp = xp.shape[:2]
    bs, ss = xs.shape[:2]
    rw_in, rw_out = rw[:8], rw[8:]
    nsa_p, zr_p, gl_p = in_proj(xp, norm_mix_g[l], w_in[l])
    tiles, (kc_p, ksl_p, kw_p) = nsa_prompt(nsa_p, cmp_pos[l], w_cmp[l])
    o_p, rkvg_p, wkv_p, sh_p = rwkv7_recurrence(zr_p, jnp.zeros((bp, RWKV_COLS), zr_p.dtype),
                                                jnp.zeros((bp, RWKV_HEADS, HEAD_DIM, HEAD_DIM), jnp.float32), *rw_in)
    xp, xpn, lg_p = mix_out(xp, tiles, gl_p, o_p, rkvg_p, *rw_out, w_out[l], norm_ffn_g[l], w_router[l])

    nsa_s, zr_s, _ = in_proj(xs, norm_mix_g[l], w_in[l])
    oa, (kc_s, ksl_s, kw_s) = nsa_sample(nsa_s, cache_cmp[l], cache_slc[l], page_table, state_win[l],
                                         cmp_pos[l], w_cmp[l])
    o_s, rkvg_s, wkv_s, sh_s = rwkv7_recurrence(zr_s, state_shift[l], state_wkv[l], *rw_in)
    ob = rwkv7_output(o_s, rkvg_s, *rw_out)
    xs = xs + jnp.concatenate([oa, ob], axis=-1) @ w_out[l]
    xsn = rms_norm(xs, norm_ffn_g[l]).reshape(bs * ss, D_MODEL)
    x_all = jnp.concatenate([xpn.reshape(bp * sp, D_MODEL), xsn.astype(jnp.bfloat16)], axis=0)
    lg_all = jnp.concatenate([lg_p.reshape(bp * sp, LANE)[:, :N_EXPERTS], xsn @ w_router[l]], axis=0)
    m_all = moe_ffn(x_all, lg_all, *ffn[1:])
    xp = xp + m_all[:bp * sp].reshape(xp.shape)
    xs = xs + m_all[bp * sp:].reshape(xs.shape)

    y_prompt = rms_norm(xp, norm_final_g)
    y_sample = rms_norm(xs, norm_final_g)
    st = lambda u: u[None]
    return (y_prompt, y_sample, st(kc_p), st(ksl_p), st(kw_p), st(wkv_p), st(sh_p),
            st(kc_s), st(ksl_s), st(kw_s), st(wkv_s), st(sh_s))
```

```python
import functools
import math

import jax
import jax.numpy as jnp
from jax import lax
from jax.experimental import pallas as pl
from jax.experimental.pallas import tpu as pltpu

D_MODEL = 1024
DEPTH = 1
PAGE_SIZE = 128

HEAD_DIM = 64
NSA_HEADS = 8
NSA_KV_HEADS = 2
NSA_GROUP = NSA_HEADS // NSA_KV_HEADS
CMP_BLOCK = 32
CMP_STRIDE = 16
SEL_BLOCK = 64
SEL_TOPN = 16
WINDOW = 512
Q_BLOCK = 128
FORCE_SCORE = 1e6
RWKV_HEADS = 8
RWKV_DECAY_LORA = 64
RWKV_A_LORA = 64
RWKV_GATE_LORA = 128
RWKV_GN_EPS = 64e-5
N_EXPERTS = 32
TOP_K = 4
D_FF = 1024
SWIGLU_LIMIT = 7.0
SWIGLU_ALPHA = 1.702
EXPERT_BLOCK = 128
RMS_EPS = 1e-5

NSA_WIDTH = NSA_HEADS * HEAD_DIM
KV_WIDTH = NSA_KV_HEADS * HEAD_DIM
RWKV_WIDTH = RWKV_HEADS * HEAD_DIM
MIX_WIDTH = NSA_WIDTH + RWKV_WIDTH
NSA_COLS = NSA_WIDTH + 6 * KV_WIDTH + 3 * NSA_HEADS
RWKV_COLS = 3 * RWKV_WIDTH + RWKV_DECAY_LORA + RWKV_A_LORA + RWKV_GATE_LORA
IN_COLS = NSA_COLS + RWKV_COLS
RWKV_SPLITS = (RWKV_WIDTH, 2 * RWKV_WIDTH, 3 * RWKV_WIDTH, 3 * RWKV_WIDTH + RWKV_DECAY_LORA,
               3 * RWKV_WIDTH + RWKV_DECAY_LORA + RWKV_A_LORA)

LANE = 128
VMEM_LIMIT = 48 * 1024 * 1024


GATE_COLS = 3 * NSA_HEADS
PROJ_GROUPS = (NSA_WIDTH, 2 * KV_WIDTH, 2 * KV_WIDTH, 2 * KV_WIDTH, RWKV_COLS, LANE)


def _in_proj_body(x_ref, g_ref, w_ref, *o_refs):
    x = x_ref[...]
    y = x * lax.rsqrt(jnp.mean(x * x, -1, keepdims=True) + RMS_EPS) * g_ref[...]
    z = jnp.dot(y.astype(jnp.bfloat16), w_ref[...], preferred_element_type=jnp.float32)
    c0 = 0
    for o_ref, width in zip(o_refs, PROJ_GROUPS):
        o_ref[...] = z[:, c0:c0 + width]
        c0 += width


def in_proj(x, g, w, block_rows=512):
    b, l, d = x.shape
    n = b * l
    gate0 = NSA_WIDTH + 6 * KV_WIDTH
    wb = jnp.concatenate([w[:, :gate0], w[:, NSA_COLS:], w[:, gate0:NSA_COLS],
                          jnp.zeros((d, LANE - GATE_COLS), w.dtype)], axis=1).astype(jnp.bfloat16)
    cols = sum(PROJ_GROUPS)
    outs = pl.pallas_call(
        _in_proj_body,
        grid=(n // block_rows,),
        in_specs=[pl.BlockSpec((block_rows, d), lambda i: (i, 0)),
                  pl.BlockSpec((1, d), lambda i: (0, 0)),
                  pl.BlockSpec((d, cols), lambda i: (0, 0))],
        out_specs=[pl.BlockSpec((block_rows, width), lambda i: (i, 0)) for width in PROJ_GROUPS],
        out_shape=[jax.ShapeDtypeStruct((n, width), jnp.float32) for width in PROJ_GROUPS],
        compiler_params=pltpu.CompilerParams(dimension_semantics=("arbitrary",), vmem_limit_bytes=VMEM_LIMIT),
        name="in_proj",
    )(x.reshape(n, d), g.reshape(1, d), wb)
    q = outs[0].reshape(b, l, NSA_KV_HEADS, NSA_GROUP, HEAD_DIM)
    kv = [o.reshape(b, l, KV_ROW) for o in outs[1:4]]
    gates = jax.nn.sigmoid(outs[5][:, :GATE_COLS]).reshape(b, l, NSA_KV_HEADS, NSA_GROUP, 3)
    return (q, kv[0], kv[1], kv[2], gates), outs[4].reshape(b, l, RWKV_COLS), outs[5].reshape(b, l, LANE)


QROWS = NSA_GROUP * Q_BLOCK
SLC_TK = 512
MASK_BIAS = -float(2 ** 30)
NEG_INIT = -3.0e38
V_AUG = 2 * HEAD_DIM

_NT = (((1,), (1,)), ((), ()))


def _cmp_select_body(q_ref, kc_ref, vc_ref, ov_ref, oc_ref, bias_ref):
    qb = pl.program_id(1)
    ncp = kc_ref.shape[2]
    nsel = ov_ref.shape[1]
    row = lax.broadcasted_iota(jnp.int32, (QROWS, ncp), 0)
    t = qb * Q_BLOCK + (row & (Q_BLOCK - 1))
    n = lax.broadcasted_iota(jnp.int32, (QROWS, ncp), 1)
    mask = n * CMP_STRIDE + (CMP_BLOCK - 1) <= t
    scores = []
    for h in range(NSA_KV_HEADS):
        q = q_ref[0, h, 0]
        s = lax.dot_general(q, kc_ref[0, h], _NT, preferred_element_type=jnp.float32)
        s = jnp.where(mask, s, -jnp.inf)
        m = jnp.max(s, -1, keepdims=True)
        m = jnp.where(m == -jnp.inf, 0.0, m)
        e = jnp.exp(s - m)
        p = e / jnp.maximum(jnp.sum(e, -1, keepdims=True), 1e-30)
        oc_ref[0, h, 0] = jnp.dot(p.astype(jnp.bfloat16), vc_ref[0, h], preferred_element_type=jnp.float32)
        pc = p[0:Q_BLOCK]
        for g in range(1, NSA_GROUP):
            pc = pc + p[g * Q_BLOCK:(g + 1) * Q_BLOCK]
        scores.append(jnp.dot(pc.astype(jnp.bfloat16), ov_ref[...], preferred_element_type=jnp.float32))
    rows = NSA_KV_HEADS * Q_BLOCK
    score = jnp.concatenate(scores, axis=0)
    j = lax.broadcasted_iota(jnp.int32, (rows, nsel), 1)
    tq = qb * Q_BLOCK + (lax.broadcasted_iota(jnp.int32, (rows, nsel), 0) & (Q_BLOCK - 1))
    tb = tq >> (SEL_BLOCK.bit_length() - 1)
    forced = (j == 0) | (j == tb) | (j == tb - 1)
    score = jnp.where(forced, FORCE_SCORE, score)
    score = jnp.where(j <= tb, score, -1.0)
    jf = j.astype(jnp.float32)
    sel = jnp.zeros((rows, nsel), jnp.bool_)
    for _ in range(min(SEL_TOPN, nsel)):
        top = jnp.max(score, -1, keepdims=True)
        first = jnp.min(jnp.where(score == top, jf, float(nsel)), -1, keepdims=True)
        pick = jf == first
        sel = sel | pick
        score = jnp.where(pick, -2.0, score)
    bias = jnp.where(sel, 0.0, MASK_BIAS).astype(jnp.bfloat16)
    for h in range(NSA_KV_HEADS):
        bias_ref[0, h] = bias[h * Q_BLOCK:(h + 1) * Q_BLOCK]


def _slc_body(q_ref, bias_ref, kaug_ref, vaug_ref, o_ref, qaug_sc, m_sc, acc_sc):
    qb = pl.program_id(2)
    start = qb * Q_BLOCK
    nsel = bias_ref.shape[3]
    bias = bias_ref[0, 0]
    for g in range(NSA_GROUP):
        qaug_sc[g * Q_BLOCK:(g + 1) * Q_BLOCK, 0:nsel] = bias
    qaug_sc[:, nsel:nsel + HEAD_DIM] = q_ref[0, 0, 0]
    m_sc[...] = jnp.full(m_sc.shape, NEG_INIT, jnp.float32)
    acc_sc[...] = jnp.zeros(acc_sc.shape, jnp.float32)

    def scores(kt):
        k0 = pl.multiple_of(kt * SLC_TK, SLC_TK)
        return lax.dot_general(qaug_sc[...], kaug_ref[0, 0, pl.ds(k0, SLC_TK), :], _NT,
                               preferred_element_type=jnp.float32)

    def absorb(kt, s, causal):
        k0 = pl.multiple_of(kt * SLC_TK, SLC_TK)
        slabs = [s[:, c * LANE:(c + 1) * LANE] for c in range(SLC_TK // LANE)]
        if causal:
            row = lax.broadcasted_iota(jnp.int32, (QROWS, LANE), 0)
            col = lax.broadcasted_iota(jnp.int32, (QROWS, LANE), 1)
            t_rel = start + (row & (Q_BLOCK - 1)) - k0 - col
            slabs = [jnp.where(t_rel >= c * LANE, sl, -jnp.inf) for c, sl in enumerate(slabs)]
        top = slabs[0]
        for sl in slabs[1:]:
            top = jnp.maximum(top, sl)
        m_prev = m_sc[...]
        m_new = jnp.maximum(m_prev, jnp.max(top, -1, keepdims=True))
        p = jnp.concatenate([jnp.exp(sl - m_new) for sl in slabs], axis=1).astype(jnp.bfloat16)
        acc_sc[...] = jnp.exp(m_prev - m_new) * acc_sc[...] + jnp.dot(
            p, vaug_ref[0, 0, pl.ds(k0, SLC_TK), :], preferred_element_type=jnp.float32)
        m_sc[...] = m_new

    n_full = start // SLC_TK

    def body(kt, s_cur):
        s_next = scores(kt + 1)
        absorb(kt, s_cur, False)
        return s_next

    s_last = lax.fori_loop(0, n_full, body, scores(0))
    absorb(n_full, s_last, True)
    acc = acc_sc[...]
    o_ref[0, 0, 0] = acc[:, :HEAD_DIM] / jnp.maximum(acc[:, HEAD_DIM:HEAD_DIM + 1], 1e-30)


def _win_body(q_ref, kpad_ref, vpad_ref, o_ref):
    qb = pl.program_id(2)
    start = pl.multiple_of(qb * Q_BLOCK, Q_BLOCK)
    nk = Q_BLOCK + WINDOW
    k = kpad_ref[0, 0, pl.ds(start, nk), :]
    s = lax.dot_general(q_ref[0, 0, 0], k, _NT, preferred_element_type=jnp.float32)
    r = lax.broadcasted_iota(jnp.int32, (QROWS, nk), 0) & (Q_BLOCK - 1)
    c = lax.broadcasted_iota(jnp.int32, (QROWS, nk), 1)
    d = r + WINDOW - c
    mask = (d >= 0) & (d < WINDOW) & (c >= WINDOW - start)
    s = jnp.where(mask, s, -jnp.inf)
    m = jnp.max(s, -1, keepdims=True)
    m = jnp.where(m == -jnp.inf, 0.0, m)
    p = jnp.exp(s - m)
    pv = jnp.dot(p.astype(jnp.bfloat16), vpad_ref[0, 0, pl.ds(start, nk), :], preferred_element_type=jnp.float32)
    o_ref[0, 0, 0] = pv[:, :HEAD_DIM] / jnp.maximum(pv[:, HEAD_DIM:HEAD_DIM + 1], 1e-30)


def _to_tiles(u, scale=None):
    b, s = u.shape[:2]
    u = u.reshape(b, s // Q_BLOCK, Q_BLOCK, NSA_KV_HEADS, NSA_GROUP, HEAD_DIM)
    u = jnp.transpose(u, (0, 3, 1, 4, 2, 5)).reshape(b, NSA_KV_HEADS, s // Q_BLOCK, QROWS, HEAD_DIM)
    return u if scale is None else (u * scale).astype(jnp.bfloat16)


def _head_major(kv):
    return jnp.transpose(kv, (0, 2, 1, 3))


def _with_ones(v):
    ones = jnp.ones(v.shape[:-1] + (1,), v.dtype)
    zeros = jnp.zeros(v.shape[:-1] + (V_AUG - HEAD_DIM - 1,), v.dtype)
    return jnp.concatenate([v, ones, zeros], -1).astype(jnp.bfloat16)


def _split_heads(kv_rows):
    piece = lambda c: jnp.stack([kv_rows[..., (c * NSA_KV_HEADS + h) * HEAD_DIM:(c * NSA_KV_HEADS + h + 1) * HEAD_DIM]
                                 for h in range(NSA_KV_HEADS)], axis=1)
    return piece(0), piece(1)


def nsa_prompt(parts, cmp_pos, w_cmp):
    q, kv_cmp, kv_slc, kv_win, gates = parts
    b, s = q.shape[:2]
    nq = s // Q_BLOCK
    nsel = s // SEL_BLOCK
    assert s % SLC_TK == 0 and nsel % LANE == 0
    summ = compress_blocks(kv_cmp, cmp_pos, w_cmp)
    n_cmp = summ.shape[1]
    ncp = -(-n_cmp // LANE) * LANE
    summ = jnp.pad(summ, ((0, 0), (0, ncp - n_cmp), (0, 0), (0, 0), (0, 0))).astype(jnp.bfloat16)
    kc, vc = _head_major(summ[:, :, 0]), _head_major(summ[:, :, 1])
    i = jnp.arange(ncp)[:, None]
    jj = jnp.arange(nsel)[None, :]
    overlap = jnp.clip(jnp.minimum(i * CMP_STRIDE + CMP_BLOCK, (jj + 1) * SEL_BLOCK)
                       - jnp.maximum(i * CMP_STRIDE, jj * SEL_BLOCK), 0, None)
    ov = (overlap.astype(jnp.float32) / CMP_BLOCK).astype(jnp.bfloat16)
    qt = _to_tiles(q, HEAD_DIM ** -0.5)

    grid = (b, NSA_KV_HEADS, nq)
    params = pltpu.CompilerParams(dimension_semantics=("arbitrary",) * 3, vmem_limit_bytes=VMEM_LIMIT)
    q_spec = pl.BlockSpec((1, 1, 1, QROWS, HEAD_DIM), lambda bi, h, qi: (bi, h, qi, 0, 0))
    o_spec = pl.BlockSpec((1, 1, 1, QROWS, HEAD_DIM), lambda bi, h, qi: (bi, h, qi, 0, 0))
    o_shape = jax.ShapeDtypeStruct((b, NSA_KV_HEADS, nq, QROWS, HEAD_DIM), jnp.float32)
    seq_spec = lambda rows, width: pl.BlockSpec((1, 1, rows, width), lambda bi, h, qi: (bi, h, 0, 0))
    bias_spec = pl.BlockSpec((1, 1, Q_BLOCK, nsel), lambda bi, h, qi: (bi, h, qi, 0))

    kvh = NSA_KV_HEADS
    both_q = pl.BlockSpec((1, kvh, 1, QROWS, HEAD_DIM), lambda bi, qi: (bi, 0, qi, 0, 0))
    both_seq = pl.BlockSpec((1, kvh, ncp, HEAD_DIM), lambda bi, qi: (bi, 0, 0, 0))
    o_c, bias = pl.pallas_call(
        _cmp_select_body, grid=(b, nq),
        in_specs=[both_q, both_seq, both_seq, pl.BlockSpec((ncp, nsel), lambda bi, qi: (0, 0))],
        out_specs=[both_q, pl.BlockSpec((1, kvh, Q_BLOCK, nsel), lambda bi, qi: (bi, 0, qi, 0))],
        out_shape=[o_shape, jax.ShapeDtypeStruct((b, kvh, s, nsel), jnp.bfloat16)],
        compiler_params=pltpu.CompilerParams(dimension_semantics=("arbitrary",) * 2, vmem_limit_bytes=VMEM_LIMIT),
        name="nsa_cmp_select",
    )(qt, kc, vc, ov)

    k_s, v_s = _split_heads(kv_slc)
    onehot = (jnp.arange(s)[:, None] // SEL_BLOCK == jnp.arange(nsel)[None, :]).astype(jnp.bfloat16)
    kaug = jnp.concatenate([jnp.broadcast_to(onehot, (b, NSA_KV_HEADS, s, nsel)), k_s.astype(jnp.bfloat16)], -1)
    o_s = pl.pallas_call(
        _slc_body, grid=grid,
        in_specs=[q_spec, bias_spec, seq_spec(s, nsel + HEAD_DIM), seq_spec(s, V_AUG)],
        out_specs=o_spec, out_shape=o_shape,
        scratch_shapes=[pltpu.VMEM((QROWS, nsel + HEAD_DIM), jnp.bfloat16),
                        pltpu.VMEM((QROWS, LANE), jnp.float32),
                        pltpu.VMEM((QROWS, V_AUG), jnp.float32)],
        compiler_params=params, name="nsa_slc",
    )(qt, bias, kaug, _with_ones(v_s))

    front = ((0, 0), (0, 0), (WINDOW, 0), (0, 0))
    k_w, v_w = _split_heads(kv_win)
    k_w = jnp.pad(k_w.astype(jnp.bfloat16), front)
    v_w = jnp.pad(_with_ones(v_w), front)
    o_w = pl.pallas_call(
        _win_body, grid=grid,
        in_specs=[q_spec, seq_spec(s + WINDOW, HEAD_DIM), seq_spec(s + WINDOW, V_AUG)],
        out_specs=o_spec, out_shape=o_shape,
        compiler_params=params, name="nsa_win",
    )(qt, k_w, v_w)

    as_rows = lambda u: u.reshape(u.shape[0], u.shape[1], 2, NSA_KV_HEADS, HEAD_DIM)
    return (o_c, o_s, o_w), (as_rows(kv_cmp), as_rows(kv_slc), as_rows(kv_win[:, s - min(WINDOW, s):]))


RWKV_PAIRS = RWKV_HEADS // 2
RWKV_TC = 256
RWKV_NB = 4
RWKV_GROUP = 8
V_PIECES = 4
V_STEPS = LANE // (V_PIECES * RWKV_HEADS)


def _rwkv_body(nkk_ref, w_ref, bb_ref, k_ref, r_ref, vt_ref, e2_ref, ebd_ref, s0_ref, o_ref, sT_ref, s_sc):
    nb, tc = nkk_ref.shape[0], nkk_ref.shape[1]
    ci = pl.program_id(1)

    @pl.when(ci == 0)
    def _():
        s_sc[...] = s0_ref[...]

    lane = lax.broadcasted_iota(jnp.int32, (1, LANE), 1)
    left = lane < HEAD_DIM
    vlane = lax.broadcasted_iota(jnp.int32, (HEAD_DIM, LANE), 1)
    vstep = (vlane >> (RWKV_HEADS.bit_length() - 1)) & (V_STEPS - 1)

    def one_step(b, t0, i, rows):
        nkk, w, bb, k, r = rows
        t = t0 + i
        row = lambda u, p: u[i:i + 1, p * LANE:(p + 1) * LANE]
        vm = jnp.where(vstep == i % V_STEPS, vt_ref[b, t0 // V_STEPS + i // V_STEPS], jnp.zeros((), jnp.bfloat16))
        vb = jnp.dot(vm, e2_ref[...], preferred_element_type=jnp.float32)
        rbd = (ebd_ref[...] * r[i:i + 1, :]).astype(jnp.bfloat16)
        o = jnp.zeros((RWKV_HEADS, HEAD_DIM), jnp.float32)
        for p in range(RWKV_PAIRS):
            s = s_sc[b, p]
            c = row(nkk, p)
            c_l = jnp.where(left, c, 0.0)
            c_r = jnp.where(left, 0.0, c)
            sa = jnp.where(left, jnp.sum(s * c_l, -1, keepdims=True), jnp.sum(s * c_r, -1, keepdims=True))
            s = s * row(w, p) + sa * row(bb, p) + vb[:, p * LANE:(p + 1) * LANE] * row(k, p)
            s_sc[b, p] = s
            o = o + lax.dot_general(rbd[:, p * LANE:(p + 1) * LANE], s.astype(jnp.bfloat16), _NT,
                                    preferred_element_type=jnp.float32)
        o_ref[b, pl.ds(pl.multiple_of(t * RWKV_HEADS, RWKV_HEADS), RWKV_HEADS), :] = o

    group = min(tc, RWKV_GROUP)

    def group_steps(gi, carry):
        t0 = pl.multiple_of(gi * group, group)
        rows = [tuple(ref[b, pl.ds(t0, group), :] for ref in (nkk_ref, w_ref, bb_ref, k_ref, r_ref)) for b in range(nb)]
        for i in range(group):
            for b in range(nb):
                one_step(b, t0, i, rows[b])
        return carry

    lax.fori_loop(0, tc // group, group_steps, 0)
    sT_ref[...] = s_sc[...]


def rwkv_scan(r, k, v, nkk, bb, decay, wkv0):
    b, l = r.shape[:2]
    tc = min(l, RWKV_TC)
    nb = RWKV_NB
    assert l % tc == 0 and tc % V_STEPS == 0 and b % nb == 0
    vh = v.reshape(b, l, RWKV_HEADS, HEAD_DIM)
    v1 = vh.astype(jnp.bfloat16)
    res = vh - v1.astype(jnp.float32)
    v2 = res.astype(jnp.bfloat16)
    v3 = (res - v2.astype(jnp.float32)).astype(jnp.bfloat16)
    pieces = jnp.stack([v1, v2, v3, jnp.zeros_like(v1)], 0)
    pieces = pieces.reshape(V_PIECES, b, l // V_STEPS, V_STEPS, RWKV_HEADS, HEAD_DIM)
    vt = jnp.transpose(pieces, (1, 2, 5, 0, 3, 4)).reshape(b, l // V_STEPS, HEAD_DIM, LANE)
    lane_head = jnp.arange(LANE) % RWKV_HEADS
    col_head = jnp.arange(RWKV_WIDTH) // HEAD_DIM
    e2 = (lane_head[:, None] == col_head[None, :]).astype(jnp.bfloat16)
    ebd = (jnp.arange(RWKV_HEADS)[:, None] == col_head[None, :]).astype(jnp.float32)
    to_pairs = lambda s: jnp.transpose(s.reshape(b, RWKV_PAIRS, 2, HEAD_DIM, HEAD_DIM), (0, 1, 3, 2, 4)
                                       ).reshape(b, RWKV_PAIRS, HEAD_DIM, LANE)
    s0 = to_pairs(wkv0.astype(jnp.float32))

    row_spec = pl.BlockSpec((nb, tc, RWKV_WIDTH), lambda bi, ci: (bi, ci, 0))
    state_spec = pl.BlockSpec((nb, RWKV_PAIRS, HEAD_DIM, LANE), lambda bi, ci: (bi, 0, 0, 0))
    o, s_t = pl.pallas_call(
        _rwkv_body, grid=(b // nb, l // tc),
        in_specs=[row_spec] * 5 + [
            pl.BlockSpec((nb, tc // V_STEPS, HEAD_DIM, LANE), lambda bi, ci: (bi, ci, 0, 0)),
            pl.BlockSpec((LANE, RWKV_WIDTH), lambda bi, ci: (0, 0)),
            pl.BlockSpec((RWKV_HEADS, RWKV_WIDTH), lambda bi, ci: (0, 0)),
            state_spec],
        out_specs=[pl.BlockSpec((nb, tc * RWKV_HEADS, HEAD_DIM), lambda bi, ci: (bi, ci, 0)), state_spec],
        out_shape=[jax.ShapeDtypeStruct((b, l * RWKV_HEADS, HEAD_DIM), jnp.float32),
                   jax.ShapeDtypeStruct((b, RWKV_PAIRS, HEAD_DIM, LANE), jnp.float32)],
        scratch_shapes=[pltpu.VMEM((nb, RWKV_PAIRS, HEAD_DIM, LANE), jnp.float32)],
        compiler_params=pltpu.CompilerParams(dimension_semantics=("arbitrary", "arbitrary"),
                                             vmem_limit_bytes=VMEM_LIMIT),
        name="rwkv_scan",
    )(nkk, decay, bb, k, r, vt, e2, ebd, s0)
    s_t = jnp.transpose(s_t.reshape(b, RWKV_PAIRS, HEAD_DIM, 2, HEAD_DIM), (0, 1, 3, 2, 4))
    return o.reshape(b, l, RWKV_HEADS, HEAD_DIM), s_t.reshape(b, RWKV_HEADS, HEAD_DIM, HEAD_DIM)


MOE_ROWS = 512


def _expert_body(blk_e_ref, n_used_ref, x_ref, wgu_ref, bgu_ref, wd_ref, bd_ref, o_ref):
    i = pl.program_id(0)

    @pl.when(i < n_used_ref[0])
    def _():
        hgu = jnp.dot(x_ref[...], wgu_ref[0], preferred_element_type=jnp.float32) + bgu_ref[0]
        gate = jnp.minimum(hgu[:, :D_FF], SWIGLU_LIMIT)
        up = jnp.clip(hgu[:, D_FF:], -SWIGLU_LIMIT, SWIGLU_LIMIT)
        act = (up + 1.0) * gate * jax.nn.sigmoid(SWIGLU_ALPHA * gate)
        o_ref[...] = jnp.dot(act.astype(jnp.bfloat16), wd_ref[0], preferred_element_type=jnp.float32) + bd_ref[0]

    @pl.when(i >= n_used_ref[0])
    def _():
        o_ref[...] = jnp.zeros(o_ref.shape, o_ref.dtype)


def expert_blocks(xb, blk_e, n_used, w_gate_up, b_gate_up, w_down, b_down):
    n_rows, d = xb.shape
    n_blocks = n_rows // MOE_ROWS
    wgu = w_gate_up.astype(jnp.bfloat16)
    wd = w_down.astype(jnp.bfloat16)
    grid_spec = pltpu.PrefetchScalarGridSpec(
        num_scalar_prefetch=2, grid=(n_blocks,),
        in_specs=[pl.BlockSpec((MOE_ROWS, d), lambda i, e, n: (i, 0)),
                  pl.BlockSpec((1, d, 2 * D_FF), lambda i, e, n: (e[i], 0, 0)),
                  pl.BlockSpec((1, 1, 2 * D_FF), lambda i, e, n: (e[i], 0, 0)),
                  pl.BlockSpec((1, D_FF, d), lambda i, e, n: (e[i], 0, 0)),
                  pl.BlockSpec((1, 1, d), lambda i, e, n: (e[i], 0, 0))],
        out_specs=pl.BlockSpec((MOE_ROWS, d), lambda i, e, n: (i, 0)))
    return pl.pallas_call(
        _expert_body, grid_spec=grid_spec,
        out_shape=jax.ShapeDtypeStruct((n_rows, d), jnp.float32),
        compiler_params=pltpu.CompilerParams(dimension_semantics=("arbitrary",), vmem_limit_bytes=VMEM_LIMIT),
        name="moe_experts",
    )(blk_e, n_used, xb, wgu, b_gate_up.reshape(N_EXPERTS, 1, 2 * D_FF), wd, b_down.reshape(N_EXPERTS, 1, d))


def rms_norm(x, g):
    xf = x.astype(jnp.float32)
    y = xf * lax.rsqrt(jnp.mean(xf * xf, -1, keepdims=True) + RMS_EPS)
    return (y * g.astype(jnp.float32)).astype(x.dtype)


def masked_softmax(s, mask):
    s = jnp.where(mask, s.astype(jnp.float32), -jnp.inf)
    m = jnp.max(s, -1, keepdims=True)
    m = jnp.where(jnp.isfinite(m), m, 0.0)
    e = jnp.where(mask, jnp.exp(s - m), 0.0)
    return e / jnp.maximum(jnp.sum(e, -1, keepdims=True), 1e-30)


CMP_PAGES = 32
PAGE_CHUNKS = PAGE_SIZE // CMP_STRIDE
KV_ROW = 2 * KV_WIDTH


def _compress_body(x_ref, pos_ref, wlo_ref, whi_ref, lo_ref, hi_ref, xt_sc):
    cp = x_ref.shape[0]

    def to_rows(pi, carry):
        for c in range(2):
            xt_sc[pi, c] = x_ref[pi, c].reshape(KV_WIDTH, PAGE_SIZE).T
        return carry

    lax.fori_loop(0, cp, to_rows, 0, unroll=4)
    for c in range(2):
        lo = jnp.zeros((cp * PAGE_CHUNKS, KV_WIDTH), jnp.float32)
        hi = jnp.zeros((cp * PAGE_CHUNKS, KV_WIDTH), jnp.float32)
        for l in range(CMP_STRIDE):
            xl = xt_sc[:, c, pl.ds(l, PAGE_CHUNKS, stride=CMP_STRIDE), :]
            xl = xl.reshape(cp * PAGE_CHUNKS, KV_WIDTH)
            lo = lo + jnp.dot((xl + pos_ref[c, l:l + 1]).astype(jnp.bfloat16), wlo_ref[c, l],
                              preferred_element_type=jnp.float32)
            hi = hi + jnp.dot((xl + pos_ref[c, CMP_STRIDE + l:CMP_STRIDE + l + 1]).astype(jnp.bfloat16), whi_ref[c, l],
                              preferred_element_type=jnp.float32)
        lo_ref[:, :, c * KV_WIDTH:(c + 1) * KV_WIDTH] = lo.reshape(cp, PAGE_CHUNKS, KV_WIDTH)
        hi_ref[:, :, c * KV_WIDTH:(c + 1) * KV_WIDTH] = hi.reshape(cp, PAGE_CHUNKS, KV_WIDTH)


def compress_pages(pages, cmp_pos, w_cmp):
    p = pages.shape[0]
    assert p % CMP_PAGES == 0
    pos = jnp.broadcast_to(jnp.transpose(cmp_pos, (1, 0, 2))[:, :, None, :],
                           (2, CMP_BLOCK, NSA_KV_HEADS, HEAD_DIM)).reshape(2, CMP_BLOCK, KV_WIDTH)
    eye = jnp.eye(NSA_KV_HEADS, dtype=jnp.float32)
    wbd = jnp.einsum('cldo,hH->clhdHo', w_cmp, eye).reshape(2, CMP_BLOCK, KV_WIDTH, KV_WIDTH).astype(jnp.bfloat16)
    out = jax.ShapeDtypeStruct((p, PAGE_CHUNKS, KV_ROW), jnp.float32)
    spec3 = lambda a, b_: pl.BlockSpec((CMP_PAGES, a, b_), lambda i: (i, 0, 0))
    w_spec = pl.BlockSpec((2, CMP_STRIDE, KV_WIDTH, KV_WIDTH), lambda i: (0, 0, 0, 0))
    return pl.pallas_call(
        _compress_body, grid=(p // CMP_PAGES,),
        in_specs=[pl.BlockSpec((CMP_PAGES, 2, NSA_KV_HEADS, HEAD_DIM, PAGE_SIZE), lambda i: (i, 0, 0, 0, 0)),
                  pl.BlockSpec((2, CMP_BLOCK, KV_WIDTH), lambda i: (0, 0, 0)), w_spec, w_spec],
        out_specs=[spec3(PAGE_CHUNKS, KV_ROW), spec3(PAGE_CHUNKS, KV_ROW)],
        out_shape=[out, out],
        scratch_shapes=[pltpu.VMEM((CMP_PAGES, 2, PAGE_SIZE, KV_WIDTH), jnp.float32)],
        compiler_params=pltpu.CompilerParams(dimension_semantics=("arbitrary",), vmem_limit_bytes=VMEM_LIMIT),
        name="cmp_compress",
    )(pages, pos, wbd[:, :CMP_STRIDE], wbd[:, CMP_STRIDE:])


def position_minor(kv):
    return jnp.transpose(kv, (0, 2, 3, 4, 1))


def compress_blocks(kv, cmp_pos, w_cmp):
    b, l = kv.shape[:2]
    n_pages = b * l // PAGE_SIZE
    pages = jnp.transpose(kv.reshape(n_pages, PAGE_SIZE, KV_ROW), (0, 2, 1))
    lo, hi = compress_pages(pages.reshape(n_pages, 2, NSA_KV_HEADS, HEAD_DIM, PAGE_SIZE), cmp_pos, w_cmp)
    lo = lo.reshape(b, l // CMP_STRIDE, 2, NSA_KV_HEADS, HEAD_DIM)
    hi = hi.reshape(b, l // CMP_STRIDE, 2, NSA_KV_HEADS, HEAD_DIM)
    return lo[:, :-1] + hi[:, 1:]


def cmp_branch(qg, t, summ):
    n = summ.shape[1]
    end = jnp.arange(n) * CMP_STRIDE + (CMP_BLOCK - 1)
    mask = end[None, :] <= t[:, None]
    s = jnp.einsum('bqhgd,bnhd->bhgqn', qg, summ[:, :, 0]) * (HEAD_DIM ** -0.5)
    p = masked_softmax(s, mask)
    o = jnp.einsum('bhgqn,bnhd->bqhgd', p.astype(qg.dtype), summ[:, :, 1])
    return o, jnp.sum(p, axis=2)


def select_blocks(p_cmp, t, n_sel):
    n_cmp = p_cmp.shape[-1]
    i = jnp.arange(n_cmp)[:, None]
    j = jnp.arange(n_sel)[None, :]
    overlap = jnp.clip(jnp.minimum(i * CMP_STRIDE + CMP_BLOCK, (j + 1) * SEL_BLOCK)
                       - jnp.maximum(i * CMP_STRIDE, j * SEL_BLOCK), 0, None)
    score = jnp.einsum('bhqn,nj->bhqj', p_cmp, overlap.astype(jnp.float32) / CMP_BLOCK)
    tb = (t // SEL_BLOCK)[:, None]
    forced = (j == 0) | (j == tb) | (j == tb - 1)
    score = jnp.where(forced, FORCE_SCORE, score)
    score = jnp.where(j <= tb, score, -1.0)
    top_s, idx = lax.top_k(score, min(SEL_TOPN, n_sel))
    return idx, top_s >= 0.0


def slc_branch(qg, t, kv_sel, idx, ok):
    b, h, q, n = idx.shape
    kpos = idx[..., None] * SEL_BLOCK + jnp.arange(SEL_BLOCK)
    mask = (ok[..., None] & (kpos <= t[None, None, :, None, None])).reshape(b, h, 1, q, n * SEL_BLOCK)
    s = jnp.einsum('bqhgd,bhqnld->bhgqnl', qg, kv_sel[..., 0, :]) * (HEAD_DIM ** -0.5)
    p = masked_softmax(s.reshape(b, h, NSA_GROUP, q, n * SEL_BLOCK), mask)
    v = kv_sel[..., 1, :].reshape(b, h, q, n * SEL_BLOCK, HEAD_DIM)
    return jnp.einsum('bhgqk,bhqkd->bqhgd', p.astype(qg.dtype), v)


def win_branch(qg, t, kv_w, kpos):
    d = t[:, None] - kpos[None, :]
    mask = (d >= 0) & (d < WINDOW) & (kpos[None, :] >= 0)
    s = jnp.einsum('bqhgd,bkhd->bhgqk', qg, kv_w[:, :, 0]) * (HEAD_DIM ** -0.5)
    p = masked_softmax(s, mask)
    return jnp.einsum('bhgqk,bkhd->bqhgd', p.astype(qg.dtype), kv_w[:, :, 1])


def gate_merge(o_c, o_s, o_w, g):
    o = o_c * g[..., 0:1] + o_s * g[..., 1:2] + o_w * g[..., 2:3]
    return o.reshape(o.shape[0], o.shape[1], NSA_WIDTH)


def nsa_sample(parts, cache_cmp, cache_slc, page_table, state_win, cmp_pos, w_cmp):
    b, n = parts[0].shape[:2]
    q, gates = parts[0], parts[4]
    kv_cmp, kv_slc, kv_win = (u.reshape(b, n, 2, NSA_KV_HEADS, HEAD_DIM) for u in parts[1:4])
    n_pages = page_table.shape[1]
    past = n_pages * PAGE_SIZE
    total = past + n
    t = past + jnp.arange(n)
    b_ix = jnp.arange(b)[:, None, None, None]
    h_ix = jnp.arange(NSA_KV_HEADS)[None, :, None, None]
    assert n <= CMP_STRIDE
    lo_pool, hi_pool = compress_pages(position_minor(cache_cmp), cmp_pos, w_cmp)
    as_chunks = lambda u: u[page_table].reshape(b, past // CMP_STRIDE, 2, NSA_KV_HEADS, HEAD_DIM)
    lo_past, hi_past = as_chunks(lo_pool), as_chunks(hi_pool)
    new_chunk = jnp.pad(kv_cmp, ((0, 0), (0, CMP_STRIDE - n), (0, 0), (0, 0), (0, 0)))
    hi_new = jnp.einsum('blchd,cldo->bcho', new_chunk + cmp_pos[None, CMP_STRIDE:, :, None, :], w_cmp[:, CMP_STRIDE:])
    summ = lo_past + jnp.concatenate([hi_past[:, 1:], hi_new[:, None]], axis=1)
    o_c, p_c = cmp_branch(q, t, summ)
    n_sel = -(-total // SEL_BLOCK)
    idx, ok = select_blocks(p_c, t, n_sel)
    past_blocks = past // SEL_BLOCK
    bpp = PAGE_SIZE // SEL_BLOCK
    phys = page_table[b_ix, jnp.minimum(idx // bpp, n_pages - 1)]
    pool = cache_slc.reshape(cache_slc.shape[0], bpp, SEL_BLOCK, 2, NSA_KV_HEADS, HEAD_DIM)
    from_past = pool[phys, idx % bpp, :, :, h_ix, :]
    n_new_blk = -(-n // SEL_BLOCK)
    new_blocks = jnp.pad(kv_slc, ((0, 0), (0, n_new_blk * SEL_BLOCK - n), (0, 0), (0, 0), (0, 0)))
    new_blocks = new_blocks.reshape(b, n_new_blk, SEL_BLOCK, 2, NSA_KV_HEADS, HEAD_DIM)
    from_new = new_blocks[b_ix, jnp.clip(idx - past_blocks, 0, n_new_blk - 1), :, :, h_ix, :]
    kv_sel = jnp.where((idx >= past_blocks)[..., None, None, None], from_new, from_past)
    o_s = slc_branch(q, t, kv_sel, idx, ok)
    wb = state_win.shape[1]
    kv_w = jnp.concatenate([state_win, kv_win], axis=1)
    o_w = win_branch(q, t, kv_w, past - wb + jnp.arange(wb + n))
    o = gate_merge(o_c, o_s, o_w, gates)
    return o, (kv_cmp, kv_slc, kv_w[:, n:])


PREP_ROWS = 512


def _head_sums(x):
    left = lax.broadcasted_iota(jnp.int32, (1, LANE), 1) < HEAD_DIM
    out = []
    for p in range(RWKV_PAIRS):
        xp = x[:, p * LANE:(p + 1) * LANE]
        s_l = jnp.sum(jnp.where(left, xp, 0.0), -1, keepdims=True)
        s_r = jnp.sum(jnp.where(left, 0.0, xp), -1, keepdims=True)
        out.append(jnp.where(left, s_l, s_r))
    return jnp.concatenate(out, axis=1)


def _rwkv_prep_body(z_ref, prev_ref, mu_ref, w0_ref, w2_ref, a0_ref, a2_ref, g2_ref, kk_ref, ka_ref,
                    r_out, k_out, v_out, nkk_out, dec_out, bb_out, g_out, last_sc):
    ci = pl.program_id(1)

    @pl.when(ci == 0)
    def _():
        last_sc[...] = prev_ref[0]

    zf = z_ref[0]
    first = lax.broadcasted_iota(jnp.int32, zf.shape, 0) == 0
    prev = jnp.where(first, last_sc[...], pltpu.roll(zf, 1, 0))
    last_sc[...] = zf[zf.shape[0] - 1:, :]
    zs = zf + mu_ref[...] * (prev - zf)
    r, k, v = (zs[:, i * RWKV_WIDTH:(i + 1) * RWKV_WIDTH] for i in range(3))
    wd = zs[:, RWKV_SPLITS[2]:RWKV_SPLITS[3]]
    ad = zs[:, RWKV_SPLITS[3]:RWKV_SPLITS[4]]
    gd = zs[:, RWKV_SPLITS[4]:]
    bdot = lambda u, w_ref: jnp.dot(u.astype(jnp.bfloat16), w_ref[...], preferred_element_type=jnp.float32)
    w = -jax.nn.softplus(-(w0_ref[...] + bdot(jnp.tanh(wd), w2_ref))) - 0.5
    a = jax.nn.sigmoid(a0_ref[...] + bdot(ad, a2_ref))
    kk = k * kk_ref[...]
    kk = kk / jnp.maximum(jnp.sqrt(_head_sums(kk * kk)), 1e-12)
    r_out[0] = r
    k_out[0] = k * (1.0 + (a - 1.0) * ka_ref[...])
    v_out[0] = v
    nkk_out[0] = -kk
    dec_out[0] = jnp.exp(-jnp.exp(w))
    bb_out[0] = kk * a
    g_out[0] = bdot(jax.nn.sigmoid(gd), g2_ref)


def rwkv_operands(z, prev_row, shift_mu, w0, w2, a0, a2, g2, k_k, k_a):
    b, l = z.shape[:2]
    rows = PREP_ROWS
    if l % rows:
        zf = z.astype(jnp.float32)
        prev = jnp.concatenate([prev_row[:, None].astype(jnp.float32), zf[:, :-1]], axis=1)
        zs = zf + shift_mu * (prev - zf)
        r, k, v, wd, ad, gd = jnp.split(zs, RWKV_SPLITS, axis=-1)
        w = -jax.nn.softplus(-(w0 + jnp.tanh(wd) @ w2)) - 0.5
        a = jax.nn.sigmoid(a0 + ad @ a2)
        g = jax.nn.sigmoid(gd) @ g2
        kk = (k * k_k).reshape(b, l, RWKV_HEADS, HEAD_DIM)
        kk = (kk / jnp.maximum(jnp.sqrt(jnp.sum(kk * kk, -1, keepdims=True)), 1e-12)).reshape(b, l, RWKV_WIDTH)
        return r, k * (1.0 + (a - 1.0) * k_a), v, -kk, jnp.exp(-jnp.exp(w)), kk * a, g
    row = lambda u: u.reshape(1, -1)
    const = lambda shape: pl.BlockSpec(shape, lambda bi, ci: (0,) * len(shape))
    seq = lambda width: pl.BlockSpec((1, rows, width), lambda bi, ci: (bi, ci, 0))
    out = jax.ShapeDtypeStruct((b, l, RWKV_WIDTH), jnp.float32)
    bf = lambda u: u.astype(jnp.bfloat16)
    return pl.pallas_call(
        _rwkv_prep_body, grid=(b, l // rows),
        in_specs=[seq(RWKV_COLS), pl.BlockSpec((1, 1, RWKV_COLS), lambda bi, ci: (bi, 0, 0)),
                  const((1, RWKV_COLS)), const((1, RWKV_WIDTH)), const(w2.shape), const((1, RWKV_WIDTH)),
                  const(a2.shape), const(g2.shape), const((1, RWKV_WIDTH)), const((1, RWKV_WIDTH))],
        out_specs=[seq(RWKV_WIDTH)] * 7, out_shape=[out] * 7,
        scratch_shapes=[pltpu.VMEM((1, RWKV_COLS), jnp.float32)],
        compiler_params=pltpu.CompilerParams(dimension_semantics=("arbitrary", "arbitrary"),
                                             vmem_limit_bytes=VMEM_LIMIT),
        name="rwkv_prep",
    )(z, prev_row.astype(jnp.float32).reshape(b, 1, RWKV_COLS), row(shift_mu), row(w0), bf(w2), row(a0), bf(a2), bf(g2),
      row(k_k), row(k_a))


def rwkv7_recurrence(z, prev_row, wkv0, shift_mu, w0, w2, a0, a2, g2, k_k, k_a):
    b, l = z.shape[:2]
    r, k, v, nkk, decay, bb, g = rwkv_operands(z, prev_row, shift_mu, w0, w2, a0, a2, g2, k_k, k_a)
    o, s_final = rwkv_scan(r, k, v, nkk, bb, decay, wkv0)
    return o.reshape(b, l, RWKV_WIDTH), (r, k, v, g), s_final, z[:, -1]


def rwkv7_output(o, rkvg, r_k, gn_g, gn_b):
    b, l = o.shape[:2]
    heads = lambda u: u.reshape(b, l, RWKV_HEADS, HEAD_DIM)
    r, k, v, g = rkvg
    o, r, k, v = heads(o), heads(r), heads(k), heads(v)
    mu = jnp.mean(o, -1, keepdims=True)
    var = jnp.mean(jnp.square(o - mu), -1, keepdims=True)
    o = ((o - mu) * lax.rsqrt(var + RWKV_GN_EPS)).reshape(b, l, RWKV_WIDTH) * gn_g + gn_b
    bonus = (jnp.sum(r * k * r_k, -1, keepdims=True) * v).reshape(b, l, RWKV_WIDTH)
    return (o + bonus) * g


def _mix_out_body(x_ref, oc_ref, os_ref, ow_ref, gl_ref, o_ref, r_ref, k_ref, v_ref, g_ref, rk_ref, gng_ref, gnb_ref,
                  w_ref, gf_ref, wr_ref, y_ref, yn_ref, lg_ref):
    acc = x_ref[0]
    gate = jax.nn.sigmoid(gl_ref[0])
    for h in range(NSA_KV_HEADS):
        for g in range(NSA_GROUP):
            hg = h * NSA_GROUP + g
            rows = slice(g * Q_BLOCK, (g + 1) * Q_BLOCK)
            merged = (oc_ref[0, h, 0, rows, :] * gate[:, 3 * hg:3 * hg + 1]
                      + os_ref[0, h, 0, rows, :] * gate[:, 3 * hg + 1:3 * hg + 2]
                      + ow_ref[0, h, 0, rows, :] * gate[:, 3 * hg + 2:3 * hg + 3])
            acc = acc + jnp.dot(merged.astype(jnp.bfloat16), w_ref[hg * HEAD_DIM:(hg + 1) * HEAD_DIM, :],
                                preferred_element_type=jnp.float32)
    o = o_ref[0]
    mu = _head_sums(o) * (1.0 / HEAD_DIM)
    var = _head_sums(jnp.square(o - mu)) * (1.0 / HEAD_DIM)
    o = (o - mu) * lax.rsqrt(var + RWKV_GN_EPS) * gng_ref[...] + gnb_ref[...]
    bonus = _head_sums(r_ref[0] * k_ref[0] * rk_ref[...]) * v_ref[0]
    mixed = (o + bonus) * g_ref[0]
    y = acc + jnp.dot(mixed.astype(jnp.bfloat16), w_ref[NSA_WIDTH:, :], preferred_element_type=jnp.float32)
    y_ref[0] = y
    yn = y * lax.rsqrt(jnp.mean(y * y, -1, keepdims=True) + RMS_EPS) * gf_ref[...]
    yn_ref[0] = yn.astype(jnp.bfloat16)
    lg_ref[0] = jnp.dot(yn, wr_ref[...], preferred_element_type=jnp.float32, precision=lax.Precision.HIGHEST)


def mix_out(x, nsa_tiles, gate_logits, o, rkvg, r_k, gn_g, gn_b, w_out, ffn_norm_g, w_router):
    b, s, d = x.shape
    tile = pl.BlockSpec((1, NSA_KV_HEADS, 1, QROWS, HEAD_DIM), lambda bi, qi: (bi, 0, qi, 0, 0))
    seq = lambda width: pl.BlockSpec((1, Q_BLOCK, width), lambda bi, qi: (bi, qi, 0))
    const = lambda shape: pl.BlockSpec(shape, lambda bi, qi: (0,) * len(shape))
    row = lambda u: u.reshape(1, RWKV_WIDTH)
    return pl.pallas_call(
        _mix_out_body, grid=(b, s // Q_BLOCK),
        in_specs=[seq(d), tile, tile, tile, seq(LANE)] + [seq(RWKV_WIDTH)] * 5 + [const((1, RWKV_WIDTH))] * 3
                 + [const((MIX_WIDTH, d)), const((1, d)), const((d, LANE))],
        out_specs=[seq(d), seq(d), seq(LANE)],
        out_shape=[jax.ShapeDtypeStruct((b, s, d), jnp.float32), jax.ShapeDtypeStruct((b, s, d), jnp.bfloat16),
                   jax.ShapeDtypeStruct((b, s, LANE), jnp.float32)],
        compiler_params=pltpu.CompilerParams(dimension_semantics=("arbitrary", "arbitrary"),
                                             vmem_limit_bytes=VMEM_LIMIT),
        name="mix_out",
    )(x, *nsa_tiles, gate_logits, o, *rkvg, row(r_k), row(gn_g), row(gn_b), w_out.astype(jnp.bfloat16),
      ffn_norm_g.reshape(1, d), jnp.pad(w_router, ((0, 0), (0, LANE - w_router.shape[1]))))


def moe_ffn(x, router_logits, b_router, w_gate_up, b_gate_up, w_down, b_down):
    shape = x.shape
    xt = x.reshape(-1, shape[-1])
    n_tok = xt.shape[0]
    logits = router_logits.astype(jnp.float32) + b_router.astype(jnp.float32)
    top_v, top_e = lax.top_k(logits, TOP_K)
    gates = jax.nn.softmax(top_v, axis=-1)
    flat_e = top_e.reshape(-1)
    n_assign = n_tok * TOP_K
    order = jnp.argsort(flat_e)
    rank = jnp.argsort(order)
    counts = jnp.sum(flat_e[:, None] == jnp.arange(N_EXPERTS)[None, :], axis=0)
    rows = MOE_ROWS
    padded = (counts + rows - 1) // rows * rows
    pad_end = jnp.cumsum(padded)
    pad_start = pad_end - padded
    grp_start = jnp.cumsum(counts) - counts
    n_blocks = -(-n_assign // rows) + N_EXPERTS
    n_rows = n_blocks * rows
    blk_e = jnp.minimum(jnp.sum(pad_end[None, :] <= (jnp.arange(n_blocks) * rows)[:, None], axis=1), N_EXPERTS - 1)
    row_e = jnp.repeat(blk_e, rows)
    within = jnp.arange(n_rows) - pad_start[row_e]
    src = jnp.clip(grp_start[row_e] + within, 0, n_assign - 1)
    row_token = jnp.where(within < counts[row_e], order[src] // TOP_K, n_tok).astype(jnp.int32)
    x_pad = jnp.concatenate([xt.astype(jnp.bfloat16), jnp.zeros((1, shape[-1]), jnp.bfloat16)], axis=0)
    xb = x_pad[row_token]
    n_used = (pad_end[-1] // rows).astype(jnp.int32).reshape(1)
    yb = expert_blocks(xb, blk_e.astype(jnp.int32), n_used, w_gate_up, b_gate_up, w_down, b_down)
    dest = (pad_start[flat_e] + rank - grp_start[flat_e]).reshape(n_tok, TOP_K)
    y = gates[:, 0:1] * yb[dest[:, 0]]
    for j in range(1, TOP_K):
        y = y + gates[:, j:j + 1] * yb[dest[:, j]]
    return y.reshape(shape)


def kernel(x_prompt, x_sample, cache_cmp, cache_slc, page_table, state_win, state_wkv, state_shift,
           norm_mix_g, w_in, cmp_pos, w_cmp, shift_mu, w0, w2, a0, a2, g2, k_k, k_a, r_k, gn_g, gn_b,
           w_out, norm_ffn_g, w_router, b_router, w_gate_up, b_gate_up, w_down, b_down, norm_final_g):
    xp, xs = x_prompt, x_sample
    l = 0
    rw = (shift_mu[l], w0[l], w2[l], a0[l], a2[l], g2[l], k_k[l], k_a[l], r_k[l], gn_g[l], gn_b[l])
    ffn = (w_router[l], b_router[l], w_gate_up[l], b_gate_up[l], w_down[l], b_down[l])
    bp, sp = xp.shape[:2]
    bs, ss = xs.shape[:2]
    rw_in, rw_out = rw[:8], rw[8:]
    nsa_p, zr_p, gl_p = in_proj(xp, norm_mix_g[l], w_in[l])
    tiles, (kc_p, ksl_p, kw_p) = nsa_prompt(nsa_p, cmp_pos[l], w_cmp[l])
    o_p, rkvg_p, wkv_p, sh_p = rwkv7_recurrence(zr_p, jnp.zeros((bp, RWKV_COLS), zr_p.dtype),
                                                jnp.zeros((bp, RWKV_HEADS, HEAD_DIM, HEAD_DIM), jnp.float32), *rw_in)
    xp, xpn, lg_p = mix_out(xp, tiles, gl_p, o_p, rkvg_p, *rw_out, w_out[l], norm_ffn_g[l], w_router[l])

    nsa_s, zr_s, _ = in_proj(xs, norm_mix_g[l], w_in[l])
    oa, (kc_s, ksl_s, kw_s) = nsa_sample(nsa_s, cache_cmp[l], cache_slc[l], page_table, state_win[l],
                                         cmp_pos[l], w_cmp[l])
    o_s, rkvg_s, wkv_s, sh_s = rwkv7_recurrence(zr_s, state_shift[l], state_wkv[l], *rw_in)
    ob = rwkv7_output(o_s, rkvg_s, *rw_out)
    xs = xs + jnp.concatenate([oa, ob], axis=-1) @ w_out[l]
    xsn = rms_norm(xs, norm_ffn_g[l]).reshape(bs * ss, D_MODEL)
    x_all = jnp.concatenate([xpn.reshape(bp * sp, D_MODEL), xsn.astype(jnp.bfloat16)], axis=0)
    lg_all = jnp.concatenate([lg_p.reshape(bp * sp, LANE)[:, :N_EXPERTS], xsn @ w_router[l]], axis=0)
    m_all = moe_ffn(x_all, lg_all, *ffn[1:])
    xp = xp + m_all[:bp * sp].reshape(xp.shape)
    xs = xs + m_all[bp * sp:].reshape(xs.shape)

    y_prompt = rms_norm(xp, norm_final_g)
    y_sample = rms_norm(xs, norm_final_g)
    st = lambda u: u[None]
    return (y_prompt, y_sample, st(kc_p), st(ksl_p), st(kw_p), st(wkv_p), st(sh_p),
            st(kc_s), st(ksl_s), st(kw_s), st(wkv_s), st(sh_s))
```

```python
import functools
import math

import jax
import jax.numpy as jnp
from jax import lax
from jax.experimental import pallas as pl
from jax.experimental.pallas import tpu as pltpu

D_MODEL = 1024
DEPTH = 1
PAGE_SIZE = 128

HEAD_DIM = 64
NSA_HEADS = 8
NSA_KV_HEADS = 2
NSA_GROUP = NSA_HEADS // NSA_KV_HEADS
CMP_BLOCK = 32
CMP_STRIDE = 16
SEL_BLOCK = 64
SEL_TOPN = 16
WINDOW = 512
Q_BLOCK = 128
FORCE_SCORE = 1e6
RWKV_HEADS = 8
RWKV_DECAY_LORA = 64
RWKV_A_LORA = 64
RWKV_GATE_LORA = 128
RWKV_GN_EPS = 64e-5
N_EXPERTS = 32
TOP_K = 4
D_FF = 1024
SWIGLU_LIMIT = 7.0
SWIGLU_ALPHA = 1.702
EXPERT_BLOCK = 128
RMS_EPS = 1e-5

NSA_WIDTH = NSA_HEADS * HEAD_DIM
KV_WIDTH = NSA_KV_HEADS * HEAD_DIM
RWKV_WIDTH = RWKV_HEADS * HEAD_DIM
MIX_WIDTH = NSA_WIDTH + RWKV_WIDTH
NSA_COLS = NSA_WIDTH + 6 * KV_WIDTH + 3 * NSA_HEADS
RWKV_COLS = 3 * RWKV_WIDTH + RWKV_DECAY_LORA + RWKV_A_LORA + RWKV_GATE_LORA
IN_COLS = NSA_COLS + RWKV_COLS
RWKV_SPLITS = (RWKV_WIDTH, 2 * RWKV_WIDTH, 3 * RWKV_WIDTH, 3 * RWKV_WIDTH + RWKV_DECAY_LORA,
               3 * RWKV_WIDTH + RWKV_DECAY_LORA + RWKV_A_LORA)

LANE = 128
VMEM_LIMIT = 48 * 1024 * 1024


GATE_COLS = 3 * NSA_HEADS
PROJ_GROUPS = (NSA_WIDTH, 2 * KV_WIDTH, 2 * KV_WIDTH, 2 * KV_WIDTH, RWKV_COLS, LANE)


def _in_proj_body(x_ref, g_ref, w_ref, *o_refs):
    x = x_ref[...]
    y = x * lax.rsqrt(jnp.mean(x * x, -1, keepdims=True) + RMS_EPS) * g_ref[...]
    z = jnp.dot(y.astype(jnp.bfloat16), w_ref[...], preferred_element_type=jnp.float32)
    c0 = 0
    for o_ref, width in zip(o_refs, PROJ_GROUPS):
        o_ref[...] = z[:, c0:c0 + width]
        c0 += width


def in_proj(x, g, w, block_rows=512):
    b, l, d = x.shape
    n = b * l
    gate0 = NSA_WIDTH + 6 * KV_WIDTH
    wb = jnp.concatenate([w[:, :gate0], w[:, NSA_COLS:], w[:, gate0:NSA_COLS],
                          jnp.zeros((d, LANE - GATE_COLS), w.dtype)], axis=1).astype(jnp.bfloat16)
    cols = sum(PROJ_GROUPS)
    outs = pl.pallas_call(
        _in_proj_body,
        grid=(n // block_rows,),
        in_specs=[pl.BlockSpec((block_rows, d), lambda i: (i, 0)),
                  pl.BlockSpec((1, d), lambda i: (0, 0)),
                  pl.BlockSpec((d, cols), lambda i: (0, 0))],
        out_specs=[pl.BlockSpec((block_rows, width), lambda i: (i, 0)) for width in PROJ_GROUPS],
        out_shape=[jax.ShapeDtypeStruct((n, width), jnp.float32) for width in PROJ_GROUPS],
        compiler_params=pltpu.CompilerParams(dimension_semantics=("arbitrary",), vmem_limit_bytes=VMEM_LIMIT),
        name="in_proj",
    )(x.reshape(n, d), g.reshape(1, d), wb)
    q = outs[0].reshape(b, l, NSA_KV_HEADS, NSA_GROUP, HEAD_DIM)
    kv = [o.reshape(b, l, KV_ROW) for o in outs[1:4]]
    gates = jax.nn.sigmoid(outs[5][:, :GATE_COLS]).reshape(b, l, NSA_KV_HEADS, NSA_GROUP, 3)
    return (q, kv[0], kv[1], kv[2], gates), outs[4].reshape(b, l, RWKV_COLS), outs[5].reshape(b, l, LANE)


QROWS = NSA_GROUP * Q_BLOCK
SLC_TK = 512
CMP_QBLOCKS = 2
MASK_BIAS = -float(2 ** 30)
NEG_INIT = -3.0e38
V_AUG = 2 * HEAD_DIM

_NT = (((1,), (1,)), ((), ()))


def _cmp_select_body(q_ref, kc_ref, vc_ref, ov_ref, oc_ref, bias_ref):
    ncp = kc_ref.shape[2]
    nsel = ov_ref.shape[1]
    row = lax.broadcasted_iota(jnp.int32, (QROWS, ncp), 0)
    n = lax.broadcasted_iota(jnp.int32, (QROWS, ncp), 1)
    scores, starts = [], []
    for u in range(CMP_QBLOCKS):
        start = (pl.program_id(1) * CMP_QBLOCKS + u) * Q_BLOCK
        mask = n * CMP_STRIDE + (CMP_BLOCK - 1) <= start + (row & (Q_BLOCK - 1))
        for h in range(NSA_KV_HEADS):
            q = q_ref[0, h, u]
            s = lax.dot_general(q, kc_ref[0, h], _NT, preferred_element_type=jnp.float32)
            s = jnp.where(mask, s, -jnp.inf)
            m = jnp.max(s, -1, keepdims=True)
            m = jnp.where(m == -jnp.inf, 0.0, m)
            e = jnp.exp(s - m)
            p = e / jnp.maximum(jnp.sum(e, -1, keepdims=True), 1e-30)
            oc_ref[0, h, u] = jnp.dot(p.astype(jnp.bfloat16), vc_ref[0, h], preferred_element_type=jnp.float32)
            pc = p[0:Q_BLOCK]
            for g in range(1, NSA_GROUP):
                pc = pc + p[g * Q_BLOCK:(g + 1) * Q_BLOCK]
            scores.append(jnp.dot(pc.astype(jnp.bfloat16), ov_ref[...], preferred_element_type=jnp.float32))
            starts.append(jnp.full((Q_BLOCK, nsel), start, jnp.int32))
    rows = len(scores) * Q_BLOCK
    score = jnp.concatenate(scores, axis=0)
    j = lax.broadcasted_iota(jnp.int32, (rows, nsel), 1)
    tq = jnp.concatenate(starts, axis=0) + (lax.broadcasted_iota(jnp.int32, (rows, nsel), 0) & (Q_BLOCK - 1))
    tb = tq >> (SEL_BLOCK.bit_length() - 1)
    forced = (j == 0) | (j == tb) | (j == tb - 1)
    score = jnp.where(forced, FORCE_SCORE, score)
    score = jnp.where(j <= tb, score, -1.0)
    jf = j.astype(jnp.float32)
    sel = jnp.zeros((rows, nsel), jnp.bool_)
    for _ in range(min(SEL_TOPN, nsel)):
        top = jnp.max(score, -1, keepdims=True)
        first = jnp.min(jnp.where(score == top, jf, float(nsel)), -1, keepdims=True)
        pick = jf == first
        sel = sel | pick
        score = jnp.where(pick, -2.0, score)
    bias = jnp.where(sel, 0.0, MASK_BIAS).astype(jnp.bfloat16)
    for u in range(CMP_QBLOCKS):
        for h in range(NSA_KV_HEADS):
            r0 = (u * NSA_KV_HEADS + h) * Q_BLOCK
            bias_ref[0, h, u * Q_BLOCK:(u + 1) * Q_BLOCK, :] = bias[r0:r0 + Q_BLOCK]


def _slc_body(q_ref, bias_ref, kaug_ref, vaug_ref, o_ref, qaug_sc, m_sc, acc_sc):
    qb = pl.program_id(2)
    start = qb * Q_BLOCK
    nsel = bias_ref.shape[3]
    bias = bias_ref[0, 0]
    for g in range(NSA_GROUP):
        qaug_sc[g * Q_BLOCK:(g + 1) * Q_BLOCK, 0:nsel] = bias
    qaug_sc[:, nsel:nsel + HEAD_DIM] = q_ref[0, 0, 0]
    m_sc[...] = jnp.full(m_sc.shape, NEG_INIT, jnp.float32)
    acc_sc[...] = jnp.zeros(acc_sc.shape, jnp.float32)

    def scores(kt):
        k0 = pl.multiple_of(kt * SLC_TK, SLC_TK)
        return lax.dot_general(qaug_sc[...], kaug_ref[0, 0, pl.ds(k0, SLC_TK), :], _NT,
                               preferred_element_type=jnp.float32)

    def absorb(kt, s, causal):
        k0 = pl.multiple_of(kt * SLC_TK, SLC_TK)
        slabs = [s[:, c * LANE:(c + 1) * LANE] for c in range(SLC_TK // LANE)]
        if causal:
            row = lax.broadcasted_iota(jnp.int32, (QROWS, LANE), 0)
            col = lax.broadcasted_iota(jnp.int32, (QROWS, LANE), 1)
            t_rel = start + (row & (Q_BLOCK - 1)) - k0 - col
            slabs = [jnp.where(t_rel >= c * LANE, sl, -jnp.inf) for c, sl in enumerate(slabs)]
        top = slabs[0]
        for sl in slabs[1:]:
            top = jnp.maximum(top, sl)
        m_prev = m_sc[...]
        m_new = jnp.maximum(m_prev, jnp.max(top, -1, keepdims=True))
        p = jnp.concatenate([jnp.exp(sl - m_new) for sl in slabs], axis=1).astype(jnp.bfloat16)
        acc_sc[...] = jnp.exp(m_prev - m_new) * acc_sc[...] + jnp.dot(
            p, vaug_ref[0, 0, pl.ds(k0, SLC_TK), :], preferred_element_type=jnp.float32)
        m_sc[...] = m_new

    n_full = start // SLC_TK

    def body(kt, s_cur):
        s_next = scores(kt + 1)
        absorb(kt, s_cur, False)
        return s_next

    s_last = lax.fori_loop(0, n_full, body, scores(0))
    absorb(n_full, s_last, True)
    acc = acc_sc[...]
    o_ref[0, 0, 0] = acc[:, :HEAD_DIM] / jnp.maximum(acc[:, HEAD_DIM:HEAD_DIM + 1], 1e-30)


def _win_body(q_ref, kpad_ref, vpad_ref, o_ref):
    qb = pl.program_id(2)
    start = pl.multiple_of(qb * Q_BLOCK, Q_BLOCK)
    nk = Q_BLOCK + WINDOW
    k = kpad_ref[0, 0, pl.ds(start, nk), :]
    s = lax.dot_general(q_ref[0, 0, 0], k, _NT, preferred_element_type=jnp.float32)
    r = lax.broadcasted_iota(jnp.int32, (QROWS, nk), 0) & (Q_BLOCK - 1)
    c = lax.broadcasted_iota(jnp.int32, (QROWS, nk), 1)
    d = r + WINDOW - c
    mask = (d >= 0) & (d < WINDOW) & (c >= WINDOW - start)
    s = jnp.where(mask, s, -jnp.inf)
    m = jnp.max(s, -1, keepdims=True)
    m = jnp.where(m == -jnp.inf, 0.0, m)
    p = jnp.exp(s - m)
    pv = jnp.dot(p.astype(jnp.bfloat16), vpad_ref[0, 0, pl.ds(start, nk), :], preferred_element_type=jnp.float32)
    o_ref[0, 0, 0] = pv[:, :HEAD_DIM] / jnp.maximum(pv[:, HEAD_DIM:HEAD_DIM + 1], 1e-30)


def _to_tiles(u, scale=None):
    b, s = u.shape[:2]
    u = u.reshape(b, s // Q_BLOCK, Q_BLOCK, NSA_KV_HEADS, NSA_GROUP, HEAD_DIM)
    u = jnp.transpose(u, (0, 3, 1, 4, 2, 5)).reshape(b, NSA_KV_HEADS, s // Q_BLOCK, QROWS, HEAD_DIM)
    return u if scale is None else (u * scale).astype(jnp.bfloat16)


def _head_major(kv):
    return jnp.transpose(kv, (0, 2, 1, 3))


def _with_ones(v):
    ones = jnp.ones(v.shape[:-1] + (1,), v.dtype)
    zeros = jnp.zeros(v.shape[:-1] + (V_AUG - HEAD_DIM - 1,), v.dtype)
    return jnp.concatenate([v, ones, zeros], -1).astype(jnp.bfloat16)


def _split_heads(kv_rows):
    piece = lambda c: jnp.stack([kv_rows[..., (c * NSA_KV_HEADS + h) * HEAD_DIM:(c * NSA_KV_HEADS + h + 1) * HEAD_DIM]
                                 for h in range(NSA_KV_HEADS)], axis=1)
    return piece(0), piece(1)


def nsa_prompt(parts, cmp_pos, w_cmp):
    q, kv_cmp, kv_slc, kv_win, gates = parts
    b, s = q.shape[:2]
    nq = s // Q_BLOCK
    nsel = s // SEL_BLOCK
    assert s % SLC_TK == 0 and nsel % LANE == 0
    summ = compress_blocks(kv_cmp, cmp_pos, w_cmp)
    n_cmp = summ.shape[1]
    ncp = -(-n_cmp // LANE) * LANE
    summ = jnp.pad(summ, ((0, 0), (0, ncp - n_cmp), (0, 0), (0, 0), (0, 0))).astype(jnp.bfloat16)
    kc, vc = _head_major(summ[:, :, 0]), _head_major(summ[:, :, 1])
    i = jnp.arange(ncp)[:, None]
    jj = jnp.arange(nsel)[None, :]
    overlap = jnp.clip(jnp.minimum(i * CMP_STRIDE + CMP_BLOCK, (jj + 1) * SEL_BLOCK)
                       - jnp.maximum(i * CMP_STRIDE, jj * SEL_BLOCK), 0, None)
    ov = (overlap.astype(jnp.float32) / CMP_BLOCK).astype(jnp.bfloat16)
    qt = _to_tiles(q, HEAD_DIM ** -0.5)

    grid = (b, NSA_KV_HEADS, nq)
    params = pltpu.CompilerParams(dimension_semantics=("arbitrary",) * 3, vmem_limit_bytes=VMEM_LIMIT)
    q_spec = pl.BlockSpec((1, 1, 1, QROWS, HEAD_DIM), lambda bi, h, qi: (bi, h, qi, 0, 0))
    o_spec = pl.BlockSpec((1, 1, 1, QROWS, HEAD_DIM), lambda bi, h, qi: (bi, h, qi, 0, 0))
    o_shape = jax.ShapeDtypeStruct((b, NSA_KV_HEADS, nq, QROWS, HEAD_DIM), jnp.float32)
    seq_spec = lambda rows, width: pl.BlockSpec((1, 1, rows, width), lambda bi, h, qi: (bi, h, 0, 0))
    bias_spec = pl.BlockSpec((1, 1, Q_BLOCK, nsel), lambda bi, h, qi: (bi, h, qi, 0))

    kvh = NSA_KV_HEADS
    assert nq % CMP_QBLOCKS == 0
    both_q = pl.BlockSpec((1, kvh, CMP_QBLOCKS, QROWS, HEAD_DIM), lambda bi, qi: (bi, 0, qi, 0, 0))
    both_seq = pl.BlockSpec((1, kvh, ncp, HEAD_DIM), lambda bi, qi: (bi, 0, 0, 0))
    o_c, bias = pl.pallas_call(
        _cmp_select_body, grid=(b, nq // CMP_QBLOCKS),
        in_specs=[both_q, both_seq, both_seq, pl.BlockSpec((ncp, nsel), lambda bi, qi: (0, 0))],
        out_specs=[both_q, pl.BlockSpec((1, kvh, CMP_QBLOCKS * Q_BLOCK, nsel), lambda bi, qi: (bi, 0, qi, 0))],
        out_shape=[o_shape, jax.ShapeDtypeStruct((b, kvh, s, nsel), jnp.bfloat16)],
        compiler_params=pltpu.CompilerParams(dimension_semantics=("arbitrary",) * 2, vmem_limit_bytes=VMEM_LIMIT),
        name="nsa_cmp_select",
    )(qt, kc, vc, ov)

    k_s, v_s = _split_heads(kv_slc)
    onehot = (jnp.arange(s)[:, None] // SEL_BLOCK == jnp.arange(nsel)[None, :]).astype(jnp.bfloat16)
    kaug = jnp.concatenate([jnp.broadcast_to(onehot, (b, NSA_KV_HEADS, s, nsel)), k_s.astype(jnp.bfloat16)], -1)
    o_s = pl.pallas_call(
        _slc_body, grid=grid,
        in_specs=[q_spec, bias_spec, seq_spec(s, nsel + HEAD_DIM), seq_spec(s, V_AUG)],
        out_specs=o_spec, out_shape=o_shape,
        scratch_shapes=[pltpu.VMEM((QROWS, nsel + HEAD_DIM), jnp.bfloat16),
                        pltpu.VMEM((QROWS, LANE), jnp.float32),
                        pltpu.VMEM((QROWS, V_AUG), jnp.float32)],
        compiler_params=params, name="nsa_slc",
    )(qt, bias, kaug, _with_ones(v_s))

    front = ((0, 0), (0, 0), (WINDOW, 0), (0, 0))
    k_w, v_w = _split_heads(kv_win)
    k_w = jnp.pad(k_w.astype(jnp.bfloat16), front)
    v_w = jnp.pad(_with_ones(v_w), front)
    o_w = pl.pallas_call(
        _win_body, grid=grid,
        in_specs=[q_spec, seq_spec(s + WINDOW, HEAD_DIM), seq_spec(s + WINDOW, V_AUG)],
        out_specs=o_spec, out_shape=o_shape,
        compiler_params=params, name="nsa_win",
    )(qt, k_w, v_w)

    as_rows = lambda u: u.reshape(u.shape[0], u.shape[1], 2, NSA_KV_HEADS, HEAD_DIM)
    return (o_c, o_s, o_w), (as_rows(kv_cmp), as_rows(kv_slc), as_rows(kv_win[:, s - min(WINDOW, s):]))


RWKV_PAIRS = RWKV_HEADS // 2
RWKV_TC = 256
RWKV_NB = 4
RWKV_GROUP = 8
V_PIECES = 4
V_STEPS = LANE // (V_PIECES * RWKV_HEADS)


def _rwkv_body(nkk_ref, w_ref, bb_ref, k_ref, r_ref, vt_ref, e2_ref, ebd_ref, s0_ref, o_ref, sT_ref, s_sc):
    nb, tc = nkk_ref.shape[0], nkk_ref.shape[1]
    ci = pl.program_id(1)

    @pl.when(ci == 0)
    def _():
        s_sc[...] = s0_ref[...]

    lane = lax.broadcasted_iota(jnp.int32, (1, LANE), 1)
    left = lane < HEAD_DIM
    vlane = lax.broadcasted_iota(jnp.int32, (HEAD_DIM, LANE), 1)
    vstep = (vlane >> (RWKV_HEADS.bit_length() - 1)) & (V_STEPS - 1)

    def one_step(b, t0, i, rows):
        nkk, w, bb, k, r = rows
        t = t0 + i
        row = lambda u, p: u[i:i + 1, p * LANE:(p + 1) * LANE]
        vm = jnp.where(vstep == i % V_STEPS, vt_ref[b, t0 // V_STEPS + i // V_STEPS], jnp.zeros((), jnp.bfloat16))
        vb = jnp.dot(vm, e2_ref[...], preferred_element_type=jnp.float32)
        rbd = (ebd_ref[...] * r[i:i + 1, :]).astype(jnp.bfloat16)
        o = jnp.zeros((RWKV_HEADS, HEAD_DIM), jnp.float32)
        for p in range(RWKV_PAIRS):
            s = s_sc[b, p]
            c = row(nkk, p)
            c_l = jnp.where(left, c, 0.0)
            c_r = jnp.where(left, 0.0, c)
            sa = jnp.where(left, jnp.sum(s * c_l, -1, keepdims=True), jnp.sum(s * c_r, -1, keepdims=True))
            s = s * row(w, p) + sa * row(bb, p) + vb[:, p * LANE:(p + 1) * LANE] * row(k, p)
            s_sc[b, p] = s
            o = o + lax.dot_general(rbd[:, p * LANE:(p + 1) * LANE], s.astype(jnp.bfloat16), _NT,
                                    preferred_element_type=jnp.float32)
        o_ref[b, pl.ds(pl.multiple_of(t * RWKV_HEADS, RWKV_HEADS), RWKV_HEADS), :] = o

    group = min(tc, RWKV_GROUP)

    def group_steps(gi, carry):
        t0 = pl.multiple_of(gi * group, group)
        rows = [tuple(ref[b, pl.ds(t0, group), :] for ref in (nkk_ref, w_ref, bb_ref, k_ref, r_ref)) for b in range(nb)]
        for i in range(group):
            for b in range(nb):
                one_step(b, t0, i, rows[b])
        return carry

    lax.fori_loop(0, tc // group, group_steps, 0)
    sT_ref[...] = s_sc[...]


def rwkv_scan(r, k, v, nkk, bb, decay, wkv0):
    b, l = r.shape[:2]
    tc = min(l, RWKV_TC)
    nb = RWKV_NB
    assert l % tc == 0 and tc % V_STEPS == 0 and b % nb == 0
    vh = v.reshape(b, l, RWKV_HEADS, HEAD_DIM)
    v1 = vh.astype(jnp.bfloat16)
    res = vh - v1.astype(jnp.float32)
    v2 = res.astype(jnp.bfloat16)
    v3 = (res - v2.astype(jnp.float32)).astype(jnp.bfloat16)
    pieces = jnp.stack([v1, v2, v3, jnp.zeros_like(v1)], 0)
    pieces = pieces.reshape(V_PIECES, b, l // V_STEPS, V_STEPS, RWKV_HEADS, HEAD_DIM)
    vt = jnp.transpose(pieces, (1, 2, 5, 0, 3, 4)).reshape(b, l // V_STEPS, HEAD_DIM, LANE)
    lane_head = jnp.arange(LANE) % RWKV_HEADS
    col_head = jnp.arange(RWKV_WIDTH) // HEAD_DIM
    e2 = (lane_head[:, None] == col_head[None, :]).astype(jnp.bfloat16)
    ebd = (jnp.arange(RWKV_HEADS)[:, None] == col_head[None, :]).astype(jnp.float32)
    to_pairs = lambda s: jnp.transpose(s.reshape(b, RWKV_PAIRS, 2, HEAD_DIM, HEAD_DIM), (0, 1, 3, 2, 4)
                                       ).reshape(b, RWKV_PAIRS, HEAD_DIM, LANE)
    s0 = to_pairs(wkv0.astype(jnp.float32))

    row_spec = pl.BlockSpec((nb, tc, RWKV_WIDTH), lambda bi, ci: (bi, ci, 0))
    state_spec = pl.BlockSpec((nb, RWKV_PAIRS, HEAD_DIM, LANE), lambda bi, ci: (bi, 0, 0, 0))
    o, s_t = pl.pallas_call(
        _rwkv_body, grid=(b // nb, l // tc),
        in_specs=[row_spec] * 5 + [
            pl.BlockSpec((nb, tc // V_STEPS, HEAD_DIM, LANE), lambda bi, ci: (bi, ci, 0, 0)),
            pl.BlockSpec((LANE, RWKV_WIDTH), lambda bi, ci: (0, 0)),
            pl.BlockSpec((RWKV_HEADS, RWKV_WIDTH), lambda bi, ci: (0, 0)),
            state_spec],
        out_specs=[pl.BlockSpec((nb, tc * RWKV_HEADS, HEAD_DIM), lambda bi, ci: (bi, ci, 0)), state_spec],
        out_shape=[jax.ShapeDtypeStruct((b, l * RWKV_HEADS, HEAD_DIM), jnp.float32),
                   jax.ShapeDtypeStruct((b, RWKV_PAIRS, HEAD_DIM, LANE), jnp.float32)],
        scratch_shapes=[pltpu.VMEM((nb, RWKV_PAIRS, HEAD_DIM, LANE), jnp.float32)],
        compiler_params=pltpu.CompilerParams(dimension_semantics=("arbitrary", "arbitrary"),
                                             vmem_limit_bytes=VMEM_LIMIT),
        name="rwkv_scan",
    )(nkk, decay, bb, k, r, vt, e2, ebd, s0)
    s_t = jnp.transpose(s_t.reshape(b, RWKV_PAIRS, HEAD_DIM, 2, HEAD_DIM), (0, 1, 3, 2, 4))
    return o.reshape(b, l, RWKV_HEADS, HEAD_DIM), s_t.reshape(b, RWKV_HEADS, HEAD_DIM, HEAD_DIM)


MOE_ROWS = 512


def _expert_body(blk_e_ref, n_used_ref, x_ref, wgu_ref, bgu_ref, wd_ref, bd_ref, o_ref):
    i = pl.program_id(0)

    @pl.when(i < n_used_ref[0])
    def _():
        hgu = jnp.dot(x_ref[...], wgu_ref[0], preferred_element_type=jnp.float32) + bgu_ref[0]
        gate = jnp.minimum(hgu[:, :D_FF], SWIGLU_LIMIT)
        up = jnp.clip(hgu[:, D_FF:], -SWIGLU_LIMIT, SWIGLU_LIMIT)
        act = (up + 1.0) * gate * jax.nn.sigmoid(SWIGLU_ALPHA * gate)
        o_ref[...] = jnp.dot(act.astype(jnp.bfloat16), wd_ref[0], preferred_element_type=jnp.float32) + bd_ref[0]

    @pl.when(i >= n_used_ref[0])
    def _():
        o_ref[...] = jnp.zeros(o_ref.shape, o_ref.dtype)


def expert_blocks(xb, blk_e, n_used, w_gate_up, b_gate_up, w_down, b_down):
    n_rows, d = xb.shape
    n_blocks = n_rows // MOE_ROWS
    wgu = w_gate_up.astype(jnp.bfloat16)
    wd = w_down.astype(jnp.bfloat16)
    grid_spec = pltpu.PrefetchScalarGridSpec(
        num_scalar_prefetch=2, grid=(n_blocks,),
        in_specs=[pl.BlockSpec((MOE_ROWS, d), lambda i, e, n: (i, 0)),
                  pl.BlockSpec((1, d, 2 * D_FF), lambda i, e, n: (e[i], 0, 0)),
                  pl.BlockSpec((1, 1, 2 * D_FF), lambda i, e, n: (e[i], 0, 0)),
                  pl.BlockSpec((1, D_FF, d), lambda i, e, n: (e[i], 0, 0)),
                  pl.BlockSpec((1, 1, d), lambda i, e, n: (e[i], 0, 0))],
        out_specs=pl.BlockSpec((MOE_ROWS, d), lambda i, e, n: (i, 0)))
    return pl.pallas_call(
        _expert_body, grid_spec=grid_spec,
        out_shape=jax.ShapeDtypeStruct((n_rows, d), jnp.float32),
        compiler_params=pltpu.CompilerParams(dimension_semantics=("arbitrary",), vmem_limit_bytes=VMEM_LIMIT),
        name="moe_experts",
    )(blk_e, n_used, xb, wgu, b_gate_up.reshape(N_EXPERTS, 1, 2 * D_FF), wd, b_down.reshape(N_EXPERTS, 1, d))


def rms_norm(x, g):
    xf = x.astype(jnp.float32)
    y = xf * lax.rsqrt(jnp.mean(xf * xf, -1, keepdims=True) + RMS_EPS)
    return (y * g.astype(jnp.float32)).astype(x.dtype)


def masked_softmax(s, mask):
    s = jnp.where(mask, s.astype(jnp.float32), -jnp.inf)
    m = jnp.max(s, -1, keepdims=True)
    m = jnp.where(jnp.isfinite(m), m, 0.0)
    e = jnp.where(mask, jnp.exp(s - m), 0.0)
    return e / jnp.maximum(jnp.sum(e, -1, keepdims=True), 1e-30)


CMP_PAGES = 32
PAGE_CHUNKS = PAGE_SIZE // CMP_STRIDE
KV_ROW = 2 * KV_WIDTH


def _compress_body(x_ref, pos_ref, wlo_ref, whi_ref, lo_ref, hi_ref, xt_sc):
    cp = x_ref.shape[0]

    def to_rows(pi, carry):
        for c in range(2):
            xt_sc[pi, c] = x_ref[pi, c].reshape(KV_WIDTH, PAGE_SIZE).T
        return carry

    lax.fori_loop(0, cp, to_rows, 0, unroll=4)
    for c in range(2):
        lo = jnp.zeros((cp * PAGE_CHUNKS, KV_WIDTH), jnp.float32)
        hi = jnp.zeros((cp * PAGE_CHUNKS, KV_WIDTH), jnp.float32)
        for l in range(CMP_STRIDE):
            xl = xt_sc[:, c, pl.ds(l, PAGE_CHUNKS, stride=CMP_STRIDE), :]
            xl = xl.reshape(cp * PAGE_CHUNKS, KV_WIDTH)
            lo = lo + jnp.dot((xl + pos_ref[c, l:l + 1]).astype(jnp.bfloat16), wlo_ref[c, l],
                              preferred_element_type=jnp.float32)
            hi = hi + jnp.dot((xl + pos_ref[c, CMP_STRIDE + l:CMP_STRIDE + l + 1]).astype(jnp.bfloat16), whi_ref[c, l],
                              preferred_element_type=jnp.float32)
        lo_ref[:, :, c * KV_WIDTH:(c + 1) * KV_WIDTH] = lo.reshape(cp, PAGE_CHUNKS, KV_WIDTH)
        hi_ref[:, :, c * KV_WIDTH:(c + 1) * KV_WIDTH] = hi.reshape(cp, PAGE_CHUNKS, KV_WIDTH)


def compress_pages(pages, cmp_pos, w_cmp):
    p = pages.shape[0]
    assert p % CMP_PAGES == 0
    pos = jnp.broadcast_to(jnp.transpose(cmp_pos, (1, 0, 2))[:, :, None, :],
                           (2, CMP_BLOCK, NSA_KV_HEADS, HEAD_DIM)).reshape(2, CMP_BLOCK, KV_WIDTH)
    eye = jnp.eye(NSA_KV_HEADS, dtype=jnp.float32)
    wbd = jnp.einsum('cldo,hH->clhdHo', w_cmp, eye).reshape(2, CMP_BLOCK, KV_WIDTH, KV_WIDTH).astype(jnp.bfloat16)
    out = jax.ShapeDtypeStruct((p, PAGE_CHUNKS, KV_ROW), jnp.float32)
    spec3 = lambda a, b_: pl.BlockSpec((CMP_PAGES, a, b_), lambda i: (i, 0, 0))
    w_spec = pl.BlockSpec((2, CMP_STRIDE, KV_WIDTH, KV_WIDTH), lambda i: (0, 0, 0, 0))
    return pl.pallas_call(
        _compress_body, grid=(p // CMP_PAGES,),
        in_specs=[pl.BlockSpec((CMP_PAGES, 2, NSA_KV_HEADS, HEAD_DIM, PAGE_SIZE), lambda i: (i, 0, 0, 0, 0)),
                  pl.BlockSpec((2, CMP_BLOCK, KV_WIDTH), lambda i: (0, 0, 0)), w_spec, w_spec],
        out_specs=[spec3(PAGE_CHUNKS, KV_ROW), spec3(PAGE_CHUNKS, KV_ROW)],
        out_shape=[out, out],
        scratch_shapes=[pltpu.VMEM((CMP_PAGES, 2, PAGE_SIZE, KV_WIDTH), jnp.float32)],
        compiler_params=pltpu.CompilerParams(dimension_semantics=("arbitrary",), vmem_limit_bytes=VMEM_LIMIT),
        name="cmp_compress",
    )(pages, pos, wbd[:, :CMP_STRIDE], wbd[:, CMP_STRIDE:])


def position_minor(kv):
    return jnp.transpose(kv, (0, 2, 3, 4, 1))


def compress_blocks(kv, cmp_pos, w_cmp):
    b, l = kv.shape[:2]
    n_pages = b * l // PAGE_SIZE
    pages = jnp.transpose(kv.reshape(n_pages, PAGE_SIZE, KV_ROW), (0, 2, 1))
    lo, hi = compress_pages(pages.reshape(n_pages, 2, NSA_KV_HEADS, HEAD_DIM, PAGE_SIZE), cmp_pos, w_cmp)
    lo = lo.reshape(b, l // CMP_STRIDE, 2, NSA_KV_HEADS, HEAD_DIM)
    hi = hi.reshape(b, l // CMP_STRIDE, 2, NSA_KV_HEADS, HEAD_DIM)
    return lo[:, :-1] + hi[:, 1:]


def cmp_branch(qg, t, summ):
    n = summ.shape[1]
    end = jnp.arange(n) * CMP_STRIDE + (CMP_BLOCK - 1)
    mask = end[None, :] <= t[:, None]
    s = jnp.einsum('bqhgd,bnhd->bhgqn', qg, summ[:, :, 0]) * (HEAD_DIM ** -0.5)
    p = masked_softmax(s, mask)
    o = jnp.einsum('bhgqn,bnhd->bqhgd', p.astype(qg.dtype), summ[:, :, 1])
    return o, jnp.sum(p, axis=2)


def select_blocks(p_cmp, t, n_sel):
    n_cmp = p_cmp.shape[-1]
    i = jnp.arange(n_cmp)[:, None]
    j = jnp.arange(n_sel)[None, :]
    overlap = jnp.clip(jnp.minimum(i * CMP_STRIDE + CMP_BLOCK, (j + 1) * SEL_BLOCK)
                       - jnp.maximum(i * CMP_STRIDE, j * SEL_BLOCK), 0, None)
    score = jnp.einsum('bhqn,nj->bhqj', p_cmp, overlap.astype(jnp.float32) / CMP_BLOCK)
    tb = (t // SEL_BLOCK)[:, None]
    forced = (j == 0) | (j == tb) | (j == tb - 1)
    score = jnp.where(forced, FORCE_SCORE, score)
    score = jnp.where(j <= tb, score, -1.0)
    top_s, idx = lax.top_k(score, min(SEL_TOPN, n_sel))
    return idx, top_s >= 0.0


def slc_branch(qg, t, kv_sel, idx, ok):
    b, h, q, n = idx.shape
    kpos = idx[..., None] * SEL_BLOCK + jnp.arange(SEL_BLOCK)
    mask = (ok[..., None] & (kpos <= t[None, None, :, None, None])).reshape(b, h, 1, q, n * SEL_BLOCK)
    s = jnp.einsum('bqhgd,bhqnld->bhgqnl', qg, kv_sel[..., 0, :]) * (HEAD_DIM ** -0.5)
    p = masked_softmax(s.reshape(b, h, NSA_GROUP, q, n * SEL_BLOCK), mask)
    v = kv_sel[..., 1, :].reshape(b, h, q, n * SEL_BLOCK, HEAD_DIM)
    return jnp.einsum('bhgqk,bhqkd->bqhgd', p.astype(qg.dtype), v)


def win_branch(qg, t, kv_w, kpos):
    d = t[:, None] - kpos[None, :]
    mask = (d >= 0) & (d < WINDOW) & (kpos[None, :] >= 0)
    s = jnp.einsum('bqhgd,bkhd->bhgqk', qg, kv_w[:, :, 0]) * (HEAD_DIM ** -0.5)
    p = masked_softmax(s, mask)
    return jnp.einsum('bhgqk,bkhd->bqhgd', p.astype(qg.dtype), kv_w[:, :, 1])


def gate_merge(o_c, o_s, o_w, g):
    o = o_c * g[..., 0:1] + o_s * g[..., 1:2] + o_w * g[..., 2:3]
    return o.reshape(o.shape[0], o.shape[1], NSA_WIDTH)


def nsa_sample(parts, cache_cmp, cache_slc, page_table, state_win, cmp_pos, w_cmp):
    b, n = parts[0].shape[:2]
    q, gates = parts[0], parts[4]
    kv_cmp, kv_slc, kv_win = (u.reshape(b, n, 2, NSA_KV_HEADS, HEAD_DIM) for u in parts[1:4])
    n_pages = page_table.shape[1]
    past = n_pages * PAGE_SIZE
    total = past + n
    t = past + jnp.arange(n)
    b_ix = jnp.arange(b)[:, None, None, None]
    h_ix = jnp.arange(NSA_KV_HEADS)[None, :, None, None]
    assert n <= CMP_STRIDE
    lo_pool, hi_pool = compress_pages(position_minor(cache_cmp), cmp_pos, w_cmp)
    as_chunks = lambda u: u[page_table].reshape(b, past // CMP_STRIDE, 2, NSA_KV_HEADS, HEAD_DIM)
    lo_past, hi_past = as_chunks(lo_pool), as_chunks(hi_pool)
    new_chunk = jnp.pad(kv_cmp, ((0, 0), (0, CMP_STRIDE - n), (0, 0), (0, 0), (0, 0)))
    hi_new = jnp.einsum('blchd,cldo->bcho', new_chunk + cmp_pos[None, CMP_STRIDE:, :, None, :], w_cmp[:, CMP_STRIDE:])
    summ = lo_past + jnp.concatenate([hi_past[:, 1:], hi_new[:, None]], axis=1)
    o_c, p_c = cmp_branch(q, t, summ)
    n_sel = -(-total // SEL_BLOCK)
    idx, ok = select_blocks(p_c, t, n_sel)
    past_blocks = past // SEL_BLOCK
    bpp = PAGE_SIZE // SEL_BLOCK
    phys = page_table[b_ix, jnp.minimum(idx // bpp, n_pages - 1)]
    pool = cache_slc.reshape(cache_slc.shape[0], bpp, SEL_BLOCK, 2, NSA_KV_HEADS, HEAD_DIM)
    from_past = pool[phys, idx % bpp, :, :, h_ix, :]
    n_new_blk = -(-n // SEL_BLOCK)
    new_blocks = jnp.pad(kv_slc, ((0, 0), (0, n_new_blk * SEL_BLOCK - n), (0, 0), (0, 0), (0, 0)))
    new_blocks = new_blocks.reshape(b, n_new_blk, SEL_BLOCK, 2, NSA_KV_HEADS, HEAD_DIM)
    from_new = new_blocks[b_ix, jnp.clip(idx - past_blocks, 0, n_new_blk - 1), :, :, h_ix, :]
    kv_sel = jnp.where((idx >= past_blocks)[..., None, None, None], from_new, from_past)
    o_s = slc_branch(q, t, kv_sel, idx, ok)
    wb = state_win.shape[1]
    kv_w = jnp.concatenate([state_win, kv_win], axis=1)
    o_w = win_branch(q, t, kv_w, past - wb + jnp.arange(wb + n))
    o = gate_merge(o_c, o_s, o_w, gates)
    return o, (kv_cmp, kv_slc, kv_w[:, n:])


PREP_ROWS = 512


def _head_sums(x):
    left = lax.broadcasted_iota(jnp.int32, (1, LANE), 1) < HEAD_DIM
    out = []
    for p in range(RWKV_PAIRS):
        xp = x[:, p * LANE:(p + 1) * LANE]
        s_l = jnp.sum(jnp.where(left, xp, 0.0), -1, keepdims=True)
        s_r = jnp.sum(jnp.where(left, 0.0, xp), -1, keepdims=True)
        out.append(jnp.where(left, s_l, s_r))
    return jnp.concatenate(out, axis=1)


def _rwkv_prep_body(z_ref, prev_ref, mu_ref, w0_ref, w2_ref, a0_ref, a2_ref, g2_ref, kk_ref, ka_ref,
                    r_out, k_out, v_out, nkk_out, dec_out, bb_out, g_out, last_sc):
    ci = pl.program_id(1)

    @pl.when(ci == 0)
    def _():
        last_sc[...] = prev_ref[0]

    zf = z_ref[0]
    first = lax.broadcasted_iota(jnp.int32, zf.shape, 0) == 0
    prev = jnp.where(first, last_sc[...], pltpu.roll(zf, 1, 0))
    last_sc[...] = zf[zf.shape[0] - 1:, :]
    zs = zf + mu_ref[...] * (prev - zf)
    r, k, v = (zs[:, i * RWKV_WIDTH:(i + 1) * RWKV_WIDTH] for i in range(3))
    wd = zs[:, RWKV_SPLITS[2]:RWKV_SPLITS[3]]
    ad = zs[:, RWKV_SPLITS[3]:RWKV_SPLITS[4]]
    gd = zs[:, RWKV_SPLITS[4]:]
    bdot = lambda u, w_ref: jnp.dot(u.astype(jnp.bfloat16), w_ref[...], preferred_element_type=jnp.float32)
    w = -jax.nn.softplus(-(w0_ref[...] + bdot(jnp.tanh(wd), w2_ref))) - 0.5
    a = jax.nn.sigmoid(a0_ref[...] + bdot(ad, a2_ref))
    kk = k * kk_ref[...]
    kk = kk / jnp.maximum(jnp.sqrt(_head_sums(kk * kk)), 1e-12)
    r_out[0] = r
    k_out[0] = k * (1.0 + (a - 1.0) * ka_ref[...])
    v_out[0] = v
    nkk_out[0] = -kk
    dec_out[0] = jnp.exp(-jnp.exp(w))
    bb_out[0] = kk * a
    g_out[0] = bdot(jax.nn.sigmoid(gd), g2_ref)


def rwkv_operands(z, prev_row, shift_mu, w0, w2, a0, a2, g2, k_k, k_a):
    b, l = z.shape[:2]
    rows = PREP_ROWS
    if l % rows:
        zf = z.astype(jnp.float32)
        prev = jnp.concatenate([prev_row[:, None].astype(jnp.float32), zf[:, :-1]], axis=1)
        zs = zf + shift_mu * (prev - zf)
        r, k, v, wd, ad, gd = jnp.split(zs, RWKV_SPLITS, axis=-1)
        w = -jax.nn.softplus(-(w0 + jnp.tanh(wd) @ w2)) - 0.5
        a = jax.nn.sigmoid(a0 + ad @ a2)
        g = jax.nn.sigmoid(gd) @ g2
        kk = (k * k_k).reshape(b, l, RWKV_HEADS, HEAD_DIM)
        kk = (kk / jnp.maximum(jnp.sqrt(jnp.sum(kk * kk, -1, keepdims=True)), 1e-12)).reshape(b, l, RWKV_WIDTH)
        return r, k * (1.0 + (a - 1.0) * k_a), v, -kk, jnp.exp(-jnp.exp(w)), kk * a, g
    row = lambda u: u.reshape(1, -1)
    const = lambda shape: pl.BlockSpec(shape, lambda bi, ci: (0,) * len(shape))
    seq = lambda width: pl.BlockSpec((1, rows, width), lambda bi, ci: (bi, ci, 0))
    out = jax.ShapeDtypeStruct((b, l, RWKV_WIDTH), jnp.float32)
    bf = lambda u: u.astype(jnp.bfloat16)
    return pl.pallas_call(
        _rwkv_prep_body, grid=(b, l // rows),
        in_specs=[seq(RWKV_COLS), pl.BlockSpec((1, 1, RWKV_COLS), lambda bi, ci: (bi, 0, 0)),
                  const((1, RWKV_COLS)), const((1, RWKV_WIDTH)), const(w2.shape), const((1, RWKV_WIDTH)),
                  const(a2.shape), const(g2.shape), const((1, RWKV_WIDTH)), const((1, RWKV_WIDTH))],
        out_specs=[seq(RWKV_WIDTH)] * 7, out_shape=[out] * 7,
        scratch_shapes=[pltpu.VMEM((1, RWKV_COLS), jnp.float32)],
        compiler_params=pltpu.CompilerParams(dimension_semantics=("arbitrary", "arbitrary"),
                                             vmem_limit_bytes=VMEM_LIMIT),
        name="rwkv_prep",
    )(z, prev_row.astype(jnp.float32).reshape(b, 1, RWKV_COLS), row(shift_mu), row(w0), bf(w2), row(a0), bf(a2), bf(g2),
      row(k_k), row(k_a))


def rwkv7_recurrence(z, prev_row, wkv0, shift_mu, w0, w2, a0, a2, g2, k_k, k_a):
    b, l = z.shape[:2]
    r, k, v, nkk, decay, bb, g = rwkv_operands(z, prev_row, shift_mu, w0, w2, a0, a2, g2, k_k, k_a)
    o, s_final = rwkv_scan(r, k, v, nkk, bb, decay, wkv0)
    return o.reshape(b, l, RWKV_WIDTH), (r, k, v, g), s_final, z[:, -1]


def rwkv7_output(o, rkvg, r_k, gn_g, gn_b):
    b, l = o.shape[:2]
    heads = lambda u: u.reshape(b, l, RWKV_HEADS, HEAD_DIM)
    r, k, v, g = rkvg
    o, r, k, v = heads(o), heads(r), heads(k), heads(v)
    mu = jnp.mean(o, -1, keepdims=True)
    var = jnp.mean(jnp.square(o - mu), -1, keepdims=True)
    o = ((o - mu) * lax.rsqrt(var + RWKV_GN_EPS)).reshape(b, l, RWKV_WIDTH) * gn_g + gn_b
    bonus = (jnp.sum(r * k * r_k, -1, keepdims=True) * v).reshape(b, l, RWKV_WIDTH)
    return (o + bonus) * g


def _mix_out_body(x_ref, oc_ref, os_ref, ow_ref, gl_ref, o_ref, r_ref, k_ref, v_ref, g_ref, rk_ref, gng_ref, gnb_ref,
                  w_ref, y_ref):
    acc = x_ref[0]
    gate = jax.nn.sigmoid(gl_ref[0])
    for h in range(NSA_KV_HEADS):
        for g in range(NSA_GROUP):
            hg = h * NSA_GROUP + g
            rows = slice(g * Q_BLOCK, (g + 1) * Q_BLOCK)
            merged = (oc_ref[0, h, 0, rows, :] * gate[:, 3 * hg:3 * hg + 1]
                      + os_ref[0, h, 0, rows, :] * gate[:, 3 * hg + 1:3 * hg + 2]
                      + ow_ref[0, h, 0, rows, :] * gate[:, 3 * hg + 2:3 * hg + 3])
            acc = acc + jnp.dot(merged.astype(jnp.bfloat16), w_ref[hg * HEAD_DIM:(hg + 1) * HEAD_DIM, :],
                                preferred_element_type=jnp.float32)
    o = o_ref[0]
    mu = _head_sums(o) * (1.0 / HEAD_DIM)
    var = _head_sums(jnp.square(o - mu)) * (1.0 / HEAD_DIM)
    o = (o - mu) * lax.rsqrt(var + RWKV_GN_EPS) * gng_ref[...] + gnb_ref[...]
    bonus = _head_sums(r_ref[0] * k_ref[0] * rk_ref[...]) * v_ref[0]
    mixed = (o + bonus) * g_ref[0]
    y_ref[0] = acc + jnp.dot(mixed.astype(jnp.bfloat16), w_ref[NSA_WIDTH:, :], preferred_element_type=jnp.float32)


def mix_out(x, nsa_tiles, gate_logits, o, rkvg, r_k, gn_g, gn_b, w_out):
    b, s, d = x.shape
    tile = pl.BlockSpec((1, NSA_KV_HEADS, 1, QROWS, HEAD_DIM), lambda bi, qi: (bi, 0, qi, 0, 0))
    seq = lambda width: pl.BlockSpec((1, Q_BLOCK, width), lambda bi, qi: (bi, qi, 0))
    const = lambda shape: pl.BlockSpec(shape, lambda bi, qi: (0,) * len(shape))
    row = lambda u: u.reshape(1, RWKV_WIDTH)
    return pl.pallas_call(
        _mix_out_body, grid=(b, s // Q_BLOCK),
        in_specs=[seq(d), tile, tile, tile, seq(LANE)] + [seq(RWKV_WIDTH)] * 5 + [const((1, RWKV_WIDTH))] * 3
                 + [const((MIX_WIDTH, d))],
        out_specs=seq(d), out_shape=jax.ShapeDtypeStruct((b, s, d), jnp.float32),
        compiler_params=pltpu.CompilerParams(dimension_semantics=("arbitrary", "arbitrary"),
                                             vmem_limit_bytes=VMEM_LIMIT),
        name="mix_out",
    )(x, *nsa_tiles, gate_logits, o, *rkvg, row(r_k), row(gn_g), row(gn_b), w_out.astype(jnp.bfloat16))


def moe_ffn(x, w_router, b_router, w_gate_up, b_gate_up, w_down, b_down):
    shape = x.shape
    xt = x.reshape(-1, shape[-1])
    n_tok = xt.shape[0]
    logits = (xt @ w_router).astype(jnp.float32) + b_router.astype(jnp.float32)
    top_v, top_e = lax.top_k(logits, TOP_K)
    gates = jax.nn.softmax(top_v, axis=-1)
    flat_e = top_e.reshape(-1)
    n_assign = n_tok * TOP_K
    order = jnp.argsort(flat_e)
    rank = jnp.argsort(order)
    counts = jnp.sum(flat_e[:, None] == jnp.arange(N_EXPERTS)[None, :], axis=0)
    rows = MOE_ROWS
    padded = (counts + rows - 1) // rows * rows
    pad_end = jnp.cumsum(padded)
    pad_start = pad_end - padded
    grp_start = jnp.cumsum(counts) - counts
    n_blocks = -(-n_assign // rows) + N_EXPERTS
    n_rows = n_blocks * rows
    blk_e = jnp.minimum(jnp.sum(pad_end[None, :] <= (jnp.arange(n_blocks) * rows)[:, None], axis=1), N_EXPERTS - 1)
    row_e = jnp.repeat(blk_e, rows)
    within = jnp.arange(n_rows) - pad_start[row_e]
    src = jnp.clip(grp_start[row_e] + within, 0, n_assign - 1)
    row_token = jnp.where(within < counts[row_e], order[src] // TOP_K, n_tok).astype(jnp.int32)
    x_pad = jnp.concatenate([xt.astype(jnp.bfloat16), jnp.zeros((1, shape[-1]), jnp.bfloat16)], axis=0)
    xb = x_pad[row_token]
    n_used = (pad_end[-1] // rows).astype(jnp.int32).reshape(1)
    yb = expert_blocks(xb, blk_e.astype(jnp.int32), n_used, w_gate_up, b_gate_up, w_down, b_down)
    dest = (pad_start[flat_e] + rank - grp_start[flat_e]).reshape(n_tok, TOP_K)
    y = gates[:, 0:1] * yb[dest[:, 0]]
    for j in range(1, TOP_K):
        y = y + gates[:, j:j + 1] * yb[dest[:, j]]
    return y.reshape(shape)


def kernel(x_prompt, x_sample, cache_cmp, cache_slc, page_table, state_win, state_wkv, state_shift,
           norm_mix_g, w_in, cmp_pos, w_cmp, shift_mu, w0, w2, a0, a2, g2, k_k, k_a, r_k, gn_g, gn_b,
           w_out, norm_ffn_g, w_router, b_router, w_gate_up, b_gate_up, w_down, b_down, norm_final_g):
    xp, xs = x_prompt, x_sample
    l = 0
    rw = (shift_mu[l], w0[l], w2[l], a0[l], a2[l], g2[l], k_k[l], k_a[l], r_k[l], gn_g[l], gn_b[l])
    ffn = (w_router[l], b_router[l], w_gate_up[l], b_gate_up[l], w_down[l], b_down[l])
    bp, sp = xp.shape[:2]
    bs, ss = xs.shape[:2]
    rw_in, rw_out = rw[:8], rw[8:]
    nsa_p, zr_p, gl_p = in_proj(xp, norm_mix_g[l], w_in[l])
    tiles, (kc_p, ksl_p, kw_p) = nsa_prompt(nsa_p, cmp_pos[l], w_cmp[l])
    o_p, rkvg_p, wkv_p, sh_p = rwkv7_recurrence(zr_p, jnp.zeros((bp, RWKV_COLS), zr_p.dtype),
                                                jnp.zeros((bp, RWKV_HEADS, HEAD_DIM, HEAD_DIM), jnp.float32), *rw_in)
    xp = mix_out(xp, tiles, gl_p, o_p, rkvg_p, *rw_out, w_out[l])

    nsa_s, zr_s, _ = in_proj(xs, norm_mix_g[l], w_in[l])
    oa, (kc_s, ksl_s, kw_s) = nsa_sample(nsa_s, cache_cmp[l], cache_slc[l], page_table, state_win[l],
                                         cmp_pos[l], w_cmp[l])
    o_s, rkvg_s, wkv_s, sh_s = rwkv7_recurrence(zr_s, state_shift[l], state_wkv[l], *rw_in)
    ob = rwkv7_output(o_s, rkvg_s, *rw_out)
    xs = xs + jnp.concatenate([oa, ob], axis=-1) @ w_out[l]
    x_all = jnp.concatenate([xp.reshape(bp * sp, D_MODEL), xs.reshape(bs * ss, D_MODEL)], axis=0)
    m_all = moe_ffn(rms_norm(x_all, norm_ffn_g[l]), *ffn)
    xp = xp + m_all[:bp * sp].reshape(xp.shape)
    xs = xs + m_all[bp * sp:].reshape(xs.shape)

    y_prompt = rms_norm(xp, norm_final_g)
    y_sample = rms_norm(xs, norm_final_g)
    st = lambda u: u[None]
    return (y_prompt, y_sample, st(kc_p), st(ksl_p), st(kw_p), st(wkv_p), st(sh_p),
            st(kc_s), st(ksl_s), st(kw_s), st(wkv_s), st(sh_s))
```
